```python
import math
import jax, jax.numpy as jnp
from jax import lax
import numpy as np

D_MODEL = 2048
BATCH = 4
SEQ = 2048
DEPTH = 4
DEC_BATCH = 8
DEC_SEQ = 4
PAST_LEN = 16384
PAGE_SIZE = 128

N_MIXERS = 3
N_A_LAYERS = (DEPTH + 2) // 3
N_B_LAYERS = (DEPTH + 1) // 3
N_C_LAYERS = DEPTH // 3
SSM_GROUP = 16
SSM_GROUPS = D_MODEL // SSM_GROUP
SSM_STATE = 64
DT_MIN = 1e-3
DT_MAX = 1e-1
CONV_W = 3
N_HEADS = 16
HEAD_DIM = D_MODEL // N_HEADS
Q_BLOCK = 128
SB_BIAS_INIT = -6.0
D_FF = 4 * D_MODEL
EPS = 1e-6

kernel_name = "hybrid_s5_shortconv_stickbreaking_step"


def rms_norm(x, g):
    x32 = x.astype(jnp.float32)
    y = x32 * lax.rsqrt(jnp.mean(x32 * x32, axis=-1, keepdims=True) + EPS)
    return (y * g.astype(jnp.float32)).astype(x.dtype)


def _scan_combine(e1, e2):
    a1, b1 = e1
    a2, b2 = e2
    return a1 * a2, a2 * b1 + b2


def s5_mixer(u, h0_re, h0_im, lam_re, lam_im, log_dt, b_re, b_im, c_re, c_im, d, w_glu):
    f32 = jnp.float32
    n, t, _ = u.shape
    lam = lax.complex(lam_re.astype(f32), lam_im.astype(f32))
    dt = jnp.exp(log_dt.astype(f32))[:, None]
    a_bar = jnp.exp(lam * dt)
    b = lax.complex(b_re.astype(f32), b_im.astype(f32))
    b_bar = ((a_bar - 1.0) / lam)[..., None] * b
    u32 = u.astype(f32)
    ug = u32.reshape(n, t, SSM_GROUPS, SSM_GROUP).astype(jnp.complex64)
    bu = jnp.einsum("ntgc,gpc->ntgp", ug, b_bar)
    a_elems = jnp.broadcast_to(a_bar[None, None], (1, t) + a_bar.shape)
    a_cum, h = lax.associative_scan(_scan_combine, (a_elems, bu), axis=1)
    h0 = lax.complex(h0_re.astype(f32), h0_im.astype(f32))[:, None]
    h = h + a_cum * h0
    c = lax.complex(c_re.astype(f32), c_im.astype(f32))
    y = jnp.real(jnp.einsum("ntgp,gcp->ntgc", h, c)).reshape(n, t, D_MODEL)
    y = y + d.astype(f32) * u32
    z = jax.nn.gelu(y).astype(u.dtype)
    a_half, g_half = jnp.split(z @ w_glu, 2, axis=-1)
    out = a_half * jax.nn.sigmoid(g_half)
    h_last = h[:, -1]
    return out, jnp.real(h_last), jnp.imag(h_last)


def short_conv_mixer(x, buf, w_in, w_dw, w_out):
    t = x.shape[1]
    gate_b, gate_c, v = jnp.split(x @ w_in, 3, axis=-1)
    cv = gate_c * v
    full = jnp.concatenate([buf.astype(cv.dtype), cv], axis=1)
    conv = sum(full[:, k:k + t] * w_dw[k] for k in range(CONV_W))
    y = (gate_b * conv) @ w_out
    return y, full[:, -(CONV_W - 1):]


def qkv_heads(x, w_qkv, q_gain, k_gain):
    n, t, _ = x.shape
    qkv = (x @ w_qkv).reshape(n, t, 3, N_HEADS, HEAD_DIM)
    q = rms_norm(qkv[:, :, 0], q_gain)
    k = rms_norm(qkv[:, :, 1], k_gain)
    return q, k, qkv[:, :, 2]


def stick_breaking_weights(z, mask):
    z = z.astype(jnp.float32)
    log_1m = jnp.where(mask, jax.nn.log_sigmoid(-z), 0.0)
    suffix = lax.cumsum(log_1m, axis=z.ndim - 1, reverse=True) - log_1m
    return jnp.where(mask, jnp.exp(jax.nn.log_sigmoid(z) + suffix), 0.0)


def sb_attention_prompt(x, w_qkv, q_gain, k_gain, sb_bias, w_o):
    n, t, _ = x.shape
    q, k, v = qkv_heads(x, w_qkv, q_gain, k_gain)
    scale = HEAD_DIM ** -0.5
    bias = sb_bias.astype(jnp.float32)[None, :, None, None]
    n_blk = t // Q_BLOCK
    q_blocks = q.reshape(n, n_blk, Q_BLOCK, N_HEADS, HEAD_DIM).transpose(1, 0, 2, 3, 4)
    key_pos = jnp.arange(t)

    def one_block(args):
        qb, blk = args
        q_pos = blk * Q_BLOCK + jnp.arange(Q_BLOCK)
        z = jnp.einsum("nqhd,nkhd->nhqk", qb, k).astype(jnp.float32) * scale + bias
        mask = key_pos[None, :] < q_pos[:, None]
        a = stick_breaking_weights(z, mask).astype(v.dtype)
        return jnp.einsum("nhqk,nkhd->nqhd", a, v)

    o = lax.map(one_block, (q_blocks, jnp.arange(n_blk)))
    o = o.transpose(1, 0, 2, 3, 4).reshape(n, t, D_MODEL)
    return o @ w_o, k, v


def sb_attention_sample(x, k_past, v_past, w_qkv, q_gain, k_gain, sb_bias, w_o):
    n, t, _ = x.shape
    p = k_past.shape[1]
    q, k, v = qkv_heads(x, w_qkv, q_gain, k_gain)
    scale = HEAD_DIM ** -0.5
    bias = sb_bias.astype(jnp.float32)[None, :, None, None]
    z = jnp.concatenate([jnp.einsum("nqhd,nkhd->nhqk", q, k_past.astype(q.dtype)),
                         jnp.einsum("nqhd,nkhd->nhqk", q, k)], axis=-1)
    z = z.astype(jnp.float32) * scale + bias
    pos = jnp.arange(t)
    mask = jnp.concatenate([jnp.ones((t, p), dtype=bool), pos[None, :] < pos[:, None]], axis=1)
    a = stick_breaking_weights(z, mask).astype(v.dtype)
    o = (jnp.einsum("nhqk,nkhd->nqhd", a[..., :p], v_past.astype(v.dtype))
         + jnp.einsum("nhqk,nkhd->nqhd", a[..., p:], v))
    return o.reshape(n, t, D_MODEL) @ w_o, k, v


def sq_relu_mlp(x, w_up, w_down):
    return jnp.square(jax.nn.relu(x @ w_up)) @ w_down


def setup_inputs(seed: int = 0) -> dict:
    key = jax.random.key(seed)
    ks = jax.random.split(key, 32)
    f32 = jnp.float32
    n_pages = PAST_LEN // PAGE_SIZE
    n_pool = (DEC_BATCH * n_pages * 5) // 4
    nrm = lambda k, s, sc: jax.random.normal(k, s, f32) * sc

    x_prompt = nrm(ks[0], (BATCH, SEQ, D_MODEL), 1.0)
    x_sample = nrm(ks[1], (DEC_BATCH, DEC_SEQ, D_MODEL), 1.0)
    state_ssm_re = nrm(ks[2], (N_A_LAYERS, DEC_BATCH, SSM_GROUPS, SSM_STATE), 0.3)
    state_ssm_im = nrm(ks[3], (N_A_LAYERS, DEC_BATCH, SSM_GROUPS, SSM_STATE), 0.3)
    state_conv = nrm(ks[4], (N_B_LAYERS, DEC_BATCH, CONV_W - 1, D_MODEL), 1.0)
    cache_k = nrm(ks[5], (N_C_LAYERS, n_pool, PAGE_SIZE, N_HEADS, HEAD_DIM), 1.0)
    cache_v = nrm(ks[6], (N_C_LAYERS, n_pool, PAGE_SIZE, N_HEADS, HEAD_DIM), 1.0)
    page_table = jax.random.permutation(ks[7], n_pool)[: DEC_BATCH * n_pages]
    page_table = page_table.reshape(DEC_BATCH, n_pages).astype(jnp.int32)

    norm_mix = 1.0 + nrm(ks[8], (DEPTH, D_MODEL), 0.02)
    norm_mlp = 1.0 + nrm(ks[9], (DEPTH, D_MODEL), 0.02)

    ssm_lambda_re = -0.5 + nrm(ks[10], (N_A_LAYERS, SSM_GROUPS, SSM_STATE), 0.01)
    ssm_lambda_im = (jnp.pi * jnp.arange(SSM_STATE, dtype=f32)
                     + nrm(ks[11], (N_A_LAYERS, SSM_GROUPS, SSM_STATE), 0.01))
    ssm_log_dt = jax.random.uniform(ks[12], (N_A_LAYERS, SSM_GROUPS), f32,
                                    minval=math.log(DT_MIN), maxval=math.log(DT_MAX))
    ssm_b_re = nrm(ks[13], (N_A_LAYERS, SSM_GROUPS, SSM_STATE, SSM_GROUP), (2 * SSM_GROUP) ** -0.5)
    ssm_b_im = nrm(ks[14], (N_A_LAYERS, SSM_GROUPS, SSM_STATE, SSM_GROUP), (2 * SSM_GROUP) ** -0.5)
    ssm_c_re = nrm(ks[15], (N_A_LAYERS, SSM_GROUPS, SSM_GROUP, SSM_STATE), (2 * SSM_STATE) ** -0.5)
    ssm_c_im = nrm(ks[16], (N_A_LAYERS, SSM_GROUPS, SSM_GROUP, SSM_STATE), (2 * SSM_STATE) ** -0.5)
    ssm_d = nrm(ks[17], (N_A_LAYERS, D_MODEL), 1.0)
    ssm_w_glu = nrm(ks[18], (N_A_LAYERS, D_MODEL, 2 * D_MODEL), D_MODEL ** -0.5)

    conv_w_in = nrm(ks[19], (N_B_LAYERS, D_MODEL, 3 * D_MODEL), D_MODEL ** -0.5)
    conv_w_dw = nrm(ks[20], (N_B_LAYERS, CONV_W, D_MODEL), CONV_W ** -0.5)
    conv_w_out = nrm(ks[21], (N_B_LAYERS, D_MODEL, D_MODEL), D_MODEL ** -0.5)

    attn_w_qkv = nrm(ks[22], (N_C_LAYERS, D_MODEL, 3 * D_MODEL), D_MODEL ** -0.5)
    attn_q_norm = 1.0 + nrm(ks[23], (N_C_LAYERS, HEAD_DIM), 0.02)
    attn_k_norm = 1.0 + nrm(ks[24], (N_C_LAYERS, HEAD_DIM), 0.02)
    attn_sb_bias = SB_BIAS_INIT + nrm(ks[28], (N_C_LAYERS, N_HEADS), 0.1)
    attn_w_o = nrm(ks[25], (N_C_LAYERS, D_MODEL, D_MODEL), D_MODEL ** -0.5)

    mlp_w_up = nrm(ks[26], (DEPTH, D_MODEL, D_FF), D_MODEL ** -0.5)
    mlp_w_down = nrm(ks[27], (DEPTH, D_FF, D_MODEL), D_FF ** -0.5)

    return {
        "x_prompt": x_prompt, "x_sample": x_sample,
        "state_ssm_re": state_ssm_re, "state_ssm_im": state_ssm_im,
        "state_conv": state_conv, "cache_k": cache_k, "cache_v": cache_v,
        "page_table": page_table,
        "norm_mix": norm_mix, "norm_mlp": norm_mlp,
        "ssm_lambda_re": ssm_lambda_re, "ssm_lambda_im": ssm_lambda_im,
        "ssm_log_dt": ssm_log_dt, "ssm_b_re": ssm_b_re, "ssm_b_im": ssm_b_im,
        "ssm_c_re": ssm_c_re, "ssm_c_im": ssm_c_im, "ssm_d": ssm_d, "ssm_w_glu": ssm_w_glu,
        "conv_w_in": conv_w_in, "conv_w_dw": conv_w_dw, "conv_w_out": conv_w_out,
        "attn_w_qkv": attn_w_qkv, "attn_q_norm": attn_q_norm, "attn_k_norm": attn_k_norm,
        "attn_sb_bias": attn_sb_bias, "attn_w_o": attn_w_o,
        "mlp_w_up": mlp_w_up, "mlp_w_down": mlp_w_down,
    }


def reference(x_prompt, x_sample, state_ssm_re, state_ssm_im, state_conv, cache_k, cache_v,
              page_table, norm_mix, norm_mlp,
              ssm_lambda_re, ssm_lambda_im, ssm_log_dt, ssm_b_re, ssm_b_im, ssm_c_re, ssm_c_im,
              ssm_d, ssm_w_glu, conv_w_in, conv_w_dw, conv_w_out,
              attn_w_qkv, attn_q_norm, attn_k_norm, attn_sb_bias, attn_w_o,
              mlp_w_up, mlp_w_down):
    n_p = x_prompt.shape[0]
    n_s = x_sample.shape[0]
    y_p, y_s = x_prompt, x_sample
    ssm_re_p, ssm_im_p, ssm_re_s, ssm_im_s = [], [], [], []
    conv_p, conv_s = [], []
    k_p, v_p, k_s, v_s = [], [], [], []
    for i in range(DEPTH):
        kind = i % N_MIXERS
        j = i // N_MIXERS
        hp = rms_norm(y_p, norm_mix[i])
        hs = rms_norm(y_s, norm_mix[i])
        if kind == 0:
            ssm_w = (ssm_lambda_re[j], ssm_lambda_im[j], ssm_log_dt[j], ssm_b_re[j], ssm_b_im[j],
                     ssm_c_re[j], ssm_c_im[j], ssm_d[j], ssm_w_glu[j])
            zeros = jnp.zeros((n_p, SSM_GROUPS, SSM_STATE), jnp.float32)
            mp, re_p, im_p = s5_mixer(hp, zeros, zeros, *ssm_w)
            ms, re_s, im_s = s5_mixer(hs, state_ssm_re[j], state_ssm_im[j], *ssm_w)
            ssm_re_p.append(re_p); ssm_im_p.append(im_p)
            ssm_re_s.append(re_s); ssm_im_s.append(im_s)
        elif kind == 1:
            zbuf = jnp.zeros((n_p, CONV_W - 1, D_MODEL), hp.dtype)
            mp, bp = short_conv_mixer(hp, zbuf, conv_w_in[j], conv_w_dw[j], conv_w_out[j])
            ms, bs = short_conv_mixer(hs, state_conv[j], conv_w_in[j], conv_w_dw[j], conv_w_out[j])
            conv_p.append(bp); conv_s.append(bs)
        else:
            mp, kp, vp = sb_attention_prompt(hp, attn_w_qkv[j], attn_q_norm[j], attn_k_norm[j],
                                             attn_sb_bias[j], attn_w_o[j])
            k_past = cache_k[j][page_table].reshape(n_s, -1, N_HEADS, HEAD_DIM)
            v_past = cache_v[j][page_table].reshape(n_s, -1, N_HEADS, HEAD_DIM)
            ms, ks_new, vs_new = sb_attention_sample(hs, k_past, v_past, attn_w_qkv[j],
                                                     attn_q_norm[j], attn_k_norm[j],
                                                     attn_sb_bias[j], attn_w_o[j])
            k_p.append(kp); v_p.append(vp); k_s.append(ks_new); v_s.append(vs_new)
        y_p = y_p + mp
        y_s = y_s + ms
        y_p = y_p + sq_relu_mlp(rms_norm(y_p, norm_mlp[i]), mlp_w_up[i], mlp_w_down[i])
        y_s = y_s + sq_relu_mlp(rms_norm(y_s, norm_mlp[i]), mlp_w_up[i], mlp_w_down[i])
    return (y_p, y_s,
            jnp.stack(ssm_re_p), jnp.stack(ssm_im_p), jnp.stack(ssm_re_s), jnp.stack(ssm_im_s),
            jnp.stack(conv_p), jnp.stack(conv_s),
            jnp.stack(k_p), jnp.stack(v_p), jnp.stack(k_s), jnp.stack(v_s))
```

```python
import functools
import math

import jax
import jax.numpy as jnp
from jax import lax
from jax.experimental import pallas as pl
from jax.experimental.pallas import tpu as pltpu

F32 = jnp.float32
BF16 = jnp.bfloat16
EPS = 1e-6
LANES = 128
SUBLANES = 8
VMEM_LIMIT_BYTES = 52 * 1024 * 1024
S5_CHUNK = 16
HIGHEST = lax.Precision.HIGHEST


def _cparams(n_axes):
    return pltpu.CompilerParams(dimension_semantics=("arbitrary",) * n_axes,
                                vmem_limit_bytes=VMEM_LIMIT_BYTES)


def _rms(x, g):
    ms = jnp.mean(x * x, axis=-1, keepdims=True)
    return x * lax.rsqrt(ms + EPS) * g


def _iota(shape, dim):
    return lax.broadcasted_iota(jnp.int32, shape, dim)


def _mod(x, n):
    return x & (n - 1) if n & (n - 1) == 0 else lax.rem(x, n)


def _div(x, n):
    return x >> (n.bit_length() - 1) if n & (n - 1) == 0 else lax.div(x, n)


def _fused_matmul(name, m, kdim, n_out, tm, tn, row_in, const_in, weights, tile_in,
                  out_dtypes, prologue, epilogue):
    n_row, n_const, n_w, n_tile, n_o = (len(row_in), len(const_in), len(weights),
                                        len(tile_in), len(out_dtypes))
    in_specs, args = [], []
    for arr, blk, imap in row_in:
        in_specs.append(pl.BlockSpec(blk, imap))
        args.append(arr)
    for arr in const_in:
        in_specs.append(pl.BlockSpec(arr.shape, lambda i, j, nd=arr.ndim: (0,) * nd))
        args.append(arr)
    for arr, layer, off in weights:
        in_specs.append(pl.BlockSpec((None, kdim, tn),
                                     lambda i, j, layer=layer, off=off: (layer, 0, j + off)))
        args.append(arr)
    for arr in tile_in:
        in_specs.append(pl.BlockSpec((tm, tn), lambda i, j: (i, j)))
        args.append(arr)
    out_shape = [jax.ShapeDtypeStruct((m, n_out), dt) for dt in out_dtypes]
    out_specs = [pl.BlockSpec((tm, tn), lambda i, j: (i, j)) for _ in out_dtypes]

    def body(*refs):
        p = 0
        row_refs = refs[p:p + n_row]; p += n_row
        const_refs = refs[p:p + n_const]; p += n_const
        w_refs = refs[p:p + n_w]; p += n_w
        tile_refs = refs[p:p + n_tile]; p += n_tile
        out_refs = refs[p:p + n_o]; p += n_o
        lhs_ref = refs[p]

        @pl.when(pl.program_id(1) == 0)
        def _():
            lhs_ref[...] = prologue(row_refs, const_refs).astype(BF16)

        lhs = lhs_ref[...]
        accs = [jnp.dot(lhs, w[...], preferred_element_type=F32) for w in w_refs]
        for o_ref, val in zip(out_refs, epilogue(accs, tile_refs, const_refs)):
            o_ref[...] = val.astype(o_ref.dtype)

    return pl.pallas_call(
        body, grid=(m // tm, n_out // tn), in_specs=in_specs, out_specs=out_specs,
        out_shape=out_shape, scratch_shapes=[pltpu.VMEM((tm, kdim), BF16)],
        compiler_params=_cparams(2), name=name)(*args)


def _row_block(tm, kdim):
    return (tm, kdim), (lambda i, j: (i, 0))


def _norm_prologue(row_refs, const_refs):
    return _rms(row_refs[0][...], const_refs[0][...])


def _ident_prologue(row_refs, const_refs):
    return row_refs[0][...]


def _head_norm(acc, gain, scale):
    segs = []
    for h in range(acc.shape[1] // LANES):
        seg = acc[:, h * LANES:(h + 1) * LANES]
        ms = jnp.mean(seg * seg, axis=-1, keepdims=True)
        y = seg * lax.rsqrt(ms + EPS) * gain
        segs.append(y * scale if scale != 1.0 else y)
    return segs[0] if len(segs) == 1 else jnp.concatenate(segs, axis=-1)


def _tiles(m, n):
    return min(512, m), min(1024, n)


def _norm_proj(name, x, gain, w, layer, col_off, n_out, out_dtypes, epilogue, extra_const=()):
    m, kdim = x.shape
    tm, tn = _tiles(m, n_out)
    blk, imap = _row_block(tm, kdim)
    return _fused_matmul(name, m, kdim, n_out, tm, tn, [(x, blk, imap)],
                         [gain.reshape(1, kdim)] + list(extra_const),
                         [(w, layer, col_off // tn)], [], out_dtypes, _norm_prologue, epilogue)


def _mlp_body(x_ref, g_ref, wu_ref, wd_ref, o_ref, xn_ref, acc_ref):
    f = pl.program_id(1)

    @pl.when(f == 0)
    def _():
        xn_ref[...] = _rms(x_ref[...], g_ref[...]).astype(BF16)
        acc_ref[...] = jnp.zeros_like(acc_ref)

    h = jnp.dot(xn_ref[...], wu_ref[...], preferred_element_type=F32)
    h = jnp.maximum(h, 0.0)
    acc_ref[...] += jnp.dot((h * h).astype(BF16), wd_ref[...], preferred_element_type=F32)

    @pl.when(f == pl.num_programs(1) - 1)
    def _():
        o_ref[...] = x_ref[...] + acc_ref[...]


def _mlp(x, gain, w_up, w_down, layer):
    m, d = x.shape
    ff = w_up.shape[2]
    tm, tf = min(512, m), min(512, ff)
    return pl.pallas_call(
        _mlp_body, grid=(m // tm, ff // tf),
        in_specs=[pl.BlockSpec((tm, d), lambda i, f: (i, 0)),
                  pl.BlockSpec((1, d), lambda i, f: (0, 0)),
                  pl.BlockSpec((None, d, tf), lambda i, f: (layer, 0, f)),
                  pl.BlockSpec((None, tf, d), lambda i, f: (layer, f, 0))],
        out_specs=pl.BlockSpec((tm, d), lambda i, f: (i, 0)),
        out_shape=jax.ShapeDtypeStruct((m, d), F32),
        scratch_shapes=[pltpu.VMEM((tm, d), BF16), pltpu.VMEM((tm, d), F32)],
        compiler_params=_cparams(2), name="mlp")(x, gain.reshape(1, d), w_up, w_down)


def _norm_body(x_ref, g_ref, o_ref):
    o_ref[...] = _rms(x_ref[...], g_ref[...])


def _norm(x, gain):
    m, d = x.shape
    tm = min(512, m)
    return pl.pallas_call(
        _norm_body, grid=(m // tm,),
        in_specs=[pl.BlockSpec((tm, d), lambda i: (i, 0)), pl.BlockSpec((1, d), lambda i: (0, 0))],
        out_specs=pl.BlockSpec((tm, d), lambda i: (i, 0)),
        out_shape=jax.ShapeDtypeStruct((m, d), F32),
        compiler_params=_cparams(1), name="rmsnorm")(x, gain.reshape(1, d))


def _complex_scale(acos, asin, h):
    return acos * h + asin * pltpu.roll(h, h.shape[1] // 2, axis=1)


def _s5_body(*refs, gt, n_chunk, has_h0):
    if has_h0:
        u_ref, m_ref, bc_ref, cp_ref, d_ref, acos_ref, asin_ref, h0_ref, z_ref, hl_ref, h_scr = refs
    else:
        u_ref, m_ref, bc_ref, cp_ref, d_ref, acos_ref, asin_ref, z_ref, hl_ref, h_scr = refs
    rows = u_ref.shape[1]
    n_seq = rows // n_chunk

    def one_group(g, carry):
        u = u_ref[g]
        x = jnp.dot(u, bc_ref[g], preferred_element_type=F32, precision=HIGHEST)
        acos, asin = acos_ref[g], asin_ref[g]
        if has_h0:
            h_prev = h0_ref[g]
            h = x + _complex_scale(acos[0:1], asin[0:1], h_prev)
            hl_ref[g] = h
        else:
            kidx = _iota(x.shape, 0) & (n_chunk - 1)
            h = x
            shift, si = 1, 0
            while shift < n_chunk:
                sh = jnp.where(kidx >= shift, pltpu.roll(h, shift, axis=0), 0.0)
                h = h + _complex_scale(acos[si:si + 1], asin[si:si + 1], sh)
                shift, si = shift * 2, si + 1
            h_prev = jnp.where(kidx >= 1, pltpu.roll(h, 1, axis=0), 0.0)
            h_scr[...] = h
            for n in range(n_seq):
                hl_ref[g, pl.ds(n, 1), :] = h_scr[pl.ds((n + 1) * n_chunk - 1, 1), :]
        y = (jnp.dot(u, m_ref[g], preferred_element_type=F32, precision=HIGHEST)
             + jnp.dot(h_prev, cp_ref[g], preferred_element_type=F32, precision=HIGHEST)
             + u * d_ref[g])
        cdf = 0.5 * (1.0 + jnp.tanh(math.sqrt(2.0 / math.pi) * (y + 0.044715 * (y * y * y))))
        z_ref[g] = (y * cdf).astype(z_ref.dtype)
        return carry

    lax.fori_loop(0, gt, one_group, 0)


def _s5_tables(lam_re, lam_im, log_dt, b_re, b_im, c_re, c_im, d, chunk, n_chunk):
    g, p = lam_re.shape
    c = b_re.shape[2]
    ldt = lax.complex(lam_re, lam_im) * jnp.exp(log_dt)[:, None]
    a_bar = jnp.exp(ldt)
    bbar = ((a_bar - 1.0) / lax.complex(lam_re, lam_im))[..., None] * lax.complex(b_re, b_im)
    taus = jnp.arange(chunk + 1, dtype=F32)
    apow = jnp.exp(ldt[:, None, :] * taus[None, :, None])
    cc = lax.complex(c_re, c_im)
    w1 = cc[:, None, :, :] * apow[:, :chunk, None, :]
    lhs = jnp.concatenate([jnp.real(w1), -jnp.imag(w1)], axis=-1)
    rhs = jnp.concatenate([jnp.real(bbar), jnp.imag(bbar)], axis=1)
    kt = jnp.einsum("gtoq,gqi->gtoi", lhs, rhs, precision=HIGHEST)
    kpad = jnp.concatenate([kt, jnp.zeros((g, 1, c, c), F32)], axis=1)
    s_idx = jnp.arange(chunk)[:, None]
    t_idx = jnp.arange(chunk)[None, :]
    tau = jnp.where(t_idx >= s_idx, t_idx - s_idx, chunk)
    mmat = kpad[:, tau].transpose(0, 1, 4, 2, 3).reshape(g, chunk * c, chunk * c)
    rev = apow[:, :chunk][:, ::-1]
    bcx = (rev[:, :, None, :] * bbar.transpose(0, 2, 1)[:, None, :, :]).reshape(g, chunk * c, p)
    bc = jnp.concatenate([jnp.real(bcx), jnp.imag(bcx)], axis=-1)
    ct = (cc[:, None, :, :] * apow[:, 1:, None, :]).transpose(0, 3, 1, 2).reshape(g, p, chunk * c)
    cp = jnp.concatenate([jnp.real(ct), -jnp.imag(ct)], axis=1)
    shifts = [1]
    while shifts[-1] * 2 < n_chunk:
        shifts.append(shifts[-1] * 2)
    while len(shifts) < SUBLANES:
        shifts.append(shifts[-1])
    apl = jnp.exp(ldt[:, None, :] * (chunk * jnp.asarray(shifts, F32))[None, :, None])
    acos = jnp.concatenate([jnp.real(apl), jnp.real(apl)], axis=-1)
    asin = jnp.concatenate([-jnp.imag(apl), jnp.imag(apl)], axis=-1)
    dt = jnp.tile(d.reshape(g, 1, c), (1, chunk, 1)).reshape(g, 1, chunk * c)
    return mmat, bc, cp, dt, acos, asin


def _s5_core(hn, n_seq, t_len, tabs, h0):
    mmat, bc, cp, dt, acos, asin = tabs
    g = mmat.shape[0]
    lc = mmat.shape[1]
    p2 = bc.shape[2]
    c = hn.shape[1] // g
    chunk = lc // c
    n_chunk = t_len // chunk
    rows = n_seq * n_chunk
    u_r = hn.reshape(n_seq, n_chunk, chunk, g, c).transpose(3, 0, 1, 2, 4).reshape(g, rows, lc)
    gt = min(8, g)
    has_h0 = h0 is not None
    grp = lambda shape: pl.BlockSpec((gt,) + shape, lambda i: (i, 0, 0))
    in_specs = [grp((rows, lc)), grp((lc, lc)), grp((lc, p2)), grp((p2, lc)), grp((1, lc)),
                grp((SUBLANES, p2)), grp((SUBLANES, p2))]
    args = [u_r, mmat, bc, cp, dt, acos, asin]
    if has_h0:
        in_specs.append(grp((n_seq, p2)))
        args.append(h0)
    z_r, h_last = pl.pallas_call(
        functools.partial(_s5_body, gt=gt, n_chunk=n_chunk, has_h0=has_h0),
        grid=(g // gt,), in_specs=in_specs,
        out_specs=[grp((rows, lc)), grp((n_seq, p2))],
        out_shape=[jax.ShapeDtypeStruct((g, rows, lc), BF16),
                   jax.ShapeDtypeStruct((g, n_seq, p2), F32)],
        scratch_shapes=[pltpu.VMEM((rows, p2), F32)],
        compiler_params=_cparams(1), name="s5_core")(*args)
    z = z_r.reshape(g, n_seq, n_chunk, chunk, c).transpose(1, 2, 3, 0, 4).reshape(n_seq * t_len, g * c)
    return z, h_last


def _glu_epilogue(accs, tile_refs, const_refs):
    a, gate = accs
    return [tile_refs[0][...] + a * jax.nn.sigmoid(gate)]


def _s5_layer(y, gain, n_seq, t_len, tabs, h0, w_glu, layer):
    m, d = y.shape
    z, h_last = _s5_core(_norm(y, gain), n_seq, t_len, tabs, h0)
    tm, tn = _tiles(m, d)
    blk, imap = _row_block(tm, d)
    (out,) = _fused_matmul("s5_glu", m, d, d, tm, tn, [(z, blk, imap)], [],
                           [(w_glu, layer, 0), (w_glu, layer, d // tn)], [y], [F32],
                           _ident_prologue, _glu_epilogue)
    p = h_last.shape[2] // 2
    h_last = h_last.transpose(1, 0, 2)
    return out, h_last[..., :p], h_last[..., p:]


def _conv_in_epilogue(accs, tile_refs, const_refs):
    gate_b, gate_c, v = accs
    return [gate_b, gate_c * v]


def _conv_prologue(row_refs, const_refs, *, t_len, tm, has_buf):
    w = const_refs[0][...]
    bg, cv = row_refs[0][...], row_refs[1][...]
    t = _mod(pl.program_id(0) * tm + _iota((tm, 1), 0), t_len)
    if has_buf:
        b0, b1 = row_refs[2][...], row_refs[3][...]
        r1 = jnp.where(t >= 1, pltpu.roll(cv, 1, axis=0), b1)
        r2 = jnp.where(t >= 2, pltpu.roll(cv, 2, axis=0), jnp.where(t == 1, b1, b0))
    else:
        full = jnp.concatenate([row_refs[2][...], cv], axis=0)
        r1 = jnp.where(t >= 1, pltpu.roll(full, 1, axis=0)[SUBLANES:], 0.0)
        r2 = jnp.where(t >= 2, pltpu.roll(full, 2, axis=0)[SUBLANES:], 0.0)
    return bg * (w[2:3] * cv + w[1:2] * r1 + w[0:1] * r2)


def _resid_epilogue(accs, tile_refs, const_refs):
    return [tile_refs[0][...] + accs[0]]


def _conv_layer(y, gain, n_seq, t_len, buf, w_in, w_dw, w_out, layer):
    m, d = y.shape
    tm, tn = min(512, m), min(512, d)
    blk, imap = _row_block(tm, d)
    bg, cv = _fused_matmul("conv_in", m, d, d, tm, tn, [(y, blk, imap)], [gain.reshape(1, d)],
                           [(w_in, layer, 0), (w_in, layer, d // tn), (w_in, layer, 2 * d // tn)],
                           [], [F32, F32], _norm_prologue, _conv_in_epilogue)
    tm, tn = _tiles(m, d)
    blk, imap = _row_block(tm, d)
    rows = [(bg, blk, imap), (cv, blk, imap)]
    if buf is None:
        per = tm // SUBLANES
        rows.append((cv, (SUBLANES, d), lambda i, j: (jnp.maximum(i * per - 1, 0), 0)))
    else:
        assert tm == m
        rows.append((jnp.repeat(buf[:, 0], t_len, axis=0), blk, imap))
        rows.append((jnp.repeat(buf[:, 1], t_len, axis=0), blk, imap))
    (out,) = _fused_matmul(
        "conv_out", m, d, d, tm, tn, rows, [w_dw[layer]], [(w_out, layer, 0)], [y], [F32],
        functools.partial(_conv_prologue, t_len=t_len, tm=tm, has_buf=buf is not None),
        _resid_epilogue)
    assert t_len >= 2
    return out, cv.reshape(n_seq, t_len, d)[:, t_len - 2:]


def _sb_tile(z, mask, upper, v_blk, carry):
    run, acc = carry
    sp = jnp.log1p(jnp.exp(-jnp.abs(z)))
    log_b = jnp.minimum(z, 0.0) - sp
    log_1m = -jnp.maximum(z, 0.0) - sp
    if mask is not None:
        log_1m = jnp.where(mask, log_1m, 0.0)
    hi = log_1m.astype(BF16)
    lo = (log_1m - hi.astype(F32)).astype(BF16)
    suffix = (jnp.dot(hi, upper, preferred_element_type=F32)
              + jnp.dot(lo, upper, preferred_element_type=F32))
    a = jnp.exp(log_b + suffix + run)
    if mask is not None:
        a = jnp.where(mask, a, 0.0)
    acc = acc + jnp.dot(a.astype(BF16), v_blk, preferred_element_type=F32)
    run = run + jnp.sum(log_1m, axis=-1, keepdims=True)
    return run, acc


def _later_key_matrix(tk):
    return (_iota((tk, tk), 0) > _iota((tk, tk), 1)).astype(BF16)


def _qk(q, k_blk):
    return lax.dot_general(q, k_blk, (((1,), (1,)), ((), ())), preferred_element_type=F32)


def _attn_body(bias_ref, q_ref, k_ref, v_ref, o_ref, *, tq, tk):
    bias = bias_ref[pl.program_id(1)]
    qi = pl.program_id(2)
    q = q_ref[...]
    upper = _later_key_matrix(tk)
    n_diag = tq // tk

    def tile(kb, carry, masked):
        start = pl.multiple_of(kb * tk, tk)
        z = _qk(q, k_ref[pl.ds(start, tk), :]) + bias
        mask = None
        if masked:
            mask = (kb * tk + _iota((tq, tk), 1)) < (qi * tq + _iota((tq, tk), 0))
        return _sb_tile(z, mask, upper, v_ref[pl.ds(start, tk), :], carry)

    carry = (jnp.zeros((tq, 1), F32), jnp.zeros((tq, q.shape[1]), F32))
    for dblk in range(n_diag - 1, -1, -1):
        carry = tile(qi * n_diag + dblk, carry, True)
    n_below = qi * n_diag
    carry = lax.fori_loop(0, n_below, lambda it, c: tile(n_below - 1 - it, c, False), carry)
    o_ref[...] = carry[1].astype(o_ref.dtype)


def _attn_prompt(q, k, v, bias, n_seq, t_len, n_heads):
    m, d = q.shape
    dh = d // n_heads
    tk = LANES
    tq = min(256, t_len)
    qb = t_len // tq
    grid_spec = pltpu.PrefetchScalarGridSpec(
        num_scalar_prefetch=1, grid=(n_seq, n_heads, qb),
        in_specs=[pl.BlockSpec((tq, dh), lambda n, h, i, b: (n * qb + i, h)),
                  pl.BlockSpec((t_len, dh), lambda n, h, i, b: (n, h)),
                  pl.BlockSpec((t_len, dh), lambda n, h, i, b: (n, h))],
        out_specs=pl.BlockSpec((tq, dh), lambda n, h, i, b: (n * qb + i, h)))
    return pl.pallas_call(
        functools.partial(_attn_body, tq=tq, tk=tk), grid_spec=grid_spec,
        out_shape=jax.ShapeDtypeStruct((m, d), BF16),
        compiler_params=_cparams(3), name="sb_attn_prompt")(bias, q, k, v)


def _decode_body(pt_ref, q_ref, bias_ref, kn_ref, vn_ref, kc_ref, vc_ref, o_ref,
                 run_ref, acc_ref, *, n_heads):
    step = pl.program_id(1)
    q = q_ref[...]
    bias = bias_ref[...]
    tk = kn_ref.shape[0]
    upper = _later_key_matrix(tk)

    @pl.when(step == 0)
    def _():
        rows = q.shape[0]
        z = _qk(q, kn_ref[...]) + bias
        mask = _iota((rows, tk), 1) < _div(_iota((rows, tk), 0), n_heads)
        carry = (jnp.zeros((rows, 1), F32), jnp.zeros(acc_ref.shape, F32))
        run, acc = _sb_tile(z, mask, upper, vn_ref[...], carry)
        run_ref[...] = jnp.broadcast_to(run, run_ref.shape)
        acc_ref[...] = acc

    @pl.when(step > 0)
    def _():
        z = _qk(q, kc_ref[...].astype(BF16)) + bias
        carry = (run_ref[:, 0:1], acc_ref[...])
        run, acc = _sb_tile(z, None, upper, vc_ref[...].astype(BF16), carry)
        run_ref[...] = jnp.broadcast_to(run, run_ref.shape)
        acc_ref[...] = acc

    @pl.when(step == pl.num_programs(1) - 1)
    def _():
        o_ref[...] = acc_ref[...]


def _attn_decode(q, k_new, v_new, bias, cache_k, cache_v, layer, page_table, n_seq, t_len,
                 n_heads):
    d = q.shape[1]
    dh = d // n_heads
    n_pages = page_table.shape[1]
    page = cache_k.shape[2]
    rows = t_len * n_heads
    head_mask = (jnp.arange(d)[None, :] // dh) == jnp.arange(n_heads)[:, None]
    q_bd = jnp.where(head_mask[None, None], q.reshape(n_seq, t_len, 1, d), 0).astype(BF16)
    q_bd = q_bd.reshape(n_seq, rows, d)
    bias_rows = jnp.tile(bias, t_len).reshape(rows, 1)
    pad = lambda x: jnp.pad(x.reshape(n_seq, t_len, d), ((0, 0), (0, page - t_len), (0, 0)))
    kc = cache_k.reshape(cache_k.shape[0], cache_k.shape[1], page, d)
    vc = cache_v.reshape(cache_v.shape[0], cache_v.shape[1], page, d)
    past = lambda n, s, pt: (layer, pt[n, n_pages - jnp.maximum(s, 1)], 0, 0)
    per_seq = lambda n, s, pt: (n, 0, 0)
    grid_spec = pltpu.PrefetchScalarGridSpec(
        num_scalar_prefetch=1, grid=(n_seq, n_pages + 1),
        in_specs=[pl.BlockSpec((None, rows, d), per_seq),
                  pl.BlockSpec((rows, 1), lambda n, s, pt: (0, 0)),
                  pl.BlockSpec((None, page, d), per_seq),
                  pl.BlockSpec((None, page, d), per_seq),
                  pl.BlockSpec((None, None, page, d), past),
                  pl.BlockSpec((None, None, page, d), past)],
        out_specs=pl.BlockSpec((None, rows, d), per_seq),
        scratch_shapes=[pltpu.VMEM((rows, LANES), F32), pltpu.VMEM((rows, d), F32)])
    o_full = pl.pallas_call(
        functools.partial(_decode_body, n_heads=n_heads), grid_spec=grid_spec,
        out_shape=jax.ShapeDtypeStruct((n_seq, rows, d), F32),
        compiler_params=_cparams(2), name="sb_attn_decode")(
            page_table, q_bd, bias_rows, pad(k_new), pad(v_new), kc, vc)
    o = o_full.reshape(n_seq, t_len, n_heads, n_heads, dh)
    o = jnp.diagonal(o, axis1=2, axis2=3)
    return jnp.moveaxis(o, -1, 2).reshape(n_seq * t_len, d).astype(BF16)


def _attn_layer(y, gain, n_seq, t_len, n_heads, w_qkv, q_gain, k_gain, sb_bias, w_o, layer,
                cache=None):
    m, d = y.shape
    dh = d // n_heads
    scale = dh ** -0.5
    qn = lambda accs, t, c: [_head_norm(accs[0], c[1][...], scale)]
    kn = lambda accs, t, c: [_head_norm(accs[0], c[1][...], 1.0)] * 2
    vn = lambda accs, t, c: [accs[0]] * 2
    (q,) = _norm_proj("attn_q", y, gain, w_qkv, layer, 0, d, [BF16], qn, [q_gain.reshape(1, dh)])
    k32, kb = _norm_proj("attn_k", y, gain, w_qkv, layer, d, d, [F32, BF16], kn,
                         [k_gain.reshape(1, dh)])
    v32, vb = _norm_proj("attn_v", y, gain, w_qkv, layer, 2 * d, d, [F32, BF16], vn)
    if cache is None:
        o = _attn_prompt(q, kb, vb, sb_bias, n_seq, t_len, n_heads)
    else:
        cache_k, cache_v, page_table = cache
        o = _attn_decode(q, kb, vb, sb_bias, cache_k, cache_v, layer, page_table, n_seq, t_len,
                         n_heads)
    tm, tn = _tiles(m, d)
    blk, imap = _row_block(tm, d)
    (out,) = _fused_matmul("attn_out", m, d, d, tm, tn, [(o, blk, imap)], [],
                           [(w_o, layer, 0)], [y], [F32], _ident_prologue, _resid_epilogue)
    shape = (n_seq, t_len, n_heads, dh)
    return out, k32.reshape(shape), v32.reshape(shape)


def kernel(x_prompt, x_sample, state_ssm_re, state_ssm_im, state_conv, cache_k, cache_v, page_table, norm_mix, norm_mlp, ssm_lambda_re, ssm_lambda_im, ssm_log_dt, ssm_b_re, ssm_b_im, ssm_c_re, ssm_c_im, ssm_d, ssm_w_glu, conv_w_in, conv_w_dw, conv_w_out, attn_w_qkv, attn_q_norm, attn_k_norm, attn_sb_bias, attn_w_o, mlp_w_up, mlp_w_down):
    n_p, t_p, d = x_prompt.shape
    n_s, t_s, _ = x_sample.shape
    depth = norm_mix.shape[0]
    n_heads = attn_sb_bias.shape[1]
    assert conv_w_dw.shape[1] == 3 and d // n_heads == LANES
    chunk_p = min(S5_CHUNK, t_p)
    assert t_p % chunk_p == 0 and (t_p // chunk_p) & (t_p // chunk_p - 1) == 0

    w_glu, w_in, w_out = (w.astype(BF16) for w in (ssm_w_glu, conv_w_in, conv_w_out))
    w_qkv, w_o = attn_w_qkv.astype(BF16), attn_w_o.astype(BF16)
    w_up, w_down = mlp_w_up.astype(BF16), mlp_w_down.astype(BF16)

    y_p = x_prompt.reshape(n_p * t_p, d)
    y_s = x_sample.reshape(n_s * t_s, d)
    outs = {k: [] for k in ("re_p", "im_p", "re_s", "im_s", "cv_p", "cv_s", "k_p", "v_p", "k_s", "v_s")}
    for i in range(depth):
        kind, j = i % 3, i // 3
        if kind == 0:
            ssm = (ssm_lambda_re[j], ssm_lambda_im[j], ssm_log_dt[j], ssm_b_re[j], ssm_b_im[j],
                   ssm_c_re[j], ssm_c_im[j], ssm_d[j])
            tabs_p = _s5_tables(*ssm, chunk_p, t_p // chunk_p)
            tabs_s = _s5_tables(*ssm, t_s, 1)
            h0 = jnp.concatenate([state_ssm_re[j], state_ssm_im[j]], axis=-1).transpose(1, 0, 2)
            y_p, re_p, im_p = _s5_layer(y_p, norm_mix[i], n_p, t_p, tabs_p, None, w_glu, j)
            y_s, re_s, im_s = _s5_layer(y_s, norm_mix[i], n_s, t_s, tabs_s, h0, w_glu, j)
            outs["re_p"].append(re_p); outs["im_p"].append(im_p)
            outs["re_s"].append(re_s); outs["im_s"].append(im_s)
        elif kind == 1:
            y_p, cv_p = _conv_layer(y_p, norm_mix[i], n_p, t_p, None, w_in, conv_w_dw, w_out, j)
            y_s, cv_s = _conv_layer(y_s, norm_mix[i], n_s, t_s, state_conv[j], w_in, conv_w_dw,
                                    w_out, j)
            outs["cv_p"].append(cv_p); outs["cv_s"].append(cv_s)
        else:
            attn = (w_qkv, attn_q_norm[j], attn_k_norm[j], attn_sb_bias[j], w_o, j)
            y_p, k_p, v_p = _attn_layer(y_p, norm_mix[i], n_p, t_p, n_heads, *attn)
            y_s, k_s, v_s = _attn_layer(y_s, norm_mix[i], n_s, t_s, n_heads, *attn,
                                        cache=(cache_k, cache_v, page_table))
            outs["k_p"].append(k_p); outs["v_p"].append(v_p)
            outs["k_s"].append(k_s); outs["v_s"].append(v_s)
        y_p = _mlp(y_p, norm_mlp[i], w_up, w_down, i)
        y_s = _mlp(y_s, norm_mlp[i], w_up, w_down, i)
    st = lambda key: jnp.stack(outs[key])
    return (y_p.reshape(n_p, t_p, d), y_s.reshape(n_s, t_s, d),
            st("re_p"), st("im_p"), st("re_s"), st("im_s"), st("cv_p"), st("cv_s"),
            st("k_p"), st("v_p"), st("k_s"), st("v_s"))
```

```python
import functools
import math

import jax
import jax.numpy as jnp
from jax import lax
from jax.experimental import pallas as pl
from jax.experimental.pallas import tpu as pltpu

F32 = jnp.float32
BF16 = jnp.bfloat16
EPS = 1e-6
LANES = 128
SUBLANES = 8
VMEM_LIMIT_BYTES = 52 * 1024 * 1024
S5_CHUNK = 16
HIGHEST = lax.Precision.HIGHEST


def _cparams(n_axes):
    return pltpu.CompilerParams(dimension_semantics=("arbitrary",) * n_axes,
                                vmem_limit_bytes=VMEM_LIMIT_BYTES)


def _rms(x, g):
    ms = jnp.mean(x * x, axis=-1, keepdims=True)
    return x * lax.rsqrt(ms + EPS) * g


def _iota(shape, dim):
    return lax.broadcasted_iota(jnp.int32, shape, dim)


def _mod(x, n):
    return x & (n - 1) if n & (n - 1) == 0 else lax.rem(x, n)


def _div(x, n):
    return x >> (n.bit_length() - 1) if n & (n - 1) == 0 else lax.div(x, n)


def _fused_matmul(name, m, kdim, n_out, tm, tn, row_in, const_in, weights, tile_in,
                  out_dtypes, prologue, epilogue):
    n_row, n_const, n_w, n_tile, n_o = (len(row_in), len(const_in), len(weights),
                                        len(tile_in), len(out_dtypes))
    in_specs, args = [], []
    for arr, blk, imap in row_in:
        in_specs.append(pl.BlockSpec(blk, imap))
        args.append(arr)
    for arr in const_in:
        in_specs.append(pl.BlockSpec(arr.shape, lambda i, j, nd=arr.ndim: (0,) * nd))
        args.append(arr)
    for arr, layer, off in weights:
        in_specs.append(pl.BlockSpec((None, kdim, tn),
                                     lambda i, j, layer=layer, off=off: (layer, 0, j + off)))
        args.append(arr)
    for arr in tile_in:
        in_specs.append(pl.BlockSpec((tm, tn), lambda i, j: (i, j)))
        args.append(arr)
    out_shape = [jax.ShapeDtypeStruct((m, n_out), dt) for dt in out_dtypes]
    out_specs = [pl.BlockSpec((tm, tn), lambda i, j: (i, j)) for _ in out_dtypes]

    def body(*refs):
        p = 0
        row_refs = refs[p:p + n_row]; p += n_row
        const_refs = refs[p:p + n_const]; p += n_const
        w_refs = refs[p:p + n_w]; p += n_w
        tile_refs = refs[p:p + n_tile]; p += n_tile
        out_refs = refs[p:p + n_o]; p += n_o
        lhs_ref = refs[p]

        @pl.when(pl.program_id(1) == 0)
        def _():
            lhs_ref[...] = prologue(row_refs, const_refs).astype(BF16)

        lhs = lhs_ref[...]
        accs = [jnp.dot(lhs, w[...], preferred_element_type=F32) for w in w_refs]
        for o_ref, val in zip(out_refs, epilogue(accs, tile_refs, const_refs)):
            o_ref[...] = val.astype(o_ref.dtype)

    return pl.pallas_call(
        body, grid=(m // tm, n_out // tn), in_specs=in_specs, out_specs=out_specs,
        out_shape=out_shape, scratch_shapes=[pltpu.VMEM((tm, kdim), BF16)],
        compiler_params=_cparams(2), name=name)(*args)


def _row_block(tm, kdim):
    return (tm, kdim), (lambda i, j: (i, 0))


def _norm_prologue(row_refs, const_refs):
    return _rms(row_refs[0][...], const_refs[0][...])


def _ident_prologue(row_refs, const_refs):
    return row_refs[0][...]


def _head_norm(acc, gain, scale):
    segs = []
    for h in range(acc.shape[1] // LANES):
        seg = acc[:, h * LANES:(h + 1) * LANES]
        ms = jnp.mean(seg * seg, axis=-1, keepdims=True)
        y = seg * lax.rsqrt(ms + EPS) * gain
        segs.append(y * scale if scale != 1.0 else y)
    return segs[0] if len(segs) == 1 else jnp.concatenate(segs, axis=-1)


def _tiles(m, n):
    return min(512, m), min(1024, n)


def _norm_proj(name, x, gain, w, layer, col_off, n_out, out_dtypes, epilogue, extra_const=()):
    m, kdim = x.shape
    tm, tn = _tiles(m, n_out)
    blk, imap = _row_block(tm, kdim)
    return _fused_matmul(name, m, kdim, n_out, tm, tn, [(x, blk, imap)],
                         [gain.reshape(1, kdim)] + list(extra_const),
                         [(w, layer, col_off // tn)], [], out_dtypes, _norm_prologue, epilogue)


def _mlp_body(x_ref, g_ref, wu_ref, wd_ref, o_ref, xn_ref, acc_ref):
    f = pl.program_id(1)

    @pl.when(f == 0)
    def _():
        xn_ref[...] = _rms(x_ref[...], g_ref[...]).astype(BF16)
        acc_ref[...] = jnp.zeros_like(acc_ref)

    h = jnp.dot(xn_ref[...], wu_ref[...], preferred_element_type=F32)
    h = jnp.maximum(h, 0.0)
    acc_ref[...] += jnp.dot((h * h).astype(BF16), wd_ref[...], preferred_element_type=F32)

    @pl.when(f == pl.num_programs(1) - 1)
    def _():
        o_ref[...] = x_ref[...] + acc_ref[...]


def _mlp(x, gain, w_up, w_down, layer):
    m, d = x.shape
    ff = w_up.shape[2]
    tm, tf = min(512, m), min(512, ff)
    return pl.pallas_call(
        _mlp_body, grid=(m // tm, ff // tf),
        in_specs=[pl.BlockSpec((tm, d), lambda i, f: (i, 0)),
                  pl.BlockSpec((1, d), lambda i, f: (0, 0)),
                  pl.BlockSpec((None, d, tf), lambda i, f: (layer, 0, f)),
                  pl.BlockSpec((None, tf, d), lambda i, f: (layer, f, 0))],
        out_specs=pl.BlockSpec((tm, d), lambda i, f: (i, 0)),
        out_shape=jax.ShapeDtypeStruct((m, d), F32),
        scratch_shapes=[pltpu.VMEM((tm, d), BF16), pltpu.VMEM((tm, d), F32)],
        compiler_params=_cparams(2), name="mlp")(x, gain.reshape(1, d), w_up, w_down)


def _norm_body(x_ref, g_ref, o_ref):
    o_ref[...] = _rms(x_ref[...], g_ref[...])


def _norm(x, gain):
    m, d = x.shape
    tm = min(512, m)
    return pl.pallas_call(
        _norm_body, grid=(m // tm,),
        in_specs=[pl.BlockSpec((tm, d), lambda i: (i, 0)), pl.BlockSpec((1, d), lambda i: (0, 0))],
        out_specs=pl.BlockSpec((tm, d), lambda i: (i, 0)),
        out_shape=jax.ShapeDtypeStruct((m, d), F32),
        compiler_params=_cparams(1), name="rmsnorm")(x, gain.reshape(1, d))


def _complex_scale(acos, asin, h):
    return acos * h + asin * pltpu.roll(h, h.shape[1] // 2, axis=1)


def _s5_body(*refs, gt, n_chunk, has_h0):
    if has_h0:
        u_ref, m_ref, bc_ref, cp_ref, d_ref, acos_ref, asin_ref, h0_ref, z_ref, hl_ref, h_scr = refs
    else:
        u_ref, m_ref, bc_ref, cp_ref, d_ref, acos_ref, asin_ref, z_ref, hl_ref, h_scr = refs
    rows = u_ref.shape[1]
    n_seq = rows // n_chunk

    def one_group(g, carry):
        u = u_ref[g]
        ub = u.astype(BF16)
        x = jnp.dot(ub, bc_ref[g], preferred_element_type=F32)
        acos, asin = acos_ref[g], asin_ref[g]
        if has_h0:
            h_prev = h0_ref[g]
            h = x + _complex_scale(acos[0:1], asin[0:1], h_prev)
            hl_ref[g] = h
        else:
            kidx = _iota(x.shape, 0) & (n_chunk - 1)
            h = x
            shift, si = 1, 0
            while shift < n_chunk:
                sh = jnp.where(kidx >= shift, pltpu.roll(h, shift, axis=0), 0.0)
                h = h + _complex_scale(acos[si:si + 1], asin[si:si + 1], sh)
                shift, si = shift * 2, si + 1
            h_prev = jnp.where(kidx >= 1, pltpu.roll(h, 1, axis=0), 0.0)
            h_scr[...] = h
            for n in range(n_seq):
                hl_ref[g, pl.ds(n, 1), :] = h_scr[pl.ds((n + 1) * n_chunk - 1, 1), :]
        y = (jnp.dot(ub, m_ref[g], preferred_element_type=F32)
             + jnp.dot(h_prev.astype(BF16), cp_ref[g], preferred_element_type=F32)
             + u * d_ref[g])
        cdf = 0.5 * (1.0 + jnp.tanh(math.sqrt(2.0 / math.pi) * (y + 0.044715 * (y * y * y))))
        z_ref[g] = (y * cdf).astype(z_ref.dtype)
        return carry

    lax.fori_loop(0, gt, one_group, 0)


def _s5_tables(lam_re, lam_im, log_dt, b_re, b_im, c_re, c_im, d, chunk, n_chunk):
    g, p = lam_re.shape
    c = b_re.shape[2]
    ldt = lax.complex(lam_re, lam_im) * jnp.exp(log_dt)[:, None]
    a_bar = jnp.exp(ldt)
    bbar = ((a_bar - 1.0) / lax.complex(lam_re, lam_im))[..., None] * lax.complex(b_re, b_im)
    taus = jnp.arange(chunk + 1, dtype=F32)
    apow = jnp.exp(ldt[:, None, :] * taus[None, :, None])
    cc = lax.complex(c_re, c_im)
    w1 = cc[:, None, :, :] * apow[:, :chunk, None, :]
    lhs = jnp.concatenate([jnp.real(w1), -jnp.imag(w1)], axis=-1)
    rhs = jnp.concatenate([jnp.real(bbar), jnp.imag(bbar)], axis=1)
    kt = jnp.einsum("gtoq,gqi->gtoi", lhs, rhs, precision=HIGHEST)
    kpad = jnp.concatenate([kt, jnp.zeros((g, 1, c, c), F32)], axis=1)
    s_idx = jnp.arange(chunk)[:, None]
    t_idx = jnp.arange(chunk)[None, :]
    tau = jnp.where(t_idx >= s_idx, t_idx - s_idx, chunk)
    mmat = kpad[:, tau].transpose(0, 1, 4, 2, 3).reshape(g, chunk * c, chunk * c)
    rev = apow[:, :chunk][:, ::-1]
    bcx = (rev[:, :, None, :] * bbar.transpose(0, 2, 1)[:, None, :, :]).reshape(g, chunk * c, p)
    bc = jnp.concatenate([jnp.real(bcx), jnp.imag(bcx)], axis=-1)
    ct = (cc[:, None, :, :] * apow[:, 1:, None, :]).transpose(0, 3, 1, 2).reshape(g, p, chunk * c)
    cp = jnp.concatenate([jnp.real(ct), -jnp.imag(ct)], axis=1)
    shifts = [1]
    while shifts[-1] * 2 < n_chunk:
        shifts.append(shifts[-1] * 2)
    while len(shifts) < SUBLANES:
        shifts.append(shifts[-1])
    apl = jnp.exp(ldt[:, None, :] * (chunk * jnp.asarray(shifts, F32))[None, :, None])
    acos = jnp.concatenate([jnp.real(apl), jnp.real(apl)], axis=-1)
    asin = jnp.concatenate([-jnp.imag(apl), jnp.imag(apl)], axis=-1)
    dt = jnp.tile(d.reshape(g, 1, c), (1, chunk, 1)).reshape(g, 1, chunk * c)
    return mmat.astype(BF16), bc.astype(BF16), cp.astype(BF16), dt, acos, asin


def _s5_core(hn, n_seq, t_len, tabs, h0):
    mmat, bc, cp, dt, acos, asin = tabs
    g = mmat.shape[0]
    lc = mmat.shape[1]
    p2 = bc.shape[2]
    c = hn.shape[1] // g
    chunk = lc // c
    n_chunk = t_len // chunk
    rows = n_seq * n_chunk
    u_r = hn.reshape(n_seq, n_chunk, chunk, g, c).transpose(3, 0, 1, 2, 4).reshape(g, rows, lc)
    gt = min(8, g)
    has_h0 = h0 is not None
    grp = lambda shape: pl.BlockSpec((gt,) + shape, lambda i: (i, 0, 0))
    in_specs = [grp((rows, lc)), grp((lc, lc)), grp((lc, p2)), grp((p2, lc)), grp((1, lc)),
                grp((SUBLANES, p2)), grp((SUBLANES, p2))]
    args = [u_r, mmat, bc, cp, dt, acos, asin]
    if has_h0:
        in_specs.append(grp((n_seq, p2)))
        args.append(h0)
    z_r, h_last = pl.pallas_call(
        functools.partial(_s5_body, gt=gt, n_chunk=n_chunk, has_h0=has_h0),
        grid=(g // gt,), in_specs=in_specs,
        out_specs=[grp((rows, lc)), grp((n_seq, p2))],
        out_shape=[jax.ShapeDtypeStruct((g, rows, lc), BF16),
                   jax.ShapeDtypeStruct((g, n_seq, p2), F32)],
        scratch_shapes=[pltpu.VMEM((rows, p2), F32)],
        compiler_params=_cparams(1), name="s5_core")(*args)
    z = z_r.reshape(g, n_seq, n_chunk, chunk, c).transpose(1, 2, 3, 0, 4).reshape(n_seq * t_len, g * c)
    return z, h_last


def _glu_epilogue(accs, tile_refs, const_refs):
    a, gate = accs
    return [tile_refs[0][...] + a * jax.nn.sigmoid(gate)]


def _s5_layer(y, gain, n_seq, t_len, tabs, h0, w_glu, layer):
    m, d = y.shape
    z, h_last = _s5_core(_norm(y, gain), n_seq, t_len, tabs, h0)
    tm, tn = _tiles(m, d)
    blk, imap = _row_block(tm, d)
    (out,) = _fused_matmul("s5_glu", m, d, d, tm, tn, [(z, blk, imap)], [],
                           [(w_glu, layer, 0), (w_glu, layer, d // tn)], [y], [F32],
                           _ident_prologue, _glu_epilogue)
    p = h_last.shape[2] // 2
    h_last = h_last.transpose(1, 0, 2)
    return out, h_last[..., :p], h_last[..., p:]


def _conv_in_epilogue(accs, tile_refs, const_refs):
    gate_b, gate_c, v = accs
    return [gate_b, gate_c * v]


def _conv_prologue(row_refs, const_refs, *, t_len, tm, has_buf):
    w = const_refs[0][...]
    bg, cv = row_refs[0][...], row_refs[1][...]
    t = _mod(pl.program_id(0) * tm + _iota((tm, 1), 0), t_len)
    if has_buf:
        b0, b1 = row_refs[2][...], row_refs[3][...]
        r1 = jnp.where(t >= 1, pltpu.roll(cv, 1, axis=0), b1)
        r2 = jnp.where(t >= 2, pltpu.roll(cv, 2, axis=0), jnp.where(t == 1, b1, b0))
    else:
        full = jnp.concatenate([row_refs[2][...], cv], axis=0)
        r1 = jnp.where(t >= 1, pltpu.roll(full, 1, axis=0)[SUBLANES:], 0.0)
        r2 = jnp.where(t >= 2, pltpu.roll(full, 2, axis=0)[SUBLANES:], 0.0)
    return bg * (w[2:3] * cv + w[1:2] * r1 + w[0:1] * r2)


def _resid_epilogue(accs, tile_refs, const_refs):
    return [tile_refs[0][...] + accs[0]]


def _conv_layer(y, gain, n_seq, t_len, buf, w_in, w_dw, w_out, layer):
    m, d = y.shape
    tm, tn = min(512, m), min(512, d)
    blk, imap = _row_block(tm, d)
    bg, cv = _fused_matmul("conv_in", m, d, d, tm, tn, [(y, blk, imap)], [gain.reshape(1, d)],
                           [(w_in, layer, 0), (w_in, layer, d // tn), (w_in, layer, 2 * d // tn)],
                           [], [F32, F32], _norm_prologue, _conv_in_epilogue)
    tm, tn = _tiles(m, d)
    blk, imap = _row_block(tm, d)
    rows = [(bg, blk, imap), (cv, blk, imap)]
    if buf is None:
        per = tm // SUBLANES
        rows.append((cv, (SUBLANES, d), lambda i, j: (jnp.maximum(i * per - 1, 0), 0)))
    else:
        assert tm == m
        rows.append((jnp.repeat(buf[:, 0], t_len, axis=0), blk, imap))
        rows.append((jnp.repeat(buf[:, 1], t_len, axis=0), blk, imap))
    (out,) = _fused_matmul(
        "conv_out", m, d, d, tm, tn, rows, [w_dw[layer]], [(w_out, layer, 0)], [y], [F32],
        functools.partial(_conv_prologue, t_len=t_len, tm=tm, has_buf=buf is not None),
        _resid_epilogue)
    assert t_len >= 2
    return out, cv.reshape(n_seq, t_len, d)[:, t_len - 2:]


def _sb_logs(z, mask):
    sp = jnp.log(1.0 + jnp.exp(-jnp.abs(z)))
    log_b = jnp.minimum(z, 0.0) - sp
    log_1m = log_b - z
    if mask is not None:
        log_1m = jnp.where(mask, log_1m, 0.0)
    return log_b, log_1m


def _sb_suffix(log_1m, upper):
    hi = log_1m.astype(BF16)
    lo = (log_1m - hi.astype(F32)).astype(BF16)
    return (jnp.dot(hi, upper, preferred_element_type=F32)
            + jnp.dot(lo, upper, preferred_element_type=F32))


def _sb_finish(log_b, log_1m, suffix, mask, run):
    a = jnp.exp(log_b + suffix + run)
    if mask is not None:
        a = jnp.where(mask, a, 0.0)
    return a, run + jnp.sum(log_1m, axis=-1, keepdims=True)


def _sb_weights(z, mask, upper, run):
    log_b, log_1m = _sb_logs(z, mask)
    return _sb_finish(log_b, log_1m, _sb_suffix(log_1m, upper), mask, run)


def _later_key_matrix(tk):
    return (_iota((tk, tk), 0) > _iota((tk, tk), 1)).astype(BF16)


def _qk(q, k_blk):
    return lax.dot_general(q, k_blk, (((1,), (1,)), ((), ())), preferred_element_type=F32)


def _attn_body(bias_ref, q_ref, k_ref, v_ref, o_ref, *, tq, tr, dh, hps):
    hg = pl.program_id(1)
    qi = pl.program_id(2)
    upper = _later_key_matrix(tq)
    units = [(hh, r0) for hh in range(hps) for r0 in range(0, tq, tr)]

    def tile(kb, carries, masked):
        start = pl.multiple_of(kb * tq, tq)
        lanes = lambda hh: slice(hh * dh, (hh + 1) * dh)
        masks = [(_iota((tr, tq), 1) < _iota((tr, tq), 0) + r0) if masked else None
                 for _, r0 in units]
        zs = [_qk(q_ref[r0:r0 + tr, lanes(hh)], k_ref[pl.ds(start, tq), lanes(hh)])
              + bias_ref[hg * hps + hh] for hh, r0 in units]
        logs = [_sb_logs(z, m) for z, m in zip(zs, masks)]
        sufs = [_sb_suffix(l1m, upper) for _, l1m in logs]
        fins = [_sb_finish(lb, l1m, suf, m, run)
                for (lb, l1m), suf, m, (run, _) in zip(logs, sufs, masks, carries)]
        accs = [acc + jnp.dot(a.astype(BF16), v_ref[pl.ds(start, tq), lanes(hh)],
                              preferred_element_type=F32)
                for (a, _), (_, acc), (hh, _) in zip(fins, carries, units)]
        return tuple((run, acc) for (_, run), acc in zip(fins, accs))

    carries = tuple((jnp.zeros((tr, 1), F32), jnp.zeros((tr, dh), F32)) for _ in units)
    carries = tile(qi, carries, True)
    carries = lax.fori_loop(0, qi, lambda it, c: tile(qi - 1 - it, c, False), carries)
    for (hh, r0), (_, acc) in zip(units, carries):
        o_ref[r0:r0 + tr, hh * dh:(hh + 1) * dh] = acc.astype(o_ref.dtype)


def _attn_prompt(q, k, v, bias, n_seq, t_len, n_heads):
    m, d = q.shape
    dh = d // n_heads
    tq = min(256, t_len)
    hps = 2 if n_heads % 2 == 0 else 1
    qb = t_len // tq
    grid_spec = pltpu.PrefetchScalarGridSpec(
        num_scalar_prefetch=1, grid=(n_seq, n_heads // hps, qb),
        in_specs=[pl.BlockSpec((tq, hps * dh), lambda n, h, i, b: (n * qb + i, h)),
                  pl.BlockSpec((t_len, hps * dh), lambda n, h, i, b: (n, h)),
                  pl.BlockSpec((t_len, hps * dh), lambda n, h, i, b: (n, h))],
        out_specs=pl.BlockSpec((tq, hps * dh), lambda n, h, i, b: (n * qb + i, h)))
    return pl.pallas_call(
        functools.partial(_attn_body, tq=tq, tr=min(128, tq), dh=dh, hps=hps),
        grid_spec=grid_spec,
        out_shape=jax.ShapeDtypeStruct((m, d), BF16),
        compiler_params=_cparams(3), name="sb_attn_prompt")(bias, q, k, v)


def _decode_body(pt_ref, q_ref, bias_ref, e_ref, hm_ref, *refs, n_heads, ppt):
    kn_refs, vn_refs = refs[:ppt], refs[ppt:2 * ppt]
    kc_refs, vc_refs = refs[2 * ppt:3 * ppt], refs[3 * ppt:4 * ppt]
    o_ref, run_ref, acc_ref = refs[4 * ppt:]
    step = pl.program_id(1)
    page = vn_refs[0].shape[0] // n_heads
    rows = q_ref.shape[0]
    tk = ppt * page

    def tile(k_refs, v_refs, mask, run, acc):
        pages = []
        for slot in range(ppt):
            heads = [k_refs[slot][pl.ds(h, page, stride=n_heads), :].astype(BF16)
                     for h in range(n_heads)]
            pages.append(jnp.concatenate(heads, axis=1))
        k_wide = pages[0] if ppt == 1 else jnp.concatenate(pages, axis=0)
        z = _qk(q_ref[...], k_wide) + bias_ref[...]
        a, run = _sb_weights(z, mask, _later_key_matrix(tk), run)
        for slot in range(ppt):
            a_p = a[:, slot * page:(slot + 1) * page].astype(BF16)
            a_rows = jnp.dot(a_p, e_ref[...], preferred_element_type=F32).astype(BF16) * hm_ref[...]
            acc = acc + jnp.dot(a_rows, v_refs[slot][...].astype(BF16),
                                preferred_element_type=F32)
        run_ref[...] = jnp.broadcast_to(run, run_ref.shape)
        acc_ref[...] = acc

    @pl.when(step == 0)
    def _():
        mask = _iota((rows, tk), 1) < _div(_iota((rows, tk), 0), n_heads)
        tile(kn_refs, vn_refs, mask, jnp.zeros((rows, 1), F32), jnp.zeros(acc_ref.shape, F32))

    @pl.when(step > 0)
    def _():
        tile(kc_refs, vc_refs, None, run_ref[:, 0:1], acc_ref[...])

    @pl.when(step == pl.num_programs(1) - 1)
    def _():
        o_ref[...] = acc_ref[...]


def _attn_decode(q, k_new, v_new, bias, cache_k, cache_v, layer, page_table, n_seq, t_len,
                 n_heads):
    d = q.shape[1]
    dh = d // n_heads
    n_layers, n_pool, page = cache_k.shape[:3]
    n_pages = page_table.shape[1]
    ppt = 2 if n_pages % 2 == 0 else 1
    n_tiles = n_pages // ppt
    rows = t_len * n_heads
    assert n_heads % SUBLANES == 0 and t_len <= page
    head_of_lane = jnp.arange(d) // dh
    q_bd = jnp.where(head_of_lane[None, :] == jnp.arange(n_heads)[:, None],
                     q.reshape(n_seq, t_len, 1, d), 0).astype(BF16).reshape(n_seq, rows, d)
    bias_rows = jnp.tile(bias, t_len).reshape(rows, 1)
    key_of_row = jnp.arange(page * n_heads) // n_heads
    expand = (key_of_row[None, :] == jnp.arange(page)[:, None]).astype(BF16)
    head_of_row = jnp.arange(page * n_heads) % n_heads
    head_mask = (head_of_row[None, :] == (jnp.arange(rows) % n_heads)[:, None]).astype(BF16)
    pad_t = ((0, 0), (0, ppt * page - t_len), (0, 0), (0, 0))
    as_rows = lambda x: jnp.pad(x.reshape(n_seq, t_len, n_heads, dh), pad_t).reshape(
        n_seq, ppt, page * n_heads, dh)
    kn, vn = as_rows(k_new), as_rows(v_new)
    kc = cache_k.reshape(n_layers, n_pool, page * n_heads, dh)
    vc = cache_v.reshape(n_layers, n_pool, page * n_heads, dh)

    def phys(n, s, pt, slot):
        return pt[n, (n_tiles - jnp.maximum(s, 1)) * ppt + slot]

    const = lambda shape: pl.BlockSpec(shape, lambda n, s, pt: (0,) * len(shape))
    in_specs = [pl.BlockSpec((None, rows, d), lambda n, s, pt: (n, 0, 0)),
                const((rows, 1)), const(expand.shape), const(head_mask.shape)]
    args = [q_bd, bias_rows, expand, head_mask]
    blk = (None, None, page * n_heads, dh)
    for arr in (kn, vn):
        for slot in range(ppt):
            in_specs.append(pl.BlockSpec(blk, lambda n, s, pt, slot=slot: (n, slot, 0, 0)))
            args.append(arr)
    for arr in (kc, vc):
        for slot in range(ppt):
            in_specs.append(pl.BlockSpec(
                blk, lambda n, s, pt, slot=slot: (layer, phys(n, s, pt, slot), 0, 0)))
            args.append(arr)
    grid_spec = pltpu.PrefetchScalarGridSpec(
        num_scalar_prefetch=1, grid=(n_seq, n_tiles + 1), in_specs=in_specs,
        out_specs=pl.BlockSpec((None, rows, dh), lambda n, s, pt: (n, 0, 0)),
        scratch_shapes=[pltpu.VMEM((rows, LANES), F32), pltpu.VMEM((rows, dh), F32)])
    o = pl.pallas_call(
        functools.partial(_decode_body, n_heads=n_heads, ppt=ppt), grid_spec=grid_spec,
        out_shape=jax.ShapeDtypeStruct((n_seq, rows, dh), F32),
        compiler_params=_cparams(2), name="sb_attn_decode")(page_table, *args)
    return o.reshape(n_seq * t_len, d).astype(BF16)


def _attn_layer(y, gain, n_seq, t_len, n_heads, w_qkv, q_gain, k_gain, sb_bias, w_o, layer,
                cache=None):
    m, d = y.shape
    dh = d // n_heads
    scale = dh ** -0.5
    qn = lambda accs, t, c: [_head_norm(accs[0], c[1][...], scale)]
    kn = lambda accs, t, c: [_head_norm(accs[0], c[1][...], 1.0)] * 2
    vn = lambda accs, t, c: [accs[0]] * 2
    (q,) = _norm_proj("attn_q", y, gain, w_qkv, layer, 0, d, [BF16], qn, [q_gain.reshape(1, dh)])
    k32, kb = _norm_proj("attn_k", y, gain, w_qkv, layer, d, d, [F32, BF16], kn,
                         [k_gain.reshape(1, dh)])
    v32, vb = _norm_proj("attn_v", y, gain, w_qkv, layer, 2 * d, d, [F32, BF16], vn)
    if cache is None:
        o = _attn_prompt(q, kb, vb, sb_bias, n_seq, t_len, n_heads)
    else:
        cache_k, cache_v, page_table = cache
        o = _attn_decode(q, k32, v32, sb_bias, cache_k, cache_v, layer, page_table, n_seq, t_len,
                         n_heads)
    tm, tn = _tiles(m, d)
    blk, imap = _row_block(tm, d)
    (out,) = _fused_matmul("attn_out", m, d, d, tm, tn, [(o, blk, imap)], [],
                           [(w_o, layer, 0)], [y], [F32], _ident_prologue, _resid_epilogue)
    shape = (n_seq, t_len, n_heads, dh)
    return out, k32.reshape(shape), v32.reshape(shape)


def kernel(x_prompt, x_sample, state_ssm_re, state_ssm_im, state_conv, cache_k, cache_v, page_table, norm_mix, norm_mlp, ssm_lambda_re, ssm_lambda_im, ssm_log_dt, ssm_b_re, ssm_b_im, ssm_c_re, ssm_c_im, ssm_d, ssm_w_glu, conv_w_in, conv_w_dw, conv_w_out, attn_w_qkv, attn_q_norm, attn_k_norm, attn_sb_bias, attn_w_o, mlp_w_up, mlp_w_down):
    n_p, t_p, d = x_prompt.shape
    n_s, t_s, _ = x_sample.shape
    depth = norm_mix.shape[0]
    n_heads = attn_sb_bias.shape[1]
    assert conv_w_dw.shape[1] == 3 and d // n_heads == LANES
    chunk_p = min(S5_CHUNK, t_p)
    assert t_p % chunk_p == 0 and (t_p // chunk_p) & (t_p // chunk_p - 1) == 0

    w_glu, w_in, w_out = (w.astype(BF16) for w in (ssm_w_glu, conv_w_in, conv_w_out))
    w_qkv, w_o = attn_w_qkv.astype(BF16), attn_w_o.astype(BF16)
    w_up, w_down = mlp_w_up.astype(BF16), mlp_w_down.astype(BF16)

    y_p = x_prompt.reshape(n_p * t_p, d)
    y_s = x_sample.reshape(n_s * t_s, d)
    outs = {k: [] for k in ("re_p", "im_p", "re_s", "im_s", "cv_p", "cv_s", "k_p", "v_p", "k_s", "v_s")}
    for i in range(depth):
        kind, j = i % 3, i // 3
        if kind == 0:
            ssm = (ssm_lambda_re[j], ssm_lambda_im[j], ssm_log_dt[j], ssm_b_re[j], ssm_b_im[j],
                   ssm_c_re[j], ssm_c_im[j], ssm_d[j])
            tabs_p = _s5_tables(*ssm, chunk_p, t_p // chunk_p)
            tabs_s = _s5_tables(*ssm, t_s, 1)
            h0 = jnp.concatenate([state_ssm_re[j], state_ssm_im[j]], axis=-1).transpose(1, 0, 2)
            y_p, re_p, im_p = _s5_layer(y_p, norm_mix[i], n_p, t_p, tabs_p, None, w_glu, j)
            y_s, re_s, im_s = _s5_layer(y_s, norm_mix[i], n_s, t_s, tabs_s, h0, w_glu, j)
            outs["re_p"].append(re_p); outs["im_p"].append(im_p)
            outs["re_s"].append(re_s); outs["im_s"].append(im_s)
        elif kind == 1:
            y_p, cv_p = _conv_layer(y_p, norm_mix[i], n_p, t_p, None, w_in, conv_w_dw, w_out, j)
            y_s, cv_s = _conv_layer(y_s, norm_mix[i], n_s, t_s, state_conv[j], w_in, conv_w_dw,
                                    w_out, j)
            outs["cv_p"].append(cv_p); outs["cv_s"].append(cv_s)
        else:
            attn = (w_qkv, attn_q_norm[j], attn_k_norm[j], attn_sb_bias[j], w_o, j)
            y_p, k_p, v_p = _attn_layer(y_p, norm_mix[i], n_p, t_p, n_heads, *attn)
            y_s, k_s, v_s = _attn_layer(y_s, norm_mix[i], n_s, t_s, n_heads, *attn,
                                        cache=(cache_k, cache_v, page_table))
            outs["k_p"].append(k_p); outs["v_p"].append(v_p)
            outs["k_s"].append(k_s); outs["v_s"].append(v_s)
        y_p = _mlp(y_p, norm_mlp[i], w_up, w_down, i)
        y_s = _mlp(y_s, norm_mlp[i], w_up, w_down, i)
    st = lambda key: jnp.stack(outs[key])
    return (y_p.reshape(n_p, t_p, d), y_s.reshape(n_s, t_s, d),
            st("re_p"), st("im_p"), st("re_s"), st("im_s"), st("cv_p"), st("cv_s"),
            st("k_p"), st("v_p"), st("k_s"), st("v_s"))
```

```python
import functools
import math

import jax
import jax.numpy as jnp
from jax import lax
from jax.experimental import pallas as pl
from jax.experimental.pallas import tpu as pltpu

F32 = jnp.float32
BF16 = jnp.bfloat16
EPS = 1e-6
LANES = 128
SUBLANES = 8
VMEM_LIMIT_BYTES = 52 * 1024 * 1024
S5_CHUNK = 16
HIGHEST = lax.Precision.HIGHEST


def _cparams(n_axes):
    return pltpu.CompilerParams(dimension_semantics=("arbitrary",) * n_axes,
                                vmem_limit_bytes=VMEM_LIMIT_BYTES)


def _rms(x, g):
    ms = jnp.mean(x * x, axis=-1, keepdims=True)
    return x * lax.rsqrt(ms + EPS) * g


def _iota(shape, dim):
    return lax.broadcasted_iota(jnp.int32, shape, dim)


def _mod(x, n):
    return x & (n - 1) if n & (n - 1) == 0 else lax.rem(x, n)


def _div(x, n):
    return x >> (n.bit_length() - 1) if n & (n - 1) == 0 else lax.div(x, n)


def _fused_matmul(name, m, kdim, n_out, tm, tn, row_in, const_in, weights, tile_in,
                  out_dtypes, prologue, epilogue):
    n_row, n_const, n_w, n_tile, n_o = (len(row_in), len(const_in), len(weights),
                                        len(tile_in), len(out_dtypes))
    in_specs, args = [], []
    for arr, blk, imap in row_in:
        in_specs.append(pl.BlockSpec(blk, imap))
        args.append(arr)
    for arr in const_in:
        in_specs.append(pl.BlockSpec(arr.shape, lambda i, j, nd=arr.ndim: (0,) * nd))
        args.append(arr)
    for arr, layer, off in weights:
        in_specs.append(pl.BlockSpec((None, kdim, tn),
                                     lambda i, j, layer=layer, off=off: (layer, 0, j + off)))
        args.append(arr)
    for arr in tile_in:
        in_specs.append(pl.BlockSpec((tm, tn), lambda i, j: (i, j)))
        args.append(arr)
    out_shape = [jax.ShapeDtypeStruct((m, n_out), dt) for dt in out_dtypes]
    out_specs = [pl.BlockSpec((tm, tn), lambda i, j: (i, j)) for _ in out_dtypes]

    def body(*refs):
        p = 0
        row_refs = refs[p:p + n_row]; p += n_row
        const_refs = refs[p:p + n_const]; p += n_const
        w_refs = refs[p:p + n_w]; p += n_w
        tile_refs = refs[p:p + n_tile]; p += n_tile
        out_refs = refs[p:p + n_o]; p += n_o
        lhs_ref = refs[p]

        @pl.when(pl.program_id(1) == 0)
        def _():
            lhs_ref[...] = prologue(row_refs, const_refs).astype(BF16)

        lhs = lhs_ref[...]
        accs = [jnp.dot(lhs, w[...], preferred_element_type=F32) for w in w_refs]
        for o_ref, val in zip(out_refs, epilogue(accs, tile_refs, const_refs)):
            o_ref[...] = val.astype(o_ref.dtype)

    return pl.pallas_call(
        body, grid=(m // tm, n_out // tn), in_specs=in_specs, out_specs=out_specs,
        out_shape=out_shape, scratch_shapes=[pltpu.VMEM((tm, kdim), BF16)],
        compiler_params=_cparams(2), name=name)(*args)


def _row_block(tm, kdim):
    return (tm, kdim), (lambda i, j: (i, 0))


def _norm_prologue(row_refs, const_refs):
    return _rms(row_refs[0][...], const_refs[0][...])


def _ident_prologue(row_refs, const_refs):
    return row_refs[0][...]


def _head_norm(acc, gain, scale):
    segs = []
    for h in range(acc.shape[1] // LANES):
        seg = acc[:, h * LANES:(h + 1) * LANES]
        ms = jnp.mean(seg * seg, axis=-1, keepdims=True)
        y = seg * lax.rsqrt(ms + EPS) * gain
        segs.append(y * scale if scale != 1.0 else y)
    return segs[0] if len(segs) == 1 else jnp.concatenate(segs, axis=-1)


def _tiles(m, n):
    return min(512, m), min(1024, n)


def _norm_proj(name, x, gain, w, layer, col_off, n_out, out_dtypes, epilogue, extra_const=()):
    m, kdim = x.shape
    tm, tn = _tiles(m, n_out)
    blk, imap = _row_block(tm, kdim)
    return _fused_matmul(name, m, kdim, n_out, tm, tn, [(x, blk, imap)],
                         [gain.reshape(1, kdim)] + list(extra_const),
                         [(w, layer, col_off // tn)], [], out_dtypes, _norm_prologue, epilogue)


def _mlp_body(x_ref, xs_ref, g_ref, wu_ref, wd_ref, o_ref, os_ref, xn_ref, acc_ref, xsn_ref,
              accs_ref):
    i, f = pl.program_id(0), pl.program_id(1)
    last = pl.num_programs(1) - 1
    wu = wu_ref[...].astype(BF16)
    wd = wd_ref[...].astype(BF16)

    def step(src_ref, dst_ref, n_ref, a_ref):
        @pl.when(f == 0)
        def _():
            n_ref[...] = _rms(src_ref[...], g_ref[...]).astype(BF16)
            a_ref[...] = jnp.zeros_like(a_ref)

        h = jnp.maximum(jnp.dot(n_ref[...], wu, preferred_element_type=F32), 0.0)
        a_ref[...] += jnp.dot((h * h).astype(BF16), wd, preferred_element_type=F32)

        @pl.when(f == last)
        def _():
            dst_ref[...] = src_ref[...] + a_ref[...]

    step(x_ref, o_ref, xn_ref, acc_ref)
    pl.when(i == 0)(lambda: step(xs_ref, os_ref, xsn_ref, accs_ref))


def _mlp(x, xs, gain, w_up, w_down, layer):
    m, d = x.shape
    ms = xs.shape[0]
    ff = w_up.shape[2]
    tm, tf = min(1024, m), min(512, ff)
    once = pl.Buffered(1)
    return pl.pallas_call(
        _mlp_body, grid=(m // tm, ff // tf),
        in_specs=[pl.BlockSpec((tm, d), lambda i, f: (i, 0), pipeline_mode=once),
                  pl.BlockSpec((ms, d), lambda i, f: (0, 0)),
                  pl.BlockSpec((1, d), lambda i, f: (0, 0)),
                  pl.BlockSpec((None, d, tf), lambda i, f: (layer, 0, f)),
                  pl.BlockSpec((None, tf, d), lambda i, f: (layer, f, 0))],
        out_specs=[pl.BlockSpec((tm, d), lambda i, f: (i, 0), pipeline_mode=once),
                   pl.BlockSpec((ms, d), lambda i, f: (0, 0))],
        out_shape=[jax.ShapeDtypeStruct((m, d), F32), jax.ShapeDtypeStruct((ms, d), F32)],
        scratch_shapes=[pltpu.VMEM((tm, d), BF16), pltpu.VMEM((tm, d), F32),
                        pltpu.VMEM((ms, d), BF16), pltpu.VMEM((ms, d), F32)],
        compiler_params=_cparams(2), name="mlp")(x, xs, gain.reshape(1, d), w_up, w_down)


def _norm_body(x_ref, g_ref, o_ref):
    o_ref[...] = _rms(x_ref[...], g_ref[...])


def _norm(x, gain):
    m, d = x.shape
    tm = min(512, m)
    return pl.pallas_call(
        _norm_body, grid=(m // tm,),
        in_specs=[pl.BlockSpec((tm, d), lambda i: (i, 0)), pl.BlockSpec((1, d), lambda i: (0, 0))],
        out_specs=pl.BlockSpec((tm, d), lambda i: (i, 0)),
        out_shape=jax.ShapeDtypeStruct((m, d), F32),
        compiler_params=_cparams(1), name="rmsnorm")(x, gain.reshape(1, d))


def _complex_scale(acos, asin, h):
    return acos * h + asin * pltpu.roll(h, h.shape[1] // 2, axis=1)


def _gelu(y):
    return y * (0.5 * (1.0 + jnp.tanh(math.sqrt(2.0 / math.pi) * (y + 0.044715 * (y * y * y)))))


def _s5_step_body(u_ref, m_ref, bc_ref, cp_ref, d_ref, acos_ref, asin_ref, h0_ref, z_ref, hl_ref,
                  *, gt):
    def one_group(g, carry):
        u = u_ref[g]
        ub = u.astype(BF16)
        h_prev = h0_ref[g]
        hl_ref[g] = (jnp.dot(ub, bc_ref[g], preferred_element_type=F32)
                     + _complex_scale(acos_ref[g][0:1], asin_ref[g][0:1], h_prev))
        y = (jnp.dot(ub, m_ref[g], preferred_element_type=F32)
             + jnp.dot(h_prev.astype(BF16), cp_ref[g], preferred_element_type=F32)
             + u * d_ref[g])
        z_ref[g] = _gelu(y).astype(z_ref.dtype)
        return carry

    lax.fori_loop(0, gt, one_group, 0)


def _s5_seq_body(x_ref, bd_ref, bct_ref, cpt_ref, acos_ref, asin_ref, z_ref, hl_ref,
                 m8_ref, bc8_ref, cp8_ref, h_ref, *, chunk, n_chunk, n_seq, c):
    gl = LANES // c
    rows = n_seq * n_chunk
    blk = lambda i: slice(i * LANES, (i + 1) * LANES)
    m8_ref[...] = jnp.zeros(m8_ref.shape, BF16)
    for s in range(chunk):
        for t in range(s, chunk):
            m8_ref[blk(s), blk(t)] = bd_ref[t - s]
    row_grp = _div(_iota((LANES, LANES), 0), c)
    lane_grp = _div(_iota((LANES, LANES), 1), c)
    for s in range(chunk):
        for g in range(gl):
            bc8_ref[blk(s), blk(g)] = jnp.where(row_grp == g, bct_ref[s], 0.0).astype(BF16)
            cp8_ref[blk(g), blk(s)] = jnp.where(lane_grp == g, cpt_ref[s], 0.0).astype(BF16)

    u2 = jnp.concatenate([x_ref[pl.ds(s, rows, stride=chunk), :].astype(BF16)
                          for s in range(chunk)], axis=1)
    h = jnp.dot(u2, bc8_ref[...], preferred_element_type=F32)
    kidx = _iota((rows, 1), 0) & (n_chunk - 1)
    swap = lambda v: jnp.concatenate(
        [pltpu.roll(v[:, blk(g)], LANES // 2, axis=1) for g in range(gl)], axis=1)
    shift, si = 1, 0
    while shift < n_chunk:
        sh = jnp.where(kidx >= shift, pltpu.roll(h, shift, axis=0), 0.0)
        h = h + acos_ref[si:si + 1, :] * sh + asin_ref[si:si + 1, :] * swap(sh)
        shift, si = shift * 2, si + 1
    h_prev = jnp.where(kidx >= 1, pltpu.roll(h, 1, axis=0), 0.0)
    y = (jnp.dot(u2, m8_ref[...], preferred_element_type=F32)
         + jnp.dot(h_prev.astype(BF16), cp8_ref[...], preferred_element_type=F32))
    z = _gelu(y)
    for t in range(chunk):
        z_ref[pl.ds(t, rows, stride=chunk), :] = z[:, blk(t)]
    h_ref[...] = h
    for n in range(n_seq):
        hl_ref[pl.ds(n, 1), :] = h_ref[pl.ds((n + 1) * n_chunk - 1, 1), :]


def _s5_tables(lam_re, lam_im, log_dt, b_re, b_im, c_re, c_im, d, chunk, n_chunk, per_tile):
    g, p = lam_re.shape
    c = b_re.shape[2]
    ldt = lax.complex(lam_re, lam_im) * jnp.exp(log_dt)[:, None]
    a_bar = jnp.exp(ldt)
    bbar = ((a_bar - 1.0) / lax.complex(lam_re, lam_im))[..., None] * lax.complex(b_re, b_im)
    taus = jnp.arange(chunk + 1, dtype=F32)
    apow = jnp.exp(ldt[:, None, :] * taus[None, :, None])
    cc = lax.complex(c_re, c_im)
    w1 = cc[:, None, :, :] * apow[:, :chunk, None, :]
    lhs = jnp.concatenate([jnp.real(w1), -jnp.imag(w1)], axis=-1)
    rhs = jnp.concatenate([jnp.real(bbar), jnp.imag(bbar)], axis=1)
    kt = jnp.einsum("gtoq,gqi->gtoi", lhs, rhs, precision=HIGHEST)
    rev = apow[:, :chunk][:, ::-1]
    bcx = (rev[:, :, None, :] * bbar.transpose(0, 2, 1)[:, None, :, :]).reshape(g, chunk * c, p)
    bc = jnp.concatenate([jnp.real(bcx), jnp.imag(bcx)], axis=-1)
    ct = (cc[:, None, :, :] * apow[:, 1:, None, :]).transpose(0, 3, 1, 2).reshape(g, p, chunk * c)
    cp = jnp.concatenate([jnp.real(ct), -jnp.imag(ct)], axis=1)
    shifts = [1]
    while shifts[-1] * 2 < n_chunk:
        shifts.append(shifts[-1] * 2)
    while len(shifts) < SUBLANES:
        shifts.append(shifts[-1])
    apl = jnp.exp(ldt[:, None, :] * (chunk * jnp.asarray(shifts, F32))[None, :, None])
    acos = jnp.concatenate([jnp.real(apl), jnp.real(apl)], axis=-1)
    asin = jnp.concatenate([-jnp.imag(apl), jnp.imag(apl)], axis=-1)
    if not per_tile:
        kpad = jnp.concatenate([kt, jnp.zeros((g, 1, c, c), F32)], axis=1)
        s_idx = jnp.arange(chunk)[:, None]
        t_idx = jnp.arange(chunk)[None, :]
        tau = jnp.where(t_idx >= s_idx, t_idx - s_idx, chunk)
        mmat = kpad[:, tau].transpose(0, 1, 4, 2, 3).reshape(g, chunk * c, chunk * c)
        dt = jnp.tile(d.reshape(g, 1, c), (1, chunk, 1)).reshape(g, 1, chunk * c)
        return mmat.astype(BF16), bc.astype(BF16), cp.astype(BF16), dt, acos, asin
    gl = LANES // c
    nt = g // gl
    p2 = 2 * p
    kd = kt.at[:, 0].add(d.reshape(g, c)[:, :, None] * jnp.eye(c, dtype=F32))
    ktr = kd.reshape(nt, gl, chunk, c, c).transpose(0, 2, 1, 4, 3)
    bd = ktr[:, :, :, :, None, :] * jnp.eye(gl, dtype=F32)[None, None, :, None, :, None]
    bd = bd.reshape(nt, chunk, LANES, LANES).astype(BF16)
    bct = bc.reshape(nt, gl, chunk, c, p2).transpose(0, 2, 1, 3, 4).reshape(nt, chunk, LANES, p2)
    cpt = cp.reshape(nt, gl, p2, chunk, c).transpose(0, 3, 2, 1, 4).reshape(nt, chunk, p2, LANES)
    tile = lambda a: a.reshape(nt, gl, SUBLANES, p2).transpose(0, 2, 1, 3).reshape(
        nt, SUBLANES, gl * p2)
    return bd, bct, cpt, tile(acos), tile(asin)


def _s5_step(hn, n_seq, t_len, tabs, h0):
    mmat, bc, cp, dt, acos, asin = tabs
    g, lc = mmat.shape[:2]
    p2 = bc.shape[2]
    c = lc // t_len
    u_r = hn.reshape(n_seq, t_len, g, c).transpose(2, 0, 1, 3).reshape(g, n_seq, lc)
    gt = min(8, g)
    grp = lambda shape: pl.BlockSpec((gt,) + shape, lambda i: (i, 0, 0))
    z_r, h_last = pl.pallas_call(
        functools.partial(_s5_step_body, gt=gt), grid=(g // gt,),
        in_specs=[grp((n_seq, lc)), grp((lc, lc)), grp((lc, p2)), grp((p2, lc)), grp((1, lc)),
                  grp((SUBLANES, p2)), grp((SUBLANES, p2)), grp((n_seq, p2))],
        out_specs=[grp((n_seq, lc)), grp((n_seq, p2))],
        out_shape=[jax.ShapeDtypeStruct((g, n_seq, lc), BF16),
                   jax.ShapeDtypeStruct((g, n_seq, p2), F32)],
        compiler_params=_cparams(1), name="s5_step")(u_r, mmat, bc, cp, dt, acos, asin, h0)
    z = z_r.reshape(g, n_seq, t_len, c).transpose(1, 2, 0, 3).reshape(n_seq * t_len, g * c)
    return z, h_last.transpose(1, 0, 2)


def _s5_seq(hn, n_seq, t_len, tabs, chunk):
    bd, bct, cpt, acos, asin = tabs
    m, d = hn.shape
    nt = bd.shape[0]
    p2 = bct.shape[3]
    gl = acos.shape[2] // p2
    c = LANES // gl
    n_chunk = t_len // chunk
    assert p2 == LANES and d == nt * LANES
    once = pl.Buffered(1)
    tab = lambda a: pl.BlockSpec((None,) + a.shape[1:], lambda j: (j,) + (0,) * (a.ndim - 1))
    z, h_last = pl.pallas_call(
        functools.partial(_s5_seq_body, chunk=chunk, n_chunk=n_chunk, n_seq=n_seq, c=c),
        grid=(nt,),
        in_specs=[pl.BlockSpec((m, LANES), lambda j: (0, j), pipeline_mode=once),
                  tab(bd), tab(bct), tab(cpt), tab(acos), tab(asin)],
        out_specs=[pl.BlockSpec((m, LANES), lambda j: (0, j), pipeline_mode=once),
                   pl.BlockSpec((None, n_seq, gl * p2), lambda j: (j, 0, 0))],
        out_shape=[jax.ShapeDtypeStruct((m, d), F32),
                   jax.ShapeDtypeStruct((nt, n_seq, gl * p2), F32)],
        scratch_shapes=[pltpu.VMEM((chunk * LANES, chunk * LANES), BF16),
                        pltpu.VMEM((chunk * LANES, gl * p2), BF16),
                        pltpu.VMEM((gl * p2, chunk * LANES), BF16),
                        pltpu.VMEM((n_seq * n_chunk, gl * p2), F32)],
        compiler_params=_cparams(1), name="s5_seq")(hn, bd, bct, cpt, acos, asin)
    h_last = h_last.reshape(nt, n_seq, gl, p2).transpose(1, 0, 2, 3).reshape(n_seq, nt * gl, p2)
    return z, h_last


def _glu_epilogue(accs, tile_refs, const_refs):
    a, gate = accs
    return [tile_refs[0][...] + a * jax.nn.sigmoid(gate)]


def _s5_layer(y, gain, n_seq, t_len, tabs, h0, w_glu, layer, chunk):
    m, d = y.shape
    if h0 is None:
        z, h_last = _s5_seq(_norm(y, gain), n_seq, t_len, tabs, chunk)
    else:
        z, h_last = _s5_step(_norm(y, gain), n_seq, t_len, tabs, h0)
    tm, tn = _tiles(m, d)
    blk, imap = _row_block(tm, d)
    (out,) = _fused_matmul("s5_glu", m, d, d, tm, tn, [(z, blk, imap)], [],
                           [(w_glu, layer, 0), (w_glu, layer, d // tn)], [y], [F32],
                           _ident_prologue, _glu_epilogue)
    p = h_last.shape[2] // 2
    return out, h_last[..., :p], h_last[..., p:]


def _conv_in_epilogue(accs, tile_refs, const_refs):
    gate_b, gate_c, v = accs
    return [gate_b, gate_c * v]


def _conv_prologue(row_refs, const_refs, *, t_len, tm, has_buf):
    w = const_refs[0][...]
    bg, cv = row_refs[0][...], row_refs[1][...]
    t = _mod(pl.program_id(0) * tm + _iota((tm, 1), 0), t_len)
    if has_buf:
        b0, b1 = row_refs[2][...], row_refs[3][...]
        r1 = jnp.where(t >= 1, pltpu.roll(cv, 1, axis=0), b1)
        r2 = jnp.where(t >= 2, pltpu.roll(cv, 2, axis=0), jnp.where(t == 1, b1, b0))
    else:
        full = jnp.concatenate([row_refs[2][...], cv], axis=0)
        r1 = jnp.where(t >= 1, pltpu.roll(full, 1, axis=0)[SUBLANES:], 0.0)
        r2 = jnp.where(t >= 2, pltpu.roll(full, 2, axis=0)[SUBLANES:], 0.0)
    return bg * (w[2:3] * cv + w[1:2] * r1 + w[0:1] * r2)


def _resid_epilogue(accs, tile_refs, const_refs):
    return [tile_refs[0][...] + accs[0]]


def _conv_layer(y, gain, n_seq, t_len, buf, w_in, w_dw, w_out, layer):
    m, d = y.shape
    tm, tn = min(512, m), min(512, d)
    blk, imap = _row_block(tm, d)
    bg, cv = _fused_matmul("conv_in", m, d, d, tm, tn, [(y, blk, imap)], [gain.reshape(1, d)],
                           [(w_in, layer, 0), (w_in, layer, d // tn), (w_in, layer, 2 * d // tn)],
                           [], [F32, F32], _norm_prologue, _conv_in_epilogue)
    tm, tn = _tiles(m, d)
    blk, imap = _row_block(tm, d)
    rows = [(bg, blk, imap), (cv, blk, imap)]
    if buf is None:
        per = tm // SUBLANES
        rows.append((cv, (SUBLANES, d), lambda i, j: (jnp.maximum(i * per - 1, 0), 0)))
    else:
        assert tm == m
        rows.append((jnp.repeat(buf[:, 0], t_len, axis=0), blk, imap))
        rows.append((jnp.repeat(buf[:, 1], t_len, axis=0), blk, imap))
    (out,) = _fused_matmul(
        "conv_out", m, d, d, tm, tn, rows, [w_dw[layer]], [(w_out, layer, 0)], [y], [F32],
        functools.partial(_conv_prologue, t_len=t_len, tm=tm, has_buf=buf is not None),
        _resid_epilogue)
    assert t_len >= 2
    return out, cv.reshape(n_seq, t_len, d)[:, t_len - 2:]


def _sb_logs(z, mask):
    sp = jnp.log(1.0 + jnp.exp(-jnp.abs(z)))
    log_b = jnp.minimum(z, 0.0) - sp
    log_1m = log_b - z
    if mask is not None:
        log_1m = jnp.where(mask, log_1m, 0.0)
    return log_b, log_1m


def _sb_suffix(log_1m, upper):
    hi = log_1m.astype(BF16)
    lo = (log_1m - hi.astype(F32)).astype(BF16)
    return (jnp.dot(hi, upper, preferred_element_type=F32)
            + jnp.dot(lo, upper, preferred_element_type=F32))


def _sb_finish(log_b, log_1m, suffix, mask, run):
    a = jnp.exp(log_b + suffix + run)
    if mask is not None:
        a = jnp.where(mask, a, 0.0)
    return a, run + jnp.sum(log_1m, axis=-1, keepdims=True)


def _sb_weights(z, mask, upper, run):
    log_b, log_1m = _sb_logs(z, mask)
    return _sb_finish(log_b, log_1m, _sb_suffix(log_1m, upper), mask, run)


def _later_key_matrix(tk):
    return (_iota((tk, tk), 0) > _iota((tk, tk), 1)).astype(BF16)


def _qk(q, k_blk):
    return lax.dot_general(q, k_blk, (((1,), (1,)), ((), ())), preferred_element_type=F32)


def _attn_body(bias_ref, q_ref, k_ref, v_ref, o_ref, *, tq, tr, dh, hps):
    hg = pl.program_id(1)
    qi = pl.program_id(2)
    upper = _later_key_matrix(tq)
    units = [(hh, r0) for hh in range(hps) for r0 in range(0, tq, tr)]

    def tile(kb, carries, masked):
        start = pl.multiple_of(kb * tq, tq)
        lanes = lambda hh: slice(hh * dh, (hh + 1) * dh)
        masks = [(_iota((tr, tq), 1) < _iota((tr, tq), 0) + r0) if masked else None
                 for _, r0 in units]
        zs = [_qk(q_ref[r0:r0 + tr, lanes(hh)], k_ref[pl.ds(start, tq), lanes(hh)])
              + bias_ref[hg * hps + hh] for hh, r0 in units]
        logs = [_sb_logs(z, m) for z, m in zip(zs, masks)]
        sufs = [_sb_suffix(l1m, upper) for _, l1m in logs]
        fins = [_sb_finish(lb, l1m, suf, m, run)
                for (lb, l1m), suf, m, (run, _) in zip(logs, sufs, masks, carries)]
        accs = [acc + jnp.dot(a.astype(BF16), v_ref[pl.ds(start, tq), lanes(hh)],
                              preferred_element_type=F32)
                for (a, _), (_, acc), (hh, _) in zip(fins, carries, units)]
        return tuple((run, acc) for (_, run), acc in zip(fins, accs))

    carries = tuple((jnp.zeros((tr, 1), F32), jnp.zeros((tr, dh), F32)) for _ in units)
    carries = tile(qi, carries, True)
    carries = lax.fori_loop(0, qi, lambda it, c: tile(qi - 1 - it, c, False), carries)
    for (hh, r0), (_, acc) in zip(units, carries):
        o_ref[r0:r0 + tr, hh * dh:(hh + 1) * dh] = acc.astype(o_ref.dtype)


def _attn_prompt(q, k, v, bias, n_seq, t_len, n_heads):
    m, d = q.shape
    dh = d // n_heads
    tq = min(256, t_len)
    hps = 2 if n_heads % 2 == 0 else 1
    qb = t_len // tq
    grid_spec = pltpu.PrefetchScalarGridSpec(
        num_scalar_prefetch=1, grid=(n_seq, n_heads // hps, qb),
        in_specs=[pl.BlockSpec((tq, hps * dh), lambda n, h, i, b: (n * qb + i, h)),
                  pl.BlockSpec((t_len, hps * dh), lambda n, h, i, b: (n, h)),
                  pl.BlockSpec((t_len, hps * dh), lambda n, h, i, b: (n, h))],
        out_specs=pl.BlockSpec((tq, hps * dh), lambda n, h, i, b: (n * qb + i, h)))
    return pl.pallas_call(
        functools.partial(_attn_body, tq=tq, tr=min(128, tq), dh=dh, hps=hps),
        grid_spec=grid_spec,
        out_shape=jax.ShapeDtypeStruct((m, d), BF16),
        compiler_params=_cparams(3), name="sb_attn_prompt")(bias, q, k, v)


def _decode_body(pt_ref, q_ref, bias_ref, e_ref, hm_ref, *refs, n_heads, ppt):
    kn_refs, vn_refs = refs[:ppt], refs[ppt:2 * ppt]
    kc_refs, vc_refs = refs[2 * ppt:3 * ppt], refs[3 * ppt:4 * ppt]
    o_ref, run_ref, acc_ref = refs[4 * ppt:]
    step = pl.program_id(1)
    page = vn_refs[0].shape[0] // n_heads
    rows = q_ref.shape[0]
    tk = ppt * page

    def tile(k_refs, v_refs, mask, run, acc):
        pages = []
        for slot in range(ppt):
            heads = [k_refs[slot][pl.ds(h, page, stride=n_heads), :].astype(BF16)
                     for h in range(n_heads)]
            pages.append(jnp.concatenate(heads, axis=1))
        k_wide = pages[0] if ppt == 1 else jnp.concatenate(pages, axis=0)
        z = _qk(q_ref[...], k_wide) + bias_ref[...]
        a, run = _sb_weights(z, mask, _later_key_matrix(tk), run)
        for slot in range(ppt):
            a_p = a[:, slot * page:(slot + 1) * page].astype(BF16)
            a_rows = jnp.dot(a_p, e_ref[...], preferred_element_type=F32).astype(BF16) * hm_ref[...]
            acc = acc + jnp.dot(a_rows, v_refs[slot][...].astype(BF16),
                                preferred_element_type=F32)
        run_ref[...] = jnp.broadcast_to(run, run_ref.shape)
        acc_ref[...] = acc

    @pl.when(step == 0)
    def _():
        mask = _iota((rows, tk), 1) < _div(_iota((rows, tk), 0), n_heads)
        tile(kn_refs, vn_refs, mask, jnp.zeros((rows, 1), F32), jnp.zeros(acc_ref.shape, F32))

    @pl.when(step > 0)
    def _():
        tile(kc_refs, vc_refs, None, run_ref[:, 0:1], acc_ref[...])

    @pl.when(step == pl.num_programs(1) - 1)
    def _():
        o_ref[...] = acc_ref[...]


def _attn_decode(q, k_new, v_new, bias, cache_k, cache_v, layer, page_table, n_seq, t_len,
                 n_heads):
    d = q.shape[1]
    dh = d // n_heads
    n_layers, n_pool, page = cache_k.shape[:3]
    n_pages = page_table.shape[1]
    ppt = 2 if n_pages % 2 == 0 else 1
    n_tiles = n_pages // ppt
    rows = t_len * n_heads
    assert n_heads % SUBLANES == 0 and t_len <= page
    head_of_lane = jnp.arange(d) // dh
    q_bd = jnp.where(head_of_lane[None, :] == jnp.arange(n_heads)[:, None],
                     q.reshape(n_seq, t_len, 1, d), 0).astype(BF16).reshape(n_seq, rows, d)
    bias_rows = jnp.tile(bias, t_len).reshape(rows, 1)
    key_of_row = jnp.arange(page * n_heads) // n_heads
    expand = (key_of_row[None, :] == jnp.arange(page)[:, None]).astype(BF16)
    head_of_row = jnp.arange(page * n_heads) % n_heads
    head_mask = (head_of_row[None, :] == (jnp.arange(rows) % n_heads)[:, None]).astype(BF16)
    pad_t = ((0, 0), (0, ppt * page - t_len), (0, 0), (0, 0))
    as_rows = lambda x: jnp.pad(x.reshape(n_seq, t_len, n_heads, dh), pad_t).reshape(
        n_seq, ppt, page * n_heads, dh)
    kn, vn = as_rows(k_new), as_rows(v_new)
    kc = cache_k.reshape(n_layers, n_pool, page * n_heads, dh)
    vc = cache_v.reshape(n_layers, n_pool, page * n_heads, dh)

    def phys(n, s, pt, slot):
        return pt[n, (n_tiles - jnp.maximum(s, 1)) * ppt + slot]

    const = lambda shape: pl.BlockSpec(shape, lambda n, s, pt: (0,) * len(shape))
    in_specs = [pl.BlockSpec((None, rows, d), lambda n, s, pt: (n, 0, 0)),
                const((rows, 1)), const(expand.shape), const(head_mask.shape)]
    args = [q_bd, bias_rows, expand, head_mask]
    blk = (None, None, page * n_heads, dh)
    for arr in (kn, vn):
        for slot in range(ppt):
            in_specs.append(pl.BlockSpec(blk, lambda n, s, pt, slot=slot: (n, slot, 0, 0)))
            args.append(arr)
    for arr in (kc, vc):
        for slot in range(ppt):
            in_specs.append(pl.BlockSpec(
                blk, lambda n, s, pt, slot=slot: (layer, phys(n, s, pt, slot), 0, 0)))
            args.append(arr)
    grid_spec = pltpu.PrefetchScalarGridSpec(
        num_scalar_prefetch=1, grid=(n_seq, n_tiles + 1), in_specs=in_specs,
        out_specs=pl.BlockSpec((None, rows, dh), lambda n, s, pt: (n, 0, 0)),
        scratch_shapes=[pltpu.VMEM((rows, LANES), F32), pltpu.VMEM((rows, dh), F32)])
    o = pl.pallas_call(
        functools.partial(_decode_body, n_heads=n_heads, ppt=ppt), grid_spec=grid_spec,
        out_shape=jax.ShapeDtypeStruct((n_seq, rows, dh), F32),
        compiler_params=_cparams(2), name="sb_attn_decode")(page_table, *args)
    return o.reshape(n_seq * t_len, d).astype(BF16)


def _attn_layer(y, gain, n_seq, t_len, n_heads, w_qkv, q_gain, k_gain, sb_bias, w_o, layer,
                cache=None):
    m, d = y.shape
    dh = d // n_heads
    scale = dh ** -0.5
    qn = lambda accs, t, c: [_head_norm(accs[0], c[1][...], scale)]
    kn = lambda accs, t, c: [_head_norm(accs[0], c[1][...], 1.0)] * 2
    vn = lambda accs, t, c: [accs[0]] * 2
    (q,) = _norm_proj("attn_q", y, gain, w_qkv, layer, 0, d, [BF16], qn, [q_gain.reshape(1, dh)])
    k32, kb = _norm_proj("attn_k", y, gain, w_qkv, layer, d, d, [F32, BF16], kn,
                         [k_gain.reshape(1, dh)])
    v32, vb = _norm_proj("attn_v", y, gain, w_qkv, layer, 2 * d, d, [F32, BF16], vn)
    if cache is None:
        o = _attn_prompt(q, kb, vb, sb_bias, n_seq, t_len, n_heads)
    else:
        cache_k, cache_v, page_table = cache
        o = _attn_decode(q, k32, v32, sb_bias, cache_k, cache_v, layer, page_table, n_seq, t_len,
                         n_heads)
    tm, tn = _tiles(m, d)
    blk, imap = _row_block(tm, d)
    (out,) = _fused_matmul("attn_out", m, d, d, tm, tn, [(o, blk, imap)], [],
                           [(w_o, layer, 0)], [y], [F32], _ident_prologue, _resid_epilogue)
    shape = (n_seq, t_len, n_heads, dh)
    return out, k32.reshape(shape), v32.reshape(shape)


def kernel(x_prompt, x_sample, state_ssm_re, state_ssm_im, state_conv, cache_k, cache_v, page_table, norm_mix, norm_mlp, ssm_lambda_re, ssm_lambda_im, ssm_log_dt, ssm_b_re, ssm_b_im, ssm_c_re, ssm_c_im, ssm_d, ssm_w_glu, conv_w_in, conv_w_dw, conv_w_out, attn_w_qkv, attn_q_norm, attn_k_norm, attn_sb_bias, attn_w_o, mlp_w_up, mlp_w_down):
    n_p, t_p, d = x_prompt.shape
    n_s, t_s, _ = x_sample.shape
    depth = norm_mix.shape[0]
    n_heads = attn_sb_bias.shape[1]
    assert conv_w_dw.shape[1] == 3 and d // n_heads == LANES
    chunk_p = min(S5_CHUNK, t_p)
    assert t_p % chunk_p == 0 and (t_p // chunk_p) & (t_p // chunk_p - 1) == 0

    w_glu, w_in, w_out = (w.astype(BF16) for w in (ssm_w_glu, conv_w_in, conv_w_out))
    w_qkv, w_o = attn_w_qkv.astype(BF16), attn_w_o.astype(BF16)
    w_up, w_down = mlp_w_up, mlp_w_down

    y_p = x_prompt.reshape(n_p * t_p, d)
    y_s = x_sample.reshape(n_s * t_s, d)
    outs = {k: [] for k in ("re_p", "im_p", "re_s", "im_s", "cv_p", "cv_s", "k_p", "v_p", "k_s", "v_s")}
    for i in range(depth):
        kind, j = i % 3, i // 3
        if kind == 0:
            ssm = (ssm_lambda_re[j], ssm_lambda_im[j], ssm_log_dt[j], ssm_b_re[j], ssm_b_im[j],
                   ssm_c_re[j], ssm_c_im[j], ssm_d[j])
            tabs_p = _s5_tables(*ssm, chunk_p, t_p // chunk_p, True)
            tabs_s = _s5_tables(*ssm, t_s, 1, False)
            h0 = jnp.concatenate([state_ssm_re[j], state_ssm_im[j]], axis=-1).transpose(1, 0, 2)
            y_p, re_p, im_p = _s5_layer(y_p, norm_mix[i], n_p, t_p, tabs_p, None, w_glu, j, chunk_p)
            y_s, re_s, im_s = _s5_layer(y_s, norm_mix[i], n_s, t_s, tabs_s, h0, w_glu, j, t_s)
            outs["re_p"].append(re_p); outs["im_p"].append(im_p)
            outs["re_s"].append(re_s); outs["im_s"].append(im_s)
        elif kind == 1:
            y_p, cv_p = _conv_layer(y_p, norm_mix[i], n_p, t_p, None, w_in, conv_w_dw, w_out, j)
            y_s, cv_s = _conv_layer(y_s, norm_mix[i], n_s, t_s, state_conv[j], w_in, conv_w_dw,
                                    w_out, j)
            outs["cv_p"].append(cv_p); outs["cv_s"].append(cv_s)
        else:
            attn = (w_qkv, attn_q_norm[j], attn_k_norm[j], attn_sb_bias[j], w_o, j)
            y_p, k_p, v_p = _attn_layer(y_p, norm_mix[i], n_p, t_p, n_heads, *attn)
            y_s, k_s, v_s = _attn_layer(y_s, norm_mix[i], n_s, t_s, n_heads, *attn,
                                        cache=(cache_k, cache_v, page_table))
            outs["k_p"].append(k_p); outs["v_p"].append(v_p)
            outs["k_s"].append(k_s); outs["v_s"].append(v_s)
        y_p, y_s = _mlp(y_p, y_s, norm_mlp[i], w_up, w_down, i)
    st = lambda key: jnp.stack(outs[key])
    return (y_p.reshape(n_p, t_p, d), y_s.reshape(n_s, t_s, d),
            st("re_p"), st("im_p"), st("re_s"), st("im_s"), st("cv_p"), st("cv_s"),
            st("k_p"), st("v_p"), st("k_s"), st("v_s"))
```

```python
import functools
import math

import jax
import jax.numpy as jnp
from jax import lax
from jax.experimental import pallas as pl
from jax.experimental.pallas import tpu as pltpu

F32 = jnp.float32
BF16 = jnp.bfloat16
EPS = 1e-6
LANES = 128
SUBLANES = 8
VMEM_LIMIT_BYTES = 56 * 1024 * 1024
VMEM_BUDGET_BYTES = 44 * 1024 * 1024
S5_CHUNK = 16
HIGHEST = lax.Precision.HIGHEST


def _cparams(n_axes):
    return pltpu.CompilerParams(dimension_semantics=("arbitrary",) * n_axes,
                                vmem_limit_bytes=VMEM_LIMIT_BYTES)


def _rms(x, g):
    ms = jnp.mean(x * x, axis=-1, keepdims=True)
    return x * lax.rsqrt(ms + EPS) * g


def _iota(shape, dim):
    return lax.broadcasted_iota(jnp.int32, shape, dim)


def _mod(x, n):
    return x & (n - 1) if n & (n - 1) == 0 else lax.rem(x, n)


def _div(x, n):
    return x >> (n.bit_length() - 1) if n & (n - 1) == 0 else lax.div(x, n)


def _plan_tiles(m, kdim, n_out, row_bytes_per_row, n_w, tile_itemsizes):
    for tm, tn, n_buf in ((1024, 1024, 2), (1024, 512, 2), (1024, 1024, 1), (1024, 512, 1),
                          (512, 1024, 2), (512, 512, 2), (512, 512, 1)):
        tm, tn = min(tm, m), min(tn, n_out)
        stream = n_w * kdim * tn * 2 + tm * tn * sum(tile_itemsizes)
        fixed = tm * kdim * 2 + 2 * n_w * tm * tn * 4
        if n_buf * tm * row_bytes_per_row + 2 * stream + fixed <= VMEM_BUDGET_BYTES:
            break
    return tm, tn, n_buf


def _fused_matmul(name, m, kdim, n_out, row_in, const_in, weights, tile_in, out_dtypes,
                  prologue, epilogue):
    n_row, n_const, n_w, n_tile, n_o = (len(row_in), len(const_in), len(weights),
                                        len(tile_in), len(out_dtypes))
    size = lambda dt: jnp.dtype(dt).itemsize
    tm, tn, n_buf = _plan_tiles(
        m, kdim, n_out, sum(a.shape[1] * size(a.dtype) for a, kind in row_in if kind == "rows"),
        n_w, [size(a.dtype) for a in tile_in] + [size(dt) for dt in out_dtypes])
    row_mode = pl.Buffered(1) if n_buf == 1 else None
    per = tm // SUBLANES
    in_specs, args = [], []
    for arr, kind in row_in:
        if kind == "rows":
            spec = pl.BlockSpec((tm, arr.shape[1]), lambda i, j: (i, 0), pipeline_mode=row_mode)
        else:
            spec = pl.BlockSpec((SUBLANES, arr.shape[1]),
                                lambda i, j: (jnp.maximum(i * per - 1, 0), 0))
        in_specs.append(spec)
        args.append(arr)
    for arr in const_in:
        in_specs.append(pl.BlockSpec(arr.shape, lambda i, j, nd=arr.ndim: (0,) * nd))
        args.append(arr)
    for arr, layer, col in weights:
        in_specs.append(pl.BlockSpec((None, kdim, tn),
                                     lambda i, j, layer=layer, off=col // tn: (layer, 0, j + off)))
        args.append(arr)
    for arr in tile_in:
        in_specs.append(pl.BlockSpec((tm, tn), lambda i, j: (i, j)))
        args.append(arr)
    out_shape = [jax.ShapeDtypeStruct((m, n_out), dt) for dt in out_dtypes]
    out_specs = [pl.BlockSpec((tm, tn), lambda i, j: (i, j)) for _ in out_dtypes]

    def body(*refs):
        p = 0
        row_refs = refs[p:p + n_row]; p += n_row
        const_refs = refs[p:p + n_const]; p += n_const
        w_refs = refs[p:p + n_w]; p += n_w
        tile_refs = refs[p:p + n_tile]; p += n_tile
        out_refs = refs[p:p + n_o]; p += n_o
        lhs_ref = refs[p]

        @pl.when(pl.program_id(1) == 0)
        def _():
            lhs_ref[...] = prologue(row_refs, const_refs, tm).astype(BF16)

        lhs = lhs_ref[...]
        accs = [jnp.dot(lhs, w[...], preferred_element_type=F32) for w in w_refs]
        for o_ref, val in zip(out_refs, epilogue(accs, tile_refs, const_refs)):
            o_ref[...] = val.astype(o_ref.dtype)

    return pl.pallas_call(
        body, grid=(m // tm, n_out // tn), in_specs=in_specs, out_specs=out_specs,
        out_shape=out_shape, scratch_shapes=[pltpu.VMEM((tm, kdim), BF16)],
        compiler_params=_cparams(2), name=name)(*args)


def _norm_prologue(row_refs, const_refs, tm):
    return _rms(row_refs[0][...], const_refs[0][...])


def _ident_prologue(row_refs, const_refs, tm):
    return row_refs[0][...]


def _head_norm(acc, gain, scale):
    segs = []
    for h in range(acc.shape[1] // LANES):
        seg = acc[:, h * LANES:(h + 1) * LANES]
        ms = jnp.mean(seg * seg, axis=-1, keepdims=True)
        y = seg * lax.rsqrt(ms + EPS) * gain
        segs.append(y * scale if scale != 1.0 else y)
    return segs[0] if len(segs) == 1 else jnp.concatenate(segs, axis=-1)


def _norm_proj(name, x, gain, w, layer, col, n_out, out_dtypes, epilogue, extra_const=()):
    m, kdim = x.shape
    return _fused_matmul(name, m, kdim, n_out, [(x, "rows")],
                         [gain.reshape(1, kdim)] + list(extra_const), [(w, layer, col)], [],
                         out_dtypes, _norm_prologue, epilogue)


def _mlp_body(x_ref, xs_ref, g_ref, wu_ref, wd_ref, o_ref, os_ref, xn_ref, acc_ref, xsn_ref,
              accs_ref):
    i, f = pl.program_id(0), pl.program_id(1)
    last = pl.num_programs(1) - 1
    wu = wu_ref[...]
    wd = wd_ref[...]

    def step(src_ref, dst_ref, n_ref, a_ref):
        @pl.when(f == 0)
        def _():
            n_ref[...] = _rms(src_ref[...], g_ref[...]).astype(BF16)
            a_ref[...] = jnp.zeros_like(a_ref)

        h = jnp.maximum(jnp.dot(n_ref[...], wu, preferred_element_type=F32), 0.0)
        a_ref[...] += jnp.dot((h * h).astype(BF16), wd, preferred_element_type=F32)

        @pl.when(f == last)
        def _():
            dst_ref[...] = src_ref[...] + a_ref[...]

    step(x_ref, o_ref, xn_ref, acc_ref)
    pl.when(i == 0)(lambda: step(xs_ref, os_ref, xsn_ref, accs_ref))


def _mlp(x, xs, gain, w_up, w_down, layer):
    m, d = x.shape
    ms = xs.shape[0]
    ff = w_up.shape[2]
    tm, tf = min(1024, m), min(512, ff)
    once = pl.Buffered(1)
    return pl.pallas_call(
        _mlp_body, grid=(m // tm, ff // tf),
        in_specs=[pl.BlockSpec((tm, d), lambda i, f: (i, 0), pipeline_mode=once),
                  pl.BlockSpec((ms, d), lambda i, f: (0, 0)),
                  pl.BlockSpec((1, d), lambda i, f: (0, 0)),
                  pl.BlockSpec((None, d, tf), lambda i, f: (layer, 0, f)),
                  pl.BlockSpec((None, tf, d), lambda i, f: (layer, f, 0))],
        out_specs=[pl.BlockSpec((tm, d), lambda i, f: (i, 0), pipeline_mode=once),
                   pl.BlockSpec((ms, d), lambda i, f: (0, 0))],
        out_shape=[jax.ShapeDtypeStruct((m, d), F32), jax.ShapeDtypeStruct((ms, d), F32)],
        scratch_shapes=[pltpu.VMEM((tm, d), BF16), pltpu.VMEM((tm, d), F32),
                        pltpu.VMEM((ms, d), BF16), pltpu.VMEM((ms, d), F32)],
        compiler_params=_cparams(2), name="mlp")(x, xs, gain.reshape(1, d), w_up, w_down)


def _norm_body(x_ref, g_ref, o_ref):
    o_ref[...] = _rms(x_ref[...], g_ref[...])


def _norm(x, gain):
    m, d = x.shape
    tm = min(512, m)
    return pl.pallas_call(
        _norm_body, grid=(m // tm,),
        in_specs=[pl.BlockSpec((tm, d), lambda i: (i, 0)), pl.BlockSpec((1, d), lambda i: (0, 0))],
        out_specs=pl.BlockSpec((tm, d), lambda i: (i, 0)),
        out_shape=jax.ShapeDtypeStruct((m, d), F32),
        compiler_params=_cparams(1), name="rmsnorm")(x, gain.reshape(1, d))


def _complex_scale(acos, asin, h):
    return acos * h + asin * pltpu.roll(h, h.shape[1] // 2, axis=1)


def _gelu(y):
    return y * (0.5 * (1.0 + jnp.tanh(math.sqrt(2.0 / math.pi) * (y + 0.044715 * (y * y * y)))))


def _s5_step_body(u_ref, m_ref, bc_ref, cp_ref, d_ref, acos_ref, asin_ref, h0_ref, z_ref, hl_ref,
                  *, gt):
    def one_group(g, carry):
        u = u_ref[g]
        ub = u.astype(BF16)
        h_prev = h0_ref[g]
        hl_ref[g] = (jnp.dot(ub, bc_ref[g], preferred_element_type=F32)
                     + _complex_scale(acos_ref[g][0:1], asin_ref[g][0:1], h_prev))
        y = (jnp.dot(ub, m_ref[g], preferred_element_type=F32)
             + jnp.dot(h_prev.astype(BF16), cp_ref[g], preferred_element_type=F32)
             + u * d_ref[g])
        z_ref[g] = _gelu(y).astype(z_ref.dtype)
        return carry

    lax.fori_loop(0, gt, one_group, 0)


def _s5_seq_body(x_ref, bd_ref, bct_ref, cpt_ref, acos_ref, asin_ref, z_ref, hl_ref,
                 m8_ref, bc8_ref, cp8_ref, h_ref, *, chunk, n_chunk, n_seq, c):
    gl = LANES // c
    rows = n_seq * n_chunk
    blk = lambda i: slice(i * LANES, (i + 1) * LANES)
    for s in range(chunk):
        for t in range(s, chunk):
            m8_ref[blk(s), blk(t)] = bd_ref[t - s]
        if s % 2:
            m8_ref[blk(s), blk(s - 1)] = jnp.zeros((LANES, LANES), BF16)
    row_grp = _div(_iota((LANES, LANES), 0), c)
    lane_grp = _div(_iota((LANES, LANES), 1), c)
    for s in range(chunk):
        for g in range(gl):
            bc8_ref[blk(s), blk(g)] = jnp.where(row_grp == g, bct_ref[s], 0.0).astype(BF16)
            cp8_ref[blk(g), blk(s)] = jnp.where(lane_grp == g, cpt_ref[s], 0.0).astype(BF16)

    u2 = jnp.concatenate([x_ref[pl.ds(s, rows, stride=chunk), :].astype(BF16)
                          for s in range(chunk)], axis=1)
    h = jnp.dot(u2, bc8_ref[...], preferred_element_type=F32)
    kidx = _iota((rows, 1), 0) & (n_chunk - 1)
    swap = lambda v: jnp.concatenate(
        [pltpu.roll(v[:, blk(g)], LANES // 2, axis=1) for g in range(gl)], axis=1)
    shift, si = 1, 0
    while shift < n_chunk:
        sh = jnp.where(kidx >= shift, pltpu.roll(h, shift, axis=0), 0.0)
        h = h + acos_ref[si:si + 1, :] * sh + asin_ref[si:si + 1, :] * swap(sh)
        shift, si = shift * 2, si + 1
    h_ref[...] = h
    h_prev = jnp.where(kidx >= 1, pltpu.roll(h, 1, axis=0), 0.0).astype(BF16)
    for t in range(0, chunk, 2):
        cols = slice(t * LANES, (t + 2) * LANES)
        y = (jnp.dot(u2[:, :(t + 2) * LANES], m8_ref[:(t + 2) * LANES, cols],
                     preferred_element_type=F32)
             + jnp.dot(h_prev, cp8_ref[:, cols], preferred_element_type=F32))
        z = _gelu(y)
        z_ref[pl.ds(t, rows, stride=chunk), :] = z[:, :LANES]
        z_ref[pl.ds(t + 1, rows, stride=chunk), :] = z[:, LANES:]
    for n in range(n_seq):
        hl_ref[pl.ds(n, 1), :] = h_ref[pl.ds((n + 1) * n_chunk - 1, 1), :]


def _s5_tables(lam_re, lam_im, log_dt, b_re, b_im, c_re, c_im, d, chunk, n_chunk, per_tile):
    g, p = lam_re.shape
    c = b_re.shape[2]
    ldt = lax.complex(lam_re, lam_im) * jnp.exp(log_dt)[:, None]
    a_bar = jnp.exp(ldt)
    bbar = ((a_bar - 1.0) / lax.complex(lam_re, lam_im))[..., None] * lax.complex(b_re, b_im)
    taus = jnp.arange(chunk + 1, dtype=F32)
    apow = jnp.exp(ldt[:, None, :] * taus[None, :, None])
    cc = lax.complex(c_re, c_im)
    w1 = cc[:, None, :, :] * apow[:, :chunk, None, :]
    lhs = jnp.concatenate([jnp.real(w1), -jnp.imag(w1)], axis=-1)
    rhs = jnp.concatenate([jnp.real(bbar), jnp.imag(bbar)], axis=1)
    kt = jnp.einsum("gtoq,gqi->gtoi", lhs, rhs, precision=HIGHEST)
    rev = apow[:, :chunk][:, ::-1]
    bcx = (rev[:, :, None, :] * bbar.transpose(0, 2, 1)[:, None, :, :]).reshape(g, chunk * c, p)
    bc = jnp.concatenate([jnp.real(bcx), jnp.imag(bcx)], axis=-1)
    ct = (cc[:, None, :, :] * apow[:, 1:, None, :]).transpose(0, 3, 1, 2).reshape(g, p, chunk * c)
    cp = jnp.concatenate([jnp.real(ct), -jnp.imag(ct)], axis=1)
    shifts = [1]
    while shifts[-1] * 2 < n_chunk:
        shifts.append(shifts[-1] * 2)
    while len(shifts) < SUBLANES:
        shifts.append(shifts[-1])
    apl = jnp.exp(ldt[:, None, :] * (chunk * jnp.asarray(shifts, F32))[None, :, None])
    acos = jnp.concatenate([jnp.real(apl), jnp.real(apl)], axis=-1)
    asin = jnp.concatenate([-jnp.imag(apl), jnp.imag(apl)], axis=-1)
    if not per_tile:
        kpad = jnp.concatenate([kt, jnp.zeros((g, 1, c, c), F32)], axis=1)
        s_idx = jnp.arange(chunk)[:, None]
        t_idx = jnp.arange(chunk)[None, :]
        tau = jnp.where(t_idx >= s_idx, t_idx - s_idx, chunk)
        mmat = kpad[:, tau].transpose(0, 1, 4, 2, 3).reshape(g, chunk * c, chunk * c)
        dt = jnp.tile(d.reshape(g, 1, c), (1, chunk, 1)).reshape(g, 1, chunk * c)
        return mmat.astype(BF16), bc.astype(BF16), cp.astype(BF16), dt, acos, asin
    gl = LANES // c
    nt = g // gl
    p2 = 2 * p
    kd = kt.at[:, 0].add(d.reshape(g, c)[:, :, None] * jnp.eye(c, dtype=F32))
    ktr = kd.reshape(nt, gl, chunk, c, c).transpose(0, 2, 1, 4, 3)
    bd = ktr[:, :, :, :, None, :] * jnp.eye(gl, dtype=F32)[None, None, :, None, :, None]
    bd = bd.reshape(nt, chunk, LANES, LANES).astype(BF16)
    bct = bc.reshape(nt, gl, chunk, c, p2).transpose(0, 2, 1, 3, 4).reshape(nt, chunk, LANES, p2)
    cpt = cp.reshape(nt, gl, p2, chunk, c).transpose(0, 3, 2, 1, 4).reshape(nt, chunk, p2, LANES)
    tile = lambda a: a.reshape(nt, gl, SUBLANES, p2).transpose(0, 2, 1, 3).reshape(
        nt, SUBLANES, gl * p2)
    return bd, bct, cpt, tile(acos), tile(asin)


def _s5_step(hn, n_seq, t_len, tabs, h0):
    mmat, bc, cp, dt, acos, asin = tabs
    g, lc = mmat.shape[:2]
    p2 = bc.shape[2]
    c = lc // t_len
    u_r = hn.reshape(n_seq, t_len, g, c).transpose(2, 0, 1, 3).reshape(g, n_seq, lc)
    gt = min(8, g)
    grp = lambda shape: pl.BlockSpec((gt,) + shape, lambda i: (i, 0, 0))
    z_r, h_last = pl.pallas_call(
        functools.partial(_s5_step_body, gt=gt), grid=(g // gt,),
        in_specs=[grp((n_seq, lc)), grp((lc, lc)), grp((lc, p2)), grp((p2, lc)), grp((1, lc)),
                  grp((SUBLANES, p2)), grp((SUBLANES, p2)), grp((n_seq, p2))],
        out_specs=[grp((n_seq, lc)), grp((n_seq, p2))],
        out_shape=[jax.ShapeDtypeStruct((g, n_seq, lc), BF16),
                   jax.ShapeDtypeStruct((g, n_seq, p2), F32)],
        compiler_params=_cparams(1), name="s5_step")(u_r, mmat, bc, cp, dt, acos, asin, h0)
    z = z_r.reshape(g, n_seq, t_len, c).transpose(1, 2, 0, 3).reshape(n_seq * t_len, g * c)
    return z, h_last.transpose(1, 0, 2)


def _s5_seq(hn, n_seq, t_len, tabs, chunk):
    bd, bct, cpt, acos, asin = tabs
    m, d = hn.shape
    nt = bd.shape[0]
    p2 = bct.shape[3]
    gl = acos.shape[2] // p2
    c = LANES // gl
    n_chunk = t_len // chunk
    assert p2 == LANES and d == nt * LANES
    assert chunk % 2 == 0
    tab = lambda a: pl.BlockSpec((None,) + a.shape[1:], lambda j: (j,) + (0,) * (a.ndim - 1))
    z, h_last = pl.pallas_call(
        functools.partial(_s5_seq_body, chunk=chunk, n_chunk=n_chunk, n_seq=n_seq, c=c),
        grid=(nt,),
        in_specs=[pl.BlockSpec((m, LANES), lambda j: (0, j)),
                  tab(bd), tab(bct), tab(cpt), tab(acos), tab(asin)],
        out_specs=[pl.BlockSpec((m, LANES), lambda j: (0, j)),
                   pl.BlockSpec((None, n_seq, gl * p2), lambda j: (j, 0, 0))],
        out_shape=[jax.ShapeDtypeStruct((m, d), F32),
                   jax.ShapeDtypeStruct((nt, n_seq, gl * p2), F32)],
        scratch_shapes=[pltpu.VMEM((chunk * LANES, chunk * LANES), BF16),
                        pltpu.VMEM((chunk * LANES, gl * p2), BF16),
                        pltpu.VMEM((gl * p2, chunk * LANES), BF16),
                        pltpu.VMEM((n_seq * n_chunk, gl * p2), F32)],
        compiler_params=_cparams(1), name="s5_seq")(hn, bd, bct, cpt, acos, asin)
    h_last = h_last.reshape(nt, n_seq, gl, p2).transpose(1, 0, 2, 3).reshape(n_seq, nt * gl, p2)
    return z, h_last


def _glu_epilogue(accs, tile_refs, const_refs):
    a, gate = accs
    return [tile_refs[0][...] + a * jax.nn.sigmoid(gate)]


def _s5_layer(y, gain, n_seq, t_len, tabs, h0, w_glu, layer, chunk):
    m, d = y.shape
    if h0 is None:
        z, h_last = _s5_seq(_norm(y, gain), n_seq, t_len, tabs, chunk)
    else:
        z, h_last = _s5_step(_norm(y, gain), n_seq, t_len, tabs, h0)
    (out,) = _fused_matmul("s5_glu", m, d, d, [(z, "rows")], [],
                           [(w_glu, layer, 0), (w_glu, layer, d)], [y], [F32],
                           _ident_prologue, _glu_epilogue)
    p = h_last.shape[2] // 2
    return out, h_last[..., :p], h_last[..., p:]


def _conv_in_epilogue(accs, tile_refs, const_refs):
    gate_b, gate_c, v = accs
    return [gate_b, gate_c * v]


def _conv_prologue(row_refs, const_refs, tm, *, t_len, has_buf):
    w = const_refs[0][...]
    bg, cv = row_refs[0][...], row_refs[1][...]
    t = _mod(pl.program_id(0) * tm + _iota((tm, 1), 0), t_len)
    if has_buf:
        b0, b1 = row_refs[2][...], row_refs[3][...]
        r1 = jnp.where(t >= 1, pltpu.roll(cv, 1, axis=0), b1)
        r2 = jnp.where(t >= 2, pltpu.roll(cv, 2, axis=0), jnp.where(t == 1, b1, b0))
    else:
        full = jnp.concatenate([row_refs[2][...], cv], axis=0)
        r1 = jnp.where(t >= 1, pltpu.roll(full, 1, axis=0)[SUBLANES:], 0.0)
        r2 = jnp.where(t >= 2, pltpu.roll(full, 2, axis=0)[SUBLANES:], 0.0)
    return bg * (w[2:3] * cv + w[1:2] * r1 + w[0:1] * r2)


def _resid_epilogue(accs, tile_refs, const_refs):
    return [tile_refs[0][...] + accs[0]]


def _conv_layer(y, gain, n_seq, t_len, buf, w_in, w_dw, w_out, layer):
    m, d = y.shape
    assert t_len >= 2
    bg, cv = _fused_matmul("conv_in", m, d, d, [(y, "rows")], [gain.reshape(1, d)],
                           [(w_in, layer, 0), (w_in, layer, d), (w_in, layer, 2 * d)],
                           [], [F32, F32], _norm_prologue, _conv_in_epilogue)
    rows = [(bg, "rows"), (cv, "rows")]
    if buf is None:
        rows.append((cv, "prev"))
    else:
        assert m <= 512
        rows += [(jnp.repeat(buf[:, k], t_len, axis=0), "rows") for k in range(2)]
    (out,) = _fused_matmul(
        "conv_out", m, d, d, rows, [w_dw[layer]], [(w_out, layer, 0)], [y], [F32],
        functools.partial(_conv_prologue, t_len=t_len, has_buf=buf is not None),
        _resid_epilogue)
    return out, cv.reshape(n_seq, t_len, d)[:, t_len - 2:]


def _sb_logs(z, mask):
    sp = jnp.log(1.0 + jnp.exp(-jnp.abs(z)))
    log_b = jnp.minimum(z, 0.0) - sp
    log_1m = log_b - z
    if mask is not None:
        log_1m = jnp.where(mask, log_1m, 0.0)
    return log_b, log_1m


def _sb_suffix(log_1m, upper):
    hi = log_1m.astype(BF16)
    lo = (log_1m - hi.astype(F32)).astype(BF16)
    return (jnp.dot(hi, upper, preferred_element_type=F32)
            + jnp.dot(lo, upper, preferred_element_type=F32))


def _sb_finish(log_b, log_1m, suffix, mask, run):
    a = jnp.exp(log_b + suffix + run)
    if mask is not None:
        a = jnp.where(mask, a, 0.0)
    return a, run + jnp.sum(log_1m, axis=-1, keepdims=True)


def _sb_weights(z, mask, upper, run):
    log_b, log_1m = _sb_logs(z, mask)
    return _sb_finish(log_b, log_1m, _sb_suffix(log_1m, upper), mask, run)


def _later_key_matrix(tk):
    return (_iota((tk, tk), 0) > _iota((tk, tk), 1)).astype(BF16)


def _qk(q, k_blk):
    return lax.dot_general(q, k_blk, (((1,), (1,)), ((), ())), preferred_element_type=F32)


def _attn_body(bias_ref, q_ref, k_ref, v_ref, o_ref, *, tq, tr, dh, hps):
    hg = pl.program_id(1)
    qi = pl.program_id(2)
    upper = _later_key_matrix(tq)
    units = [(hh, r0) for hh in range(hps) for r0 in range(0, tq, tr)]

    def tile(kb, carries, masked):
        start = pl.multiple_of(kb * tq, tq)
        lanes = lambda hh: slice(hh * dh, (hh + 1) * dh)
        masks = [(_iota((tr, tq), 1) < _iota((tr, tq), 0) + r0) if masked else None
                 for _, r0 in units]
        zs = [_qk(q_ref[r0:r0 + tr, lanes(hh)], k_ref[pl.ds(start, tq), lanes(hh)])
              + bias_ref[hg * hps + hh] for hh, r0 in units]
        logs = [_sb_logs(z, m) for z, m in zip(zs, masks)]
        sufs = [_sb_suffix(l1m, upper) for _, l1m in logs]
        fins = [_sb_finish(lb, l1m, suf, m, run)
                for (lb, l1m), suf, m, (run, _) in zip(logs, sufs, masks, carries)]
        accs = [acc + jnp.dot(a.astype(BF16), v_ref[pl.ds(start, tq), lanes(hh)],
                              preferred_element_type=F32)
                for (a, _), (_, acc), (hh, _) in zip(fins, carries, units)]
        return tuple((run, acc) for (_, run), acc in zip(fins, accs))

    carries = tuple((jnp.zeros((tr, 1), F32), jnp.zeros((tr, dh), F32)) for _ in units)
    carries = tile(qi, carries, True)
    carries = lax.fori_loop(0, qi, lambda it, c: tile(qi - 1 - it, c, False), carries)
    for (hh, r0), (_, acc) in zip(units, carries):
        o_ref[r0:r0 + tr, hh * dh:(hh + 1) * dh] = acc.astype(o_ref.dtype)


def _attn_prompt(q, k, v, bias, n_seq, t_len, n_heads):
    m, d = q.shape
    dh = d // n_heads
    tq = min(256, t_len)
    hps = 2 if n_heads % 2 == 0 else 1
    qb = t_len // tq
    grid_spec = pltpu.PrefetchScalarGridSpec(
        num_scalar_prefetch=1, grid=(n_seq, n_heads // hps, qb),
        in_specs=[pl.BlockSpec((tq, hps * dh), lambda n, h, i, b: (n * qb + i, h)),
                  pl.BlockSpec((t_len, hps * dh), lambda n, h, i, b: (n, h)),
                  pl.BlockSpec((t_len, hps * dh), lambda n, h, i, b: (n, h))],
        out_specs=pl.BlockSpec((tq, hps * dh), lambda n, h, i, b: (n * qb + i, h)))
    return pl.pallas_call(
        functools.partial(_attn_body, tq=tq, tr=min(128, tq), dh=dh, hps=hps),
        grid_spec=grid_spec,
        out_shape=jax.ShapeDtypeStruct((m, d), BF16),
        compiler_params=_cparams(3), name="sb_attn_prompt")(bias, q, k, v)


def _decode_body(pt_ref, q_ref, bias_ref, e_ref, hm_ref, *refs, n_heads, ppt):
    kn_refs, vn_refs = refs[:ppt], refs[ppt:2 * ppt]
    kc_refs, vc_refs = refs[2 * ppt:3 * ppt], refs[3 * ppt:4 * ppt]
    o_ref, run_ref, acc_ref = refs[4 * ppt:]
    step = pl.program_id(1)
    page = vn_refs[0].shape[0] // n_heads
    rows = q_ref.shape[0]
    tk = ppt * page

    def tile(k_refs, v_refs, mask, run, acc):
        pages = []
        for slot in range(ppt):
            heads = [k_refs[slot][pl.ds(h, page, stride=n_heads), :].astype(BF16)
                     for h in range(n_heads)]
            pages.append(jnp.concatenate(heads, axis=1))
        k_wide = pages[0] if ppt == 1 else jnp.concatenate(pages, axis=0)
        z = _qk(q_ref[...], k_wide) + bias_ref[...]
        a, run = _sb_weights(z, mask, _later_key_matrix(tk), run)
        for slot in range(ppt):
            a_p = a[:, slot * page:(slot + 1) * page].astype(BF16)
            a_rows = jnp.dot(a_p, e_ref[...], preferred_element_type=F32).astype(BF16) * hm_ref[...]
            acc = acc + jnp.dot(a_rows, v_refs[slot][...].astype(BF16),
                                preferred_element_type=F32)
        run_ref[...] = jnp.broadcast_to(run, run_ref.shape)
        acc_ref[...] = acc

    @pl.when(step == 0)
    def _():
        mask = _iota((rows, tk), 1) < _div(_iota((rows, tk), 0), n_heads)
        tile(kn_refs, vn_refs, mask, jnp.zeros((rows, 1), F32), jnp.zeros(acc_ref.shape, F32))

    @pl.when(step > 0)
    def _():
        tile(kc_refs, vc_refs, None, run_ref[:, 0:1], acc_ref[...])

    @pl.when(step == pl.num_programs(1) - 1)
    def _():
        o_ref[...] = acc_ref[...]


def _attn_decode(q, k_new, v_new, bias, cache_k, cache_v, layer, page_table, n_seq, t_len,
                 n_heads):
    d = q.shape[1]
    dh = d // n_heads
    n_layers, n_pool, page = cache_k.shape[:3]
    n_pages = page_table.shape[1]
    ppt = 2 if n_pages % 2 == 0 else 1
    n_tiles = n_pages // ppt
    rows = t_len * n_heads
    assert n_heads % SUBLANES == 0 and t_len <= page
    head_of_lane = jnp.arange(d) // dh
    q_bd = jnp.where(head_of_lane[None, :] == jnp.arange(n_heads)[:, None],
                     q.reshape(n_seq, t_len, 1, d), 0).astype(BF16).reshape(n_seq, rows, d)
    bias_rows = jnp.tile(bias, t_len).reshape(rows, 1)
    key_of_row = jnp.arange(page * n_heads) // n_heads
    expand = (key_of_row[None, :] == jnp.arange(page)[:, None]).astype(BF16)
    head_of_row = jnp.arange(page * n_heads) % n_heads
    head_mask = (head_of_row[None, :] == (jnp.arange(rows) % n_heads)[:, None]).astype(BF16)
    pad_t = ((0, 0), (0, ppt * page - t_len), (0, 0), (0, 0))
    as_rows = lambda x: jnp.pad(x.reshape(n_seq, t_len, n_heads, dh), pad_t).reshape(
        n_seq, ppt, page * n_heads, dh)
    kn, vn = as_rows(k_new), as_rows(v_new)
    kc = cache_k.reshape(n_layers, n_pool, page * n_heads, dh)
    vc = cache_v.reshape(n_layers, n_pool, page * n_heads, dh)

    def phys(n, s, pt, slot):
        return pt[n, (n_tiles - jnp.maximum(s, 1)) * ppt + slot]

    const = lambda shape: pl.BlockSpec(shape, lambda n, s, pt: (0,) * len(shape))
    in_specs = [pl.BlockSpec((None, rows, d), lambda n, s, pt: (n, 0, 0)),
                const((rows, 1)), const(expand.shape), const(head_mask.shape)]
    args = [q_bd, bias_rows, expand, head_mask]
    blk = (None, None, page * n_heads, dh)
    for arr in (kn, vn):
        for slot in range(ppt):
            in_specs.append(pl.BlockSpec(blk, lambda n, s, pt, slot=slot: (n, slot, 0, 0)))
            args.append(arr)
    for arr in (kc, vc):
        for slot in range(ppt):
            in_specs.append(pl.BlockSpec(
                blk, lambda n, s, pt, slot=slot: (layer, phys(n, s, pt, slot), 0, 0)))
            args.append(arr)
    grid_spec = pltpu.PrefetchScalarGridSpec(
        num_scalar_prefetch=1, grid=(n_seq, n_tiles + 1), in_specs=in_specs,
        out_specs=pl.BlockSpec((None, rows, dh), lambda n, s, pt: (n, 0, 0)),
        scratch_shapes=[pltpu.VMEM((rows, LANES), F32), pltpu.VMEM((rows, dh), F32)])
    o = pl.pallas_call(
        functools.partial(_decode_body, n_heads=n_heads, ppt=ppt), grid_spec=grid_spec,
        out_shape=jax.ShapeDtypeStruct((n_seq, rows, dh), F32),
        compiler_params=_cparams(2), name="sb_attn_decode")(page_table, *args)
    return o.reshape(n_seq * t_len, d).astype(BF16)


def _attn_layer(y, gain, n_seq, t_len, n_heads, w_qkv, q_gain, k_gain, sb_bias, w_o, layer,
                cache=None):
    m, d = y.shape
    dh = d // n_heads
    scale = dh ** -0.5
    qn = lambda accs, t, c: [_head_norm(accs[0], c[1][...], scale)]
    kn = lambda accs, t, c: [_head_norm(accs[0], c[1][...], 1.0)] * 2
    vn = lambda accs, t, c: [accs[0]] * 2
    (q,) = _norm_proj("attn_q", y, gain, w_qkv, layer, 0, d, [BF16], qn, [q_gain.reshape(1, dh)])
    k32, kb = _norm_proj("attn_k", y, gain, w_qkv, layer, d, d, [F32, BF16], kn,
                         [k_gain.reshape(1, dh)])
    v32, vb = _norm_proj("attn_v", y, gain, w_qkv, layer, 2 * d, d, [F32, BF16], vn)
    if cache is None:
        o = _attn_prompt(q, kb, vb, sb_bias, n_seq, t_len, n_heads)
    else:
        cache_k, cache_v, page_table = cache
        o = _attn_decode(q, k32, v32, sb_bias, cache_k, cache_v, layer, page_table, n_seq, t_len,
                         n_heads)
    (out,) = _fused_matmul("attn_out", m, d, d, [(o, "rows")], [], [(w_o, layer, 0)], [y], [F32],
                           _ident_prologue, _resid_epilogue)
    shape = (n_seq, t_len, n_heads, dh)
    return out, k32.reshape(shape), v32.reshape(shape)


def kernel(x_prompt, x_sample, state_ssm_re, state_ssm_im, state_conv, cache_k, cache_v, page_table, norm_mix, norm_mlp, ssm_lambda_re, ssm_lambda_im, ssm_log_dt, ssm_b_re, ssm_b_im, ssm_c_re, ssm_c_im, ssm_d, ssm_w_glu, conv_w_in, conv_w_dw, conv_w_out, attn_w_qkv, attn_q_norm, attn_k_norm, attn_sb_bias, attn_w_o, mlp_w_up, mlp_w_down):
    n_p, t_p, d = x_prompt.shape
    n_s, t_s, _ = x_sample.shape
    depth = norm_mix.shape[0]
    n_heads = attn_sb_bias.shape[1]
    assert conv_w_dw.shape[1] == 3 and d // n_heads == LANES
    chunk_p = min(S5_CHUNK, t_p)
    assert t_p % chunk_p == 0 and (t_p // chunk_p) & (t_p // chunk_p - 1) == 0

    w_glu, w_in, w_out = (w.astype(BF16) for w in (ssm_w_glu, conv_w_in, conv_w_out))
    w_qkv, w_o = attn_w_qkv.astype(BF16), attn_w_o.astype(BF16)
    w_up, w_down = mlp_w_up.astype(BF16), mlp_w_down.astype(BF16)

    y_p = x_prompt.reshape(n_p * t_p, d)
    y_s = x_sample.reshape(n_s * t_s, d)
    outs = {k: [] for k in ("re_p", "im_p", "re_s", "im_s", "cv_p", "cv_s", "k_p", "v_p", "k_s", "v_s")}
    for i in range(depth):
        kind, j = i % 3, i // 3
        if kind == 0:
            ssm = (ssm_lambda_re[j], ssm_lambda_im[j], ssm_log_dt[j], ssm_b_re[j], ssm_b_im[j],
                   ssm_c_re[j], ssm_c_im[j], ssm_d[j])
            tabs_p = _s5_tables(*ssm, chunk_p, t_p // chunk_p, True)
            tabs_s = _s5_tables(*ssm, t_s, 1, False)
            h0 = jnp.concatenate([state_ssm_re[j], state_ssm_im[j]], axis=-1).transpose(1, 0, 2)
            y_p, re_p, im_p = _s5_layer(y_p, norm_mix[i], n_p, t_p, tabs_p, None, w_glu, j, chunk_p)
            y_s, re_s, im_s = _s5_layer(y_s, norm_mix[i], n_s, t_s, tabs_s, h0, w_glu, j, t_s)
            outs["re_p"].append(re_p); outs["im_p"].append(im_p)
            outs["re_s"].append(re_s); outs["im_s"].append(im_s)
        elif kind == 1:
            y_p, cv_p = _conv_layer(y_p, norm_mix[i], n_p, t_p, None, w_in, conv_w_dw, w_out, j)
            y_s, cv_s = _conv_layer(y_s, norm_mix[i], n_s, t_s, state_conv[j], w_in, conv_w_dw,
                                    w_out, j)
            outs["cv_p"].append(cv_p); outs["cv_s"].append(cv_s)
        else:
            attn = (w_qkv, attn_q_norm[j], attn_k_norm[j], attn_sb_bias[j], w_o, j)
            y_p, k_p, v_p = _attn_layer(y_p, norm_mix[i], n_p, t_p, n_heads, *attn)
            y_s, k_s, v_s = _attn_layer(y_s, norm_mix[i], n_s, t_s, n_heads, *attn,
                                        cache=(cache_k, cache_v, page_table))
            outs["k_p"].append(k_p); outs["v_p"].append(v_p)
            outs["k_s"].append(k_s); outs["v_s"].append(v_s)
        y_p, y_s = _mlp(y_p, y_s, norm_mlp[i], w_up, w_down, i)
    st = lambda key: jnp.stack(outs[key])
    return (y_p.reshape(n_p, t_p, d), y_s.reshape(n_s, t_s, d),
            st("re_p"), st("im_p"), st("re_s"), st("im_s"), st("cv_p"), st("cv_s"),
            st("k_p"), st("v_p"), st("k_s"), st("v_s"))
```

```python
import functools
import math

import jax
import jax.numpy as jnp
from jax import lax
from jax.experimental import pallas as pl
from jax.experimental.pallas import tpu as pltpu

F32 = jnp.float32
BF16 = jnp.bfloat16
EPS = 1e-6
LANES = 128
SUBLANES = 8
VMEM_LIMIT_BYTES = 56 * 1024 * 1024
VMEM_BUDGET_BYTES = 44 * 1024 * 1024
S5_CHUNK = 16
HIGHEST = lax.Precision.HIGHEST


def _cparams(n_axes):
    return pltpu.CompilerParams(dimension_semantics=("arbitrary",) * n_axes,
                                vmem_limit_bytes=VMEM_LIMIT_BYTES)


def _rms(x, g):
    ms = jnp.mean(x * x, axis=-1, keepdims=True)
    return x * lax.rsqrt(ms + EPS) * g


def _iota(shape, dim):
    return lax.broadcasted_iota(jnp.int32, shape, dim)


def _mod(x, n):
    return x & (n - 1) if n & (n - 1) == 0 else lax.rem(x, n)


def _div(x, n):
    return x >> (n.bit_length() - 1) if n & (n - 1) == 0 else lax.div(x, n)


def _plan_tiles(m, kdim, n_out, row_bytes_per_row, n_w, tile_itemsizes):
    for tm, tn, n_buf in ((1024, 1024, 2), (1024, 512, 2), (1024, 1024, 1), (1024, 512, 1),
                          (512, 1024, 2), (512, 512, 2), (512, 512, 1)):
        tm, tn = min(tm, m), min(tn, n_out)
        stream = n_w * kdim * tn * 2 + tm * tn * sum(tile_itemsizes)
        fixed = tm * kdim * 2 + 2 * n_w * tm * tn * 4
        if n_buf * tm * row_bytes_per_row + 2 * stream + fixed <= VMEM_BUDGET_BYTES:
            break
    return tm, tn, n_buf


def _fused_matmul(name, m, kdim, n_out, row_in, const_in, weights, tile_in, out_dtypes,
                  prologue, epilogue):
    n_row, n_const, n_w, n_tile, n_o = (len(row_in), len(const_in), len(weights),
                                        len(tile_in), len(out_dtypes))
    size = lambda dt: jnp.dtype(dt).itemsize
    tm, tn, n_buf = _plan_tiles(
        m, kdim, n_out, sum(a.shape[1] * size(a.dtype) for a, kind in row_in if kind == "rows"),
        n_w, [size(a.dtype) for a in tile_in] + [size(dt) for dt in out_dtypes])
    row_mode = pl.Buffered(1) if n_buf == 1 else None
    per = tm // SUBLANES
    in_specs, args = [], []
    for arr, kind in row_in:
        if kind == "rows":
            spec = pl.BlockSpec((tm, arr.shape[1]), lambda i, j: (i, 0), pipeline_mode=row_mode)
        else:
            spec = pl.BlockSpec((SUBLANES, arr.shape[1]),
                                lambda i, j: (jnp.maximum(i * per - 1, 0), 0))
        in_specs.append(spec)
        args.append(arr)
    for arr in const_in:
        in_specs.append(pl.BlockSpec(arr.shape, lambda i, j, nd=arr.ndim: (0,) * nd))
        args.append(arr)
    for arr, layer, col in weights:
        in_specs.append(pl.BlockSpec((None, kdim, tn),
                                     lambda i, j, layer=layer, off=col // tn: (layer, 0, j + off)))
        args.append(arr)
    for arr in tile_in:
        in_specs.append(pl.BlockSpec((tm, tn), lambda i, j: (i, j)))
        args.append(arr)
    out_shape = [jax.ShapeDtypeStruct((m, n_out), dt) for dt in out_dtypes]
    out_specs = [pl.BlockSpec((tm, tn), lambda i, j: (i, j)) for _ in out_dtypes]

    def body(*refs):
        p = 0
        row_refs = refs[p:p + n_row]; p += n_row
        const_refs = refs[p:p + n_const]; p += n_const
        w_refs = refs[p:p + n_w]; p += n_w
        tile_refs = refs[p:p + n_tile]; p += n_tile
        out_refs = refs[p:p + n_o]; p += n_o
        lhs_ref = refs[p]

        @pl.when(pl.program_id(1) == 0)
        def _():
            lhs_ref[...] = prologue(row_refs, const_refs, tm).astype(BF16)

        lhs = lhs_ref[...]
        accs = [jnp.dot(lhs, w[...], preferred_element_type=F32) for w in w_refs]
        for o_ref, val in zip(out_refs, epilogue(accs, tile_refs, const_refs)):
            o_ref[...] = val.astype(o_ref.dtype)

    return pl.pallas_call(
        body, grid=(m // tm, n_out // tn), in_specs=in_specs, out_specs=out_specs,
        out_shape=out_shape, scratch_shapes=[pltpu.VMEM((tm, kdim), BF16)],
        compiler_params=_cparams(2), name=name)(*args)


def _norm_prologue(row_refs, const_refs, tm):
    return _rms(row_refs[0][...], const_refs[0][...])


def _ident_prologue(row_refs, const_refs, tm):
    return row_refs[0][...]


def _head_norm(acc, gain, scale):
    segs = []
    for h in range(acc.shape[1] // LANES):
        seg = acc[:, h * LANES:(h + 1) * LANES]
        ms = jnp.mean(seg * seg, axis=-1, keepdims=True)
        y = seg * lax.rsqrt(ms + EPS) * gain
        segs.append(y * scale if scale != 1.0 else y)
    return segs[0] if len(segs) == 1 else jnp.concatenate(segs, axis=-1)


def _norm_proj(name, x, gain, w, layer, col, n_out, out_dtypes, epilogue, extra_const=()):
    m, kdim = x.shape
    return _fused_matmul(name, m, kdim, n_out, [(x, "rows")],
                         [gain.reshape(1, kdim)] + list(extra_const), [(w, layer, col)], [],
                         out_dtypes, _norm_prologue, epilogue)


def _mlp_body(x_ref, g_ref, wu_ref, wd_ref, o_ref, xn_ref, acc_ref):
    f = pl.program_id(1)

    @pl.when(f == 0)
    def _():
        xn_ref[...] = _rms(x_ref[...], g_ref[...]).astype(BF16)
        acc_ref[...] = jnp.zeros_like(acc_ref)

    h = jnp.maximum(jnp.dot(xn_ref[...], wu_ref[...], preferred_element_type=F32), 0.0)
    acc_ref[...] += jnp.dot((h * h).astype(BF16), wd_ref[...], preferred_element_type=F32)

    @pl.when(f == pl.num_programs(1) - 1)
    def _():
        o_ref[...] = x_ref[...] + acc_ref[...]


def _mlp(x, gain, w_up, w_down, layer):
    m, d = x.shape
    ff = w_up.shape[2]
    tm = min(512, m)
    tf = min(512 if m > 512 else 2048, ff)
    return pl.pallas_call(
        _mlp_body, grid=(m // tm, ff // tf),
        in_specs=[pl.BlockSpec((tm, d), lambda i, f: (i, 0)),
                  pl.BlockSpec((1, d), lambda i, f: (0, 0)),
                  pl.BlockSpec((None, d, tf), lambda i, f: (layer, 0, f)),
                  pl.BlockSpec((None, tf, d), lambda i, f: (layer, f, 0))],
        out_specs=pl.BlockSpec((tm, d), lambda i, f: (i, 0)),
        out_shape=jax.ShapeDtypeStruct((m, d), F32),
        scratch_shapes=[pltpu.VMEM((tm, d), BF16), pltpu.VMEM((tm, d), F32)],
        compiler_params=_cparams(2), name="mlp")(x, gain.reshape(1, d), w_up, w_down)


def _norm_body(x_ref, g_ref, o_ref):
    o_ref[...] = _rms(x_ref[...], g_ref[...])


def _norm(x, gain):
    m, d = x.shape
    tm = min(512, m)
    return pl.pallas_call(
        _norm_body, grid=(m // tm,),
        in_specs=[pl.BlockSpec((tm, d), lambda i: (i, 0)), pl.BlockSpec((1, d), lambda i: (0, 0))],
        out_specs=pl.BlockSpec((tm, d), lambda i: (i, 0)),
        out_shape=jax.ShapeDtypeStruct((m, d), F32),
        compiler_params=_cparams(1), name="rmsnorm")(x, gain.reshape(1, d))


def _complex_scale(acos, asin, h):
    return acos * h + asin * pltpu.roll(h, h.shape[1] // 2, axis=1)


def _gelu(y):
    return y * (0.5 * (1.0 + jnp.tanh(math.sqrt(2.0 / math.pi) * (y + 0.044715 * (y * y * y)))))


def _s5_step_body(u_ref, m_ref, bc_ref, cp_ref, d_ref, acos_ref, asin_ref, h0_ref, z_ref, hl_ref,
                  *, gt):
    def one_group(g, carry):
        u = u_ref[g]
        ub = u.astype(BF16)
        h_prev = h0_ref[g]
        hl_ref[g] = (jnp.dot(ub, bc_ref[g], preferred_element_type=F32)
                     + _complex_scale(acos_ref[g][0:1], asin_ref[g][0:1], h_prev))
        y = (jnp.dot(ub, m_ref[g], preferred_element_type=F32)
             + jnp.dot(h_prev.astype(BF16), cp_ref[g], preferred_element_type=F32)
             + u * d_ref[g])
        z_ref[g] = _gelu(y).astype(z_ref.dtype)
        return carry

    lax.fori_loop(0, gt, one_group, 0)


def _s5_seq_body(x_ref, bd_ref, bct_ref, cpt_ref, acos_ref, asin_ref, z_ref, hl_ref,
                 m8_ref, bc8_ref, cp8_ref, h_ref, *, chunk, n_chunk, n_seq, c):
    gl = LANES // c
    rows = n_seq * n_chunk
    blk = lambda i: slice(i * LANES, (i + 1) * LANES)
    for s in range(chunk):
        for t in range(s, chunk):
            m8_ref[blk(s), blk(t)] = bd_ref[t - s]
        if s % 2:
            m8_ref[blk(s), blk(s - 1)] = jnp.zeros((LANES, LANES), BF16)
    row_grp = _div(_iota((LANES, LANES), 0), c)
    lane_grp = _div(_iota((LANES, LANES), 1), c)
    for s in range(chunk):
        for g in range(gl):
            bc8_ref[blk(s), blk(g)] = jnp.where(row_grp == g, bct_ref[s], 0.0).astype(BF16)
            cp8_ref[blk(g), blk(s)] = jnp.where(lane_grp == g, cpt_ref[s], 0.0).astype(BF16)

    u2 = jnp.concatenate([x_ref[pl.ds(s, rows, stride=chunk), :].astype(BF16)
                          for s in range(chunk)], axis=1)
    h = jnp.dot(u2, bc8_ref[...], preferred_element_type=F32)
    kidx = _iota((rows, 1), 0) & (n_chunk - 1)
    swap = lambda v: jnp.concatenate(
        [pltpu.roll(v[:, blk(g)], LANES // 2, axis=1) for g in range(gl)], axis=1)
    shift, si = 1, 0
    while shift < n_chunk:
        sh = jnp.where(kidx >= shift, pltpu.roll(h, shift, axis=0), 0.0)
        h = h + acos_ref[si:si + 1, :] * sh + asin_ref[si:si + 1, :] * swap(sh)
        shift, si = shift * 2, si + 1
    h_ref[...] = h
    h_prev = jnp.where(kidx >= 1, pltpu.roll(h, 1, axis=0), 0.0).astype(BF16)
    for t in range(0, chunk, 2):
        cols = slice(t * LANES, (t + 2) * LANES)
        y = (jnp.dot(u2[:, :(t + 2) * LANES], m8_ref[:(t + 2) * LANES, cols],
                     preferred_element_type=F32)
             + jnp.dot(h_prev, cp8_ref[:, cols], preferred_element_type=F32))
        z = _gelu(y)
        z_ref[pl.ds(t, rows, stride=chunk), :] = z[:, :LANES]
        z_ref[pl.ds(t + 1, rows, stride=chunk), :] = z[:, LANES:]
    for n in range(n_seq):
        hl_ref[pl.ds(n, 1), :] = h_ref[pl.ds((n + 1) * n_chunk - 1, 1), :]


def _s5_tables(lam_re, lam_im, log_dt, b_re, b_im, c_re, c_im, d, chunk, n_chunk, per_tile):
    g, p = lam_re.shape
    c = b_re.shape[2]
    ldt = lax.complex(lam_re, lam_im) * jnp.exp(log_dt)[:, None]
    a_bar = jnp.exp(ldt)
    bbar = ((a_bar - 1.0) / lax.complex(lam_re, lam_im))[..., None] * lax.complex(b_re, b_im)
    taus = jnp.arange(chunk + 1, dtype=F32)
    apow = jnp.exp(ldt[:, None, :] * taus[None, :, None])
    cc = lax.complex(c_re, c_im)
    w1 = cc[:, None, :, :] * apow[:, :chunk, None, :]
    lhs = jnp.concatenate([jnp.real(w1), -jnp.imag(w1)], axis=-1)
    rhs = jnp.concatenate([jnp.real(bbar), jnp.imag(bbar)], axis=1)
    kt = jnp.einsum("gtoq,gqi->gtoi", lhs, rhs, precision=HIGHEST)
    rev = apow[:, :chunk][:, ::-1]
    bcx = (rev[:, :, None, :] * bbar.transpose(0, 2, 1)[:, None, :, :]).reshape(g, chunk * c, p)
    bc = jnp.concatenate([jnp.real(bcx), jnp.imag(bcx)], axis=-1)
    ct = (cc[:, None, :, :] * apow[:, 1:, None, :]).transpose(0, 3, 1, 2).reshape(g, p, chunk * c)
    cp = jnp.concatenate([jnp.real(ct), -jnp.imag(ct)], axis=1)
    shifts = [1]
    while shifts[-1] * 2 < n_chunk:
        shifts.append(shifts[-1] * 2)
    while len(shifts) < SUBLANES:
        shifts.append(shifts[-1])
    apl = jnp.exp(ldt[:, None, :] * (chunk * jnp.asarray(shifts, F32))[None, :, None])
    acos = jnp.concatenate([jnp.real(apl), jnp.real(apl)], axis=-1)
    asin = jnp.concatenate([-jnp.imag(apl), jnp.imag(apl)], axis=-1)
    if not per_tile:
        kpad = jnp.concatenate([kt, jnp.zeros((g, 1, c, c), F32)], axis=1)
        s_idx = jnp.arange(chunk)[:, None]
        t_idx = jnp.arange(chunk)[None, :]
        tau = jnp.where(t_idx >= s_idx, t_idx - s_idx, chunk)
        mmat = kpad[:, tau].transpose(0, 1, 4, 2, 3).reshape(g, chunk * c, chunk * c)
        dt = jnp.tile(d.reshape(g, 1, c), (1, chunk, 1)).reshape(g, 1, chunk * c)
        return mmat.astype(BF16), bc.astype(BF16), cp.astype(BF16), dt, acos, asin
    gl = LANES // c
    nt = g // gl
    p2 = 2 * p
    kd = kt.at[:, 0].add(d.reshape(g, c)[:, :, None] * jnp.eye(c, dtype=F32))
    ktr = kd.reshape(nt, gl, chunk, c, c).transpose(0, 2, 1, 4, 3)
    bd = ktr[:, :, :, :, None, :] * jnp.eye(gl, dtype=F32)[None, None, :, None, :, None]
    bd = bd.reshape(nt, chunk, LANES, LANES).astype(BF16)
    bct = bc.reshape(nt, gl, chunk, c, p2).transpose(0, 2, 1, 3, 4).reshape(nt, chunk, LANES, p2)
    cpt = cp.reshape(nt, gl, p2, chunk, c).transpose(0, 3, 2, 1, 4).reshape(nt, chunk, p2, LANES)
    tile = lambda a: a.reshape(nt, gl, SUBLANES, p2).transpose(0, 2, 1, 3).reshape(
        nt, SUBLANES, gl * p2)
    return bd, bct, cpt, tile(acos), tile(asin)


def _s5_step(hn, n_seq, t_len, tabs, h0):
    mmat, bc, cp, dt, acos, asin = tabs
    g, lc = mmat.shape[:2]
    p2 = bc.shape[2]
    c = lc // t_len
    u_r = hn.reshape(n_seq, t_len, g, c).transpose(2, 0, 1, 3).reshape(g, n_seq, lc)
    gt = min(8, g)
    grp = lambda shape: pl.BlockSpec((gt,) + shape, lambda i: (i, 0, 0))
    z_r, h_last = pl.pallas_call(
        functools.partial(_s5_step_body, gt=gt), grid=(g // gt,),
        in_specs=[grp((n_seq, lc)), grp((lc, lc)), grp((lc, p2)), grp((p2, lc)), grp((1, lc)),
                  grp((SUBLANES, p2)), grp((SUBLANES, p2)), grp((n_seq, p2))],
        out_specs=[grp((n_seq, lc)), grp((n_seq, p2))],
        out_shape=[jax.ShapeDtypeStruct((g, n_seq, lc), BF16),
                   jax.ShapeDtypeStruct((g, n_seq, p2), F32)],
        compiler_params=_cparams(1), name="s5_step")(u_r, mmat, bc, cp, dt, acos, asin, h0)
    z = z_r.reshape(g, n_seq, t_len, c).transpose(1, 2, 0, 3).reshape(n_seq * t_len, g * c)
    return z, h_last.transpose(1, 0, 2)


def _s5_seq(hn, n_seq, t_len, tabs, chunk):
    bd, bct, cpt, acos, asin = tabs
    m, d = hn.shape
    nt = bd.shape[0]
    p2 = bct.shape[3]
    gl = acos.shape[2] // p2
    c = LANES // gl
    n_chunk = t_len // chunk
    assert p2 == LANES and d == nt * LANES
    assert chunk % 2 == 0
    tab = lambda a: pl.BlockSpec((None,) + a.shape[1:], lambda j: (j,) + (0,) * (a.ndim - 1))
    z, h_last = pl.pallas_call(
        functools.partial(_s5_seq_body, chunk=chunk, n_chunk=n_chunk, n_seq=n_seq, c=c),
        grid=(nt,),
        in_specs=[pl.BlockSpec((m, LANES), lambda j: (0, j)),
                  tab(bd), tab(bct), tab(cpt), tab(acos), tab(asin)],
        out_specs=[pl.BlockSpec((m, LANES), lambda j: (0, j)),
                   pl.BlockSpec((None, n_seq, gl * p2), lambda j: (j, 0, 0))],
        out_shape=[jax.ShapeDtypeStruct((m, d), F32),
                   jax.ShapeDtypeStruct((nt, n_seq, gl * p2), F32)],
        scratch_shapes=[pltpu.VMEM((chunk * LANES, chunk * LANES), BF16),
                        pltpu.VMEM((chunk * LANES, gl * p2), BF16),
                        pltpu.VMEM((gl * p2, chunk * LANES), BF16),
                        pltpu.VMEM((n_seq * n_chunk, gl * p2), F32)],
        compiler_params=_cparams(1), name="s5_seq")(hn, bd, bct, cpt, acos, asin)
    h_last = h_last.reshape(nt, n_seq, gl, p2).transpose(1, 0, 2, 3).reshape(n_seq, nt * gl, p2)
    return z, h_last


def _glu_epilogue(accs, tile_refs, const_refs):
    a, gate = accs
    return [tile_refs[0][...] + a * jax.nn.sigmoid(gate)]


def _s5_layer(y, gain, n_seq, t_len, tabs, h0, w_glu, layer, chunk):
    m, d = y.shape
    if h0 is None:
        z, h_last = _s5_seq(_norm(y, gain), n_seq, t_len, tabs, chunk)
    else:
        z, h_last = _s5_step(_norm(y, gain), n_seq, t_len, tabs, h0)
    (out,) = _fused_matmul("s5_glu", m, d, d, [(z, "rows")], [],
                           [(w_glu, layer, 0), (w_glu, layer, d)], [y], [F32],
                           _ident_prologue, _glu_epilogue)
    p = h_last.shape[2] // 2
    return out, h_last[..., :p], h_last[..., p:]


def _conv_in_epilogue(accs, tile_refs, const_refs):
    gate_b, gate_c, v = accs
    return [gate_b, gate_c * v]


def _conv_prologue(row_refs, const_refs, tm, *, t_len, has_buf):
    w = const_refs[0][...]
    bg, cv = row_refs[0][...], row_refs[1][...]
    t = _mod(pl.program_id(0) * tm + _iota((tm, 1), 0), t_len)
    if has_buf:
        b0, b1 = row_refs[2][...], row_refs[3][...]
        r1 = jnp.where(t >= 1, pltpu.roll(cv, 1, axis=0), b1)
        r2 = jnp.where(t >= 2, pltpu.roll(cv, 2, axis=0), jnp.where(t == 1, b1, b0))
    else:
        full = jnp.concatenate([row_refs[2][...], cv], axis=0)
        r1 = jnp.where(t >= 1, pltpu.roll(full, 1, axis=0)[SUBLANES:], 0.0)
        r2 = jnp.where(t >= 2, pltpu.roll(full, 2, axis=0)[SUBLANES:], 0.0)
    return bg * (w[2:3] * cv + w[1:2] * r1 + w[0:1] * r2)


def _resid_epilogue(accs, tile_refs, const_refs):
    return [tile_refs[0][...] + accs[0]]


def _conv_layer(y, gain, n_seq, t_len, buf, w_in, w_dw, w_out, layer):
    m, d = y.shape
    assert t_len >= 2
    bg, cv = _fused_matmul("conv_in", m, d, d, [(y, "rows")], [gain.reshape(1, d)],
                           [(w_in, layer, 0), (w_in, layer, d), (w_in, layer, 2 * d)],
                           [], [F32, F32], _norm_prologue, _conv_in_epilogue)
    rows = [(bg, "rows"), (cv, "rows")]
    if buf is None:
        rows.append((cv, "prev"))
    else:
        assert m <= 512
        rows += [(jnp.repeat(buf[:, k], t_len, axis=0), "rows") for k in range(2)]
    (out,) = _fused_matmul(
        "conv_out", m, d, d, rows, [w_dw[layer]], [(w_out, layer, 0)], [y], [F32],
        functools.partial(_conv_prologue, t_len=t_len, has_buf=buf is not None),
        _resid_epilogue)
    return out, cv.reshape(n_seq, t_len, d)[:, t_len - 2:]


def _sb_logs(z, mask):
    sp = jnp.log(1.0 + jnp.exp(-jnp.abs(z)))
    log_b = jnp.minimum(z, 0.0) - sp
    log_1m = log_b - z
    if mask is not None:
        log_1m = jnp.where(mask, log_1m, 0.0)
    return log_b, log_1m


def _sb_suffix(log_1m, upper):
    hi = log_1m.astype(BF16)
    lo = (log_1m - hi.astype(F32)).astype(BF16)
    return (jnp.dot(hi, upper, preferred_element_type=F32)
            + jnp.dot(lo, upper, preferred_element_type=F32))


def _sb_finish(log_b, log_1m, suffix, mask, run):
    a = jnp.exp(log_b + suffix + run)
    if mask is not None:
        a = jnp.where(mask, a, 0.0)
    return a, run + jnp.sum(log_1m, axis=-1, keepdims=True)


def _sb_weights(z, mask, upper, run):
    log_b, log_1m = _sb_logs(z, mask)
    return _sb_finish(log_b, log_1m, _sb_suffix(log_1m, upper), mask, run)


def _later_key_matrix(tk):
    return (_iota((tk, tk), 0) > _iota((tk, tk), 1)).astype(BF16)


def _qk(q, k_blk):
    return lax.dot_general(q, k_blk, (((1,), (1,)), ((), ())), preferred_element_type=F32)


def _attn_body(bias_ref, q_ref, k_ref, v_ref, o_ref, *, tq, tr, dh, hps):
    hg = pl.program_id(1)
    qi = pl.program_id(2)
    upper = _later_key_matrix(tq)
    units = [(hh, r0) for hh in range(hps) for r0 in range(0, tq, tr)]

    def tile(kb, carries, masked):
        start = pl.multiple_of(kb * tq, tq)
        lanes = lambda hh: slice(hh * dh, (hh + 1) * dh)
        masks = [(_iota((tr, tq), 1) < _iota((tr, tq), 0) + r0) if masked else None
                 for _, r0 in units]
        zs = [_qk(q_ref[r0:r0 + tr, lanes(hh)], k_ref[pl.ds(start, tq), lanes(hh)])
              + bias_ref[hg * hps + hh] for hh, r0 in units]
        logs = [_sb_logs(z, m) for z, m in zip(zs, masks)]
        sufs = [_sb_suffix(l1m, upper) for _, l1m in logs]
        fins = [_sb_finish(lb, l1m, suf, m, run)
                for (lb, l1m), suf, m, (run, _) in zip(logs, sufs, masks, carries)]
        accs = [acc + jnp.dot(a.astype(BF16), v_ref[pl.ds(start, tq), lanes(hh)],
                              preferred_element_type=F32)
                for (a, _), (_, acc), (hh, _) in zip(fins, carries, units)]
        return tuple((run, acc) for (_, run), acc in zip(fins, accs))

    carries = tuple((jnp.zeros((tr, 1), F32), jnp.zeros((tr, dh), F32)) for _ in units)
    carries = tile(qi, carries, True)
    carries = lax.fori_loop(0, qi, lambda it, c: tile(qi - 1 - it, c, False), carries)
    for (hh, r0), (_, acc) in zip(units, carries):
        o_ref[r0:r0 + tr, hh * dh:(hh + 1) * dh] = acc.astype(o_ref.dtype)


def _attn_prompt(q, k, v, bias, n_seq, t_len, n_heads):
    m, d = q.shape
    dh = d // n_heads
    tq = min(256, t_len)
    hps = next(h for h in (4, 2, 1) if n_heads % h == 0)
    qb = t_len // tq
    grid_spec = pltpu.PrefetchScalarGridSpec(
        num_scalar_prefetch=1, grid=(n_seq, n_heads // hps, qb),
        in_specs=[pl.BlockSpec((tq, hps * dh), lambda n, h, i, b: (n * qb + i, h)),
                  pl.BlockSpec((t_len, hps * dh), lambda n, h, i, b: (n, h)),
                  pl.BlockSpec((t_len, hps * dh), lambda n, h, i, b: (n, h))],
        out_specs=pl.BlockSpec((tq, hps * dh), lambda n, h, i, b: (n * qb + i, h)))
    return pl.pallas_call(
        functools.partial(_attn_body, tq=tq, tr=min(128, tq), dh=dh, hps=hps),
        grid_spec=grid_spec,
        out_shape=jax.ShapeDtypeStruct((m, d), BF16),
        compiler_params=_cparams(3), name="sb_attn_prompt")(bias, q, k, v)


def _decode_body(pt_ref, q_ref, bias_ref, e_ref, et_ref, hm_ref, hm32_ref, *refs, n_heads, ppt):
    kn_refs, vn_refs = refs[:ppt], refs[ppt:2 * ppt]
    kc_refs, vc_refs = refs[2 * ppt:3 * ppt], refs[3 * ppt:4 * ppt]
    o_ref, run_ref, acc_ref = refs[4 * ppt:]
    step = pl.program_id(1)
    page = vn_refs[0].shape[0] // n_heads
    rows = q_ref.shape[0]
    tk = ppt * page

    def tile(k_refs, v_refs, mask, run, acc):
        parts = []
        for slot in range(ppt):
            z_rows = _qk(q_ref[...], k_refs[slot][...].astype(BF16)) * hm32_ref[...]
            hi = z_rows.astype(BF16)
            parts += [hi, (z_rows - hi.astype(F32)).astype(BF16)]
        zc = jnp.dot(jnp.concatenate(parts, axis=0), et_ref[...], preferred_element_type=F32)
        zs = [zc[2 * s * rows:(2 * s + 1) * rows] + zc[(2 * s + 1) * rows:(2 * s + 2) * rows]
              for s in range(ppt)]
        z = (zs[0] if ppt == 1 else jnp.concatenate(zs, axis=1)) + bias_ref[...]
        a, run = _sb_weights(z, mask, _later_key_matrix(tk), run)
        a = a.astype(BF16)
        a_st = jnp.concatenate([a[:, s * page:(s + 1) * page] for s in range(ppt)], axis=0)
        a_rows = jnp.dot(a_st, e_ref[...], preferred_element_type=F32).astype(BF16)
        for slot in range(ppt):
            acc = acc + jnp.dot(a_rows[slot * rows:(slot + 1) * rows] * hm_ref[...],
                                v_refs[slot][...].astype(BF16), preferred_element_type=F32)
        run_ref[...] = jnp.broadcast_to(run, run_ref.shape)
        acc_ref[...] = acc

    @pl.when(step == 0)
    def _():
        mask = _iota((rows, tk), 1) < _div(_iota((rows, tk), 0), n_heads)
        tile(kn_refs, vn_refs, mask, jnp.zeros((rows, 1), F32), jnp.zeros(acc_ref.shape, F32))

    @pl.when(step > 0)
    def _():
        tile(kc_refs, vc_refs, None, run_ref[:, 0:1], acc_ref[...])

    @pl.when(step == pl.num_programs(1) - 1)
    def _():
        o_ref[...] = acc_ref[...]


def _attn_decode(q, k_new, v_new, bias, cache_k, cache_v, layer, page_table, n_seq, t_len,
                 n_heads):
    d = q.shape[1]
    dh = d // n_heads
    n_layers, n_pool, page = cache_k.shape[:3]
    n_pages = page_table.shape[1]
    ppt = next(p for p in (4, 2, 1) if n_pages % p == 0)
    n_tiles = n_pages // ppt
    rows = t_len * n_heads
    assert n_heads % SUBLANES == 0 and t_len <= page
    q_rows = q.reshape(n_seq, rows, dh)
    bias_rows = jnp.tile(bias, t_len).reshape(rows, 1)
    key_of_row = jnp.arange(page * n_heads) // n_heads
    expand = (key_of_row[None, :] == jnp.arange(page)[:, None]).astype(BF16)
    head_of_row = jnp.arange(page * n_heads) % n_heads
    head_mask = head_of_row[None, :] == (jnp.arange(rows) % n_heads)[:, None]
    pad_t = ((0, 0), (0, ppt * page - t_len), (0, 0), (0, 0))
    as_rows = lambda x: jnp.pad(x.reshape(n_seq, t_len, n_heads, dh), pad_t).reshape(
        n_seq, ppt, page * n_heads, dh)
    kn, vn = as_rows(k_new), as_rows(v_new)
    kc = cache_k.reshape(n_layers, n_pool, page * n_heads, dh)
    vc = cache_v.reshape(n_layers, n_pool, page * n_heads, dh)

    def phys(n, s, pt, slot):
        return pt[n, (n_tiles - jnp.maximum(s, 1)) * ppt + slot]

    const = lambda shape: pl.BlockSpec(shape, lambda n, s, pt: (0,) * len(shape))
    in_specs = [pl.BlockSpec((None, rows, dh), lambda n, s, pt: (n, 0, 0)),
                const((rows, 1)), const(expand.shape), const(expand.shape[::-1]),
                const(head_mask.shape), const(head_mask.shape)]
    args = [q_rows, bias_rows, expand, expand.T, head_mask.astype(BF16), head_mask.astype(F32)]
    blk = (None, None, page * n_heads, dh)
    for arr in (kn, vn):
        for slot in range(ppt):
            in_specs.append(pl.BlockSpec(blk, lambda n, s, pt, slot=slot: (n, slot, 0, 0)))
            args.append(arr)
    for arr in (kc, vc):
        for slot in range(ppt):
            in_specs.append(pl.BlockSpec(
                blk, lambda n, s, pt, slot=slot: (layer, phys(n, s, pt, slot), 0, 0)))
            args.append(arr)
    grid_spec = pltpu.PrefetchScalarGridSpec(
        num_scalar_prefetch=1, grid=(n_seq, n_tiles + 1), in_specs=in_specs,
        out_specs=pl.BlockSpec((None, rows, dh), lambda n, s, pt: (n, 0, 0)),
        scratch_shapes=[pltpu.VMEM((rows, LANES), F32), pltpu.VMEM((rows, dh), F32)])
    o = pl.pallas_call(
        functools.partial(_decode_body, n_heads=n_heads, ppt=ppt), grid_spec=grid_spec,
        out_shape=jax.ShapeDtypeStruct((n_seq, rows, dh), F32),
        compiler_params=_cparams(2), name="sb_attn_decode")(page_table, *args)
    return o.reshape(n_seq * t_len, d).astype(BF16)


def _attn_layer(y, gain, n_seq, t_len, n_heads, w_qkv, q_gain, k_gain, sb_bias, w_o, layer,
                cache=None):
    m, d = y.shape
    dh = d // n_heads
    scale = dh ** -0.5
    qn = lambda accs, t, c: [_head_norm(accs[0], c[1][...], scale)]
    kn = lambda accs, t, c: [_head_norm(accs[0], c[1][...], 1.0)] * 2
    vn = lambda accs, t, c: [accs[0]] * 2
    (q,) = _norm_proj("attn_q", y, gain, w_qkv, layer, 0, d, [BF16], qn, [q_gain.reshape(1, dh)])
    k32, kb = _norm_proj("attn_k", y, gain, w_qkv, layer, d, d, [F32, BF16], kn,
                         [k_gain.reshape(1, dh)])
    v32, vb = _norm_proj("attn_v", y, gain, w_qkv, layer, 2 * d, d, [F32, BF16], vn)
    if cache is None:
        o = _attn_prompt(q, kb, vb, sb_bias, n_seq, t_len, n_heads)
    else:
        cache_k, cache_v, page_table = cache
        o = _attn_decode(q, k32, v32, sb_bias, cache_k, cache_v, layer, page_table, n_seq, t_len,
                         n_heads)
    (out,) = _fused_matmul("attn_out", m, d, d, [(o, "rows")], [], [(w_o, layer, 0)], [y], [F32],
                           _ident_prologue, _resid_epilogue)
    shape = (n_seq, t_len, n_heads, dh)
    return out, k32.reshape(shape), v32.reshape(shape)


def kernel(x_prompt, x_sample, state_ssm_re, state_ssm_im, state_conv, cache_k, cache_v, page_table, norm_mix, norm_mlp, ssm_lambda_re, ssm_lambda_im, ssm_log_dt, ssm_b_re, ssm_b_im, ssm_c_re, ssm_c_im, ssm_d, ssm_w_glu, conv_w_in, conv_w_dw, conv_w_out, attn_w_qkv, attn_q_norm, attn_k_norm, attn_sb_bias, attn_w_o, mlp_w_up, mlp_w_down):
    n_p, t_p, d = x_prompt.shape
    n_s, t_s, _ = x_sample.shape
    depth = norm_mix.shape[0]
    n_heads = attn_sb_bias.shape[1]
    assert conv_w_dw.shape[1] == 3 and d // n_heads == LANES
    chunk_p = min(S5_CHUNK, t_p)
    assert t_p % chunk_p == 0 and (t_p // chunk_p) & (t_p // chunk_p - 1) == 0

    w_glu, w_in, w_out = (w.astype(BF16) for w in (ssm_w_glu, conv_w_in, conv_w_out))
    w_qkv, w_o = attn_w_qkv.astype(BF16), attn_w_o.astype(BF16)
    w_up, w_down = mlp_w_up.astype(BF16), mlp_w_down.astype(BF16)

    y_p = x_prompt.reshape(n_p * t_p, d)
    y_s = x_sample.reshape(n_s * t_s, d)
    outs = {k: [] for k in ("re_p", "im_p", "re_s", "im_s", "cv_p", "cv_s", "k_p", "v_p", "k_s", "v_s")}
    for i in range(depth):
        kind, j = i % 3, i // 3
        if kind == 0:
            ssm = (ssm_lambda_re[j], ssm_lambda_im[j], ssm_log_dt[j], ssm_b_re[j], ssm_b_im[j],
                   ssm_c_re[j], ssm_c_im[j], ssm_d[j])
            tabs_p = _s5_tables(*ssm, chunk_p, t_p // chunk_p, True)
            tabs_s = _s5_tables(*ssm, t_s, 1, False)
            h0 = jnp.concatenate([state_ssm_re[j], state_ssm_im[j]], axis=-1).transpose(1, 0, 2)
            y_p, re_p, im_p = _s5_layer(y_p, norm_mix[i], n_p, t_p, tabs_p, None, w_glu, j, chunk_p)
            y_s, re_s, im_s = _s5_layer(y_s, norm_mix[i], n_s, t_s, tabs_s, h0, w_glu, j, t_s)
            outs["re_p"].append(re_p); outs["im_p"].append(im_p)
            outs["re_s"].append(re_s); outs["im_s"].append(im_s)
        elif kind == 1:
            y_p, cv_p = _conv_layer(y_p, norm_mix[i], n_p, t_p, None, w_in, conv_w_dw, w_out, j)
            y_s, cv_s = _conv_layer(y_s, norm_mix[i], n_s, t_s, state_conv[j], w_in, conv_w_dw,
                                    w_out, j)
            outs["cv_p"].append(cv_p); outs["cv_s"].append(cv_s)
        else:
            attn = (w_qkv, attn_q_norm[j], attn_k_norm[j], attn_sb_bias[j], w_o, j)
            y_p, k_p, v_p = _attn_layer(y_p, norm_mix[i], n_p, t_p, n_heads, *attn)
            y_s, k_s, v_s = _attn_layer(y_s, norm_mix[i], n_s, t_s, n_heads, *attn,
                                        cache=(cache_k, cache_v, page_table))
            outs["k_p"].append(k_p); outs["v_p"].append(v_p)
            outs["k_s"].append(k_s); outs["v_s"].append(v_s)
        y_p = _mlp(y_p, norm_mlp[i], w_up, w_down, i)
        y_s = _mlp(y_s, norm_mlp[i], w_up, w_down, i)
    st = lambda key: jnp.stack(outs[key])
    return (y_p.reshape(n_p, t_p, d), y_s.reshape(n_s, t_s, d),
            st("re_p"), st("im_p"), st("re_s"), st("im_s"), st("cv_p"), st("cv_s"),
            st("k_p"), st("v_p"), st("k_s"), st("v_s"))
```

```python
import functools
import math

import jax
import jax.numpy as jnp
from jax import lax
from jax.experimental import pallas as pl
from jax.experimental.pallas import tpu as pltpu

F32 = jnp.float32
BF16 = jnp.bfloat16
EPS = 1e-6
LANES = 128
SUBLANES = 8
VMEM_LIMIT_BYTES = 56 * 1024 * 1024
VMEM_BUDGET_BYTES = 44 * 1024 * 1024
S5_CHUNK = 16
HIGHEST = lax.Precision.HIGHEST


def _cparams(n_axes):
    return pltpu.CompilerParams(dimension_semantics=("arbitrary",) * n_axes,
                                vmem_limit_bytes=VMEM_LIMIT_BYTES)


def _rms(x, g):
    ms = jnp.mean(x * x, axis=-1, keepdims=True)
    return x * lax.rsqrt(ms + EPS) * g


def _iota(shape, dim):
    return lax.broadcasted_iota(jnp.int32, shape, dim)


def _mod(x, n):
    return x & (n - 1) if n & (n - 1) == 0 else lax.rem(x, n)


def _div(x, n):
    return x >> (n.bit_length() - 1) if n & (n - 1) == 0 else lax.div(x, n)


def _plan_tiles(m, kdim, n_out, row_bytes_per_row, n_w, tile_itemsizes):
    for tm, tn, n_buf in ((1024, 1024, 2), (1024, 512, 2), (1024, 1024, 1), (1024, 512, 1),
                          (512, 1024, 2), (512, 512, 2), (512, 512, 1)):
        tm, tn = min(tm, m), min(tn, n_out)
        stream = n_w * kdim * tn * 2 + tm * tn * sum(tile_itemsizes)
        fixed = tm * kdim * 2 + 2 * n_w * tm * tn * 4
        if n_buf * tm * row_bytes_per_row + 2 * stream + fixed <= VMEM_BUDGET_BYTES:
            break
    return tm, tn, n_buf


def _fused_matmul(name, m, kdim, n_out, row_in, const_in, weights, tile_in, out_dtypes,
                  prologue, epilogue):
    n_row, n_const, n_w, n_tile, n_o = (len(row_in), len(const_in), len(weights),
                                        len(tile_in), len(out_dtypes))
    size = lambda dt: jnp.dtype(dt).itemsize
    tm, tn, n_buf = _plan_tiles(
        m, kdim, n_out, sum(a.shape[1] * size(a.dtype) for a, kind in row_in if kind == "rows"),
        n_w, [size(a.dtype) for a in tile_in] + [size(dt) for dt in out_dtypes])
    row_mode = pl.Buffered(1) if n_buf == 1 else None
    per = tm // SUBLANES
    in_specs, args = [], []
    for arr, kind in row_in:
        if kind == "rows":
            spec = pl.BlockSpec((tm, arr.shape[1]), lambda i, j: (i, 0), pipeline_mode=row_mode)
        else:
            spec = pl.BlockSpec((SUBLANES, arr.shape[1]),
                                lambda i, j: (jnp.maximum(i * per - 1, 0), 0))
        in_specs.append(spec)
        args.append(arr)
    for arr in const_in:
        in_specs.append(pl.BlockSpec(arr.shape, lambda i, j, nd=arr.ndim: (0,) * nd))
        args.append(arr)
    for arr, layer, col in weights:
        in_specs.append(pl.BlockSpec((None, kdim, tn),
                                     lambda i, j, layer=layer, off=col // tn: (layer, 0, j + off)))
        args.append(arr)
    for arr in tile_in:
        in_specs.append(pl.BlockSpec((tm, tn), lambda i, j: (i, j)))
        args.append(arr)
    out_shape = [jax.ShapeDtypeStruct((m, n_out), dt) for dt in out_dtypes]
    out_specs = [pl.BlockSpec((tm, tn), lambda i, j: (i, j)) for _ in out_dtypes]

    def body(*refs):
        p = 0
        row_refs = refs[p:p + n_row]; p += n_row
        const_refs = refs[p:p + n_const]; p += n_const
        w_refs = refs[p:p + n_w]; p += n_w
        tile_refs = refs[p:p + n_tile]; p += n_tile
        out_refs = refs[p:p + n_o]; p += n_o
        lhs_ref = refs[p]

        @pl.when(pl.program_id(1) == 0)
        def _():
            lhs_ref[...] = prologue(row_refs, const_refs, tm).astype(BF16)

        lhs = lhs_ref[...]
        accs = [jnp.dot(lhs, w[...], preferred_element_type=F32) for w in w_refs]
        for o_ref, val in zip(out_refs, epilogue(accs, tile_refs, const_refs)):
            o_ref[...] = val.astype(o_ref.dtype)

    return pl.pallas_call(
        body, grid=(m // tm, n_out // tn), in_specs=in_specs, out_specs=out_specs,
        out_shape=out_shape, scratch_shapes=[pltpu.VMEM((tm, kdim), BF16)],
        compiler_params=_cparams(2), name=name)(*args)


def _norm_prologue(row_refs, const_refs, tm):
    return _rms(row_refs[0][...], const_refs[0][...])


def _ident_prologue(row_refs, const_refs, tm):
    return row_refs[0][...]


def _head_norm(acc, gain, scale):
    segs = []
    for h in range(acc.shape[1] // LANES):
        seg = acc[:, h * LANES:(h + 1) * LANES]
        ms = jnp.mean(seg * seg, axis=-1, keepdims=True)
        y = seg * lax.rsqrt(ms + EPS) * gain
        segs.append(y * scale if scale != 1.0 else y)
    return segs[0] if len(segs) == 1 else jnp.concatenate(segs, axis=-1)


def _norm_proj(name, x, gain, w, layer, col, n_out, out_dtypes, epilogue, extra_const=()):
    m, kdim = x.shape
    return _fused_matmul(name, m, kdim, n_out, [(x, "rows")],
                         [gain.reshape(1, kdim)] + list(extra_const), [(w, layer, col)], [],
                         out_dtypes, _norm_prologue, epilogue)


def _mlp_body(x_ref, g_ref, wu_ref, wd_ref, o_ref, xn_ref, acc_ref):
    f = pl.program_id(1)

    @pl.when(f == 0)
    def _():
        xn_ref[...] = _rms(x_ref[...], g_ref[...]).astype(BF16)
        acc_ref[...] = jnp.zeros_like(acc_ref)

    h = jnp.maximum(jnp.dot(xn_ref[...], wu_ref[...], preferred_element_type=F32), 0.0)
    acc_ref[...] += jnp.dot((h * h).astype(BF16), wd_ref[...], preferred_element_type=F32)

    @pl.when(f == pl.num_programs(1) - 1)
    def _():
        o_ref[...] = x_ref[...] + acc_ref[...]


def _mlp(x, gain, w_up, w_down, layer):
    m, d = x.shape
    ff = w_up.shape[2]
    tm = min(512, m)
    tf = min(512 if m > 512 else 2048, ff)
    return pl.pallas_call(
        _mlp_body, grid=(m // tm, ff // tf),
        in_specs=[pl.BlockSpec((tm, d), lambda i, f: (i, 0)),
                  pl.BlockSpec((1, d), lambda i, f: (0, 0)),
                  pl.BlockSpec((None, d, tf), lambda i, f: (layer, 0, f)),
                  pl.BlockSpec((None, tf, d), lambda i, f: (layer, f, 0))],
        out_specs=pl.BlockSpec((tm, d), lambda i, f: (i, 0)),
        out_shape=jax.ShapeDtypeStruct((m, d), F32),
        scratch_shapes=[pltpu.VMEM((tm, d), BF16), pltpu.VMEM((tm, d), F32)],
        compiler_params=_cparams(2), name="mlp")(x, gain.reshape(1, d), w_up, w_down)


def _norm_body(x_ref, g_ref, o_ref):
    o_ref[...] = _rms(x_ref[...], g_ref[...])


def _norm(x, gain):
    m, d = x.shape
    tm = min(512, m)
    return pl.pallas_call(
        _norm_body, grid=(m // tm,),
        in_specs=[pl.BlockSpec((tm, d), lambda i: (i, 0)), pl.BlockSpec((1, d), lambda i: (0, 0))],
        out_specs=pl.BlockSpec((tm, d), lambda i: (i, 0)),
        out_shape=jax.ShapeDtypeStruct((m, d), F32),
        compiler_params=_cparams(1), name="rmsnorm")(x, gain.reshape(1, d))


def _complex_scale(acos, asin, h):
    return acos * h + asin * pltpu.roll(h, h.shape[1] // 2, axis=1)


def _gelu(y):
    return y * (0.5 * (1.0 + jnp.tanh(math.sqrt(2.0 / math.pi) * (y + 0.044715 * (y * y * y)))))


def _s5_step_body(u_ref, m_ref, bc_ref, cp_ref, d_ref, acos_ref, asin_ref, h0_ref, z_ref, hl_ref,
                  *, gt):
    def one_group(g, carry):
        u = u_ref[g]
        ub = u.astype(BF16)
        h_prev = h0_ref[g]
        hl_ref[g] = (jnp.dot(ub, bc_ref[g], preferred_element_type=F32)
                     + _complex_scale(acos_ref[g][0:1], asin_ref[g][0:1], h_prev))
        y = (jnp.dot(ub, m_ref[g], preferred_element_type=F32)
             + jnp.dot(h_prev.astype(BF16), cp_ref[g], preferred_element_type=F32)
             + u * d_ref[g])
        z_ref[g] = _gelu(y).astype(z_ref.dtype)
        return carry

    lax.fori_loop(0, gt, one_group, 0)


def _s5_seq_body(x_ref, bd_ref, bct_ref, cpt_ref, acos_ref, asin_ref, z_ref, hl_ref,
                 m8_ref, bc8_ref, cp8_ref, h_ref, *, chunk, n_chunk, n_seq, c):
    gl = LANES // c
    rows = n_seq * n_chunk
    blk = lambda i: slice(i * LANES, (i + 1) * LANES)
    grp_of_lane = _div(_iota((c, LANES), 1), c)
    for tau in range(chunk):
        kt = bd_ref[tau]
        m8_ref[blk(0), blk(tau)] = jnp.concatenate(
            [jnp.where(grp_of_lane == g, kt, 0.0) for g in range(gl)], axis=0).astype(BF16)
    for s in range(1, chunk):
        for t in range(s, chunk):
            m8_ref[blk(s), blk(t)] = m8_ref[blk(0), blk(t - s)]
        if s % 2:
            m8_ref[blk(s), blk(s - 1)] = jnp.zeros((LANES, LANES), BF16)
    row_grp = _div(_iota((LANES, LANES), 0), c)
    lane_grp = _div(_iota((LANES, LANES), 1), c)
    for s in range(chunk):
        for g in range(gl):
            bc8_ref[blk(s), blk(g)] = jnp.where(row_grp == g, bct_ref[s], 0.0).astype(BF16)
            cp8_ref[blk(g), blk(s)] = jnp.where(lane_grp == g, cpt_ref[s], 0.0).astype(BF16)

    u2 = jnp.concatenate([x_ref[pl.ds(s, rows, stride=chunk), :].astype(BF16)
                          for s in range(chunk)], axis=1)
    h = jnp.dot(u2, bc8_ref[...], preferred_element_type=F32)
    kidx = _iota((rows, 1), 0) & (n_chunk - 1)
    swap = lambda v: jnp.concatenate(
        [pltpu.roll(v[:, blk(g)], LANES // 2, axis=1) for g in range(gl)], axis=1)
    shift, si = 1, 0
    while shift < n_chunk:
        sh = jnp.where(kidx >= shift, pltpu.roll(h, shift, axis=0), 0.0)
        h = h + acos_ref[si:si + 1, :] * sh + asin_ref[si:si + 1, :] * swap(sh)
        shift, si = shift * 2, si + 1
    h_ref[...] = h
    h_prev = jnp.where(kidx >= 1, pltpu.roll(h, 1, axis=0), 0.0).astype(BF16)
    for t in range(0, chunk, 2):
        cols = slice(t * LANES, (t + 2) * LANES)
        y = (jnp.dot(u2[:, :(t + 2) * LANES], m8_ref[:(t + 2) * LANES, cols],
                     preferred_element_type=F32)
             + jnp.dot(h_prev, cp8_ref[:, cols], preferred_element_type=F32))
        z = _gelu(y)
        z_ref[pl.ds(t, rows, stride=chunk), :] = z[:, :LANES]
        z_ref[pl.ds(t + 1, rows, stride=chunk), :] = z[:, LANES:]
    for n in range(n_seq):
        hl_ref[pl.ds(n, 1), :] = h_ref[pl.ds((n + 1) * n_chunk - 1, 1), :]


def _s5_tables(lam_re, lam_im, log_dt, b_re, b_im, c_re, c_im, d, chunk, n_chunk, per_tile):
    g, p = lam_re.shape
    c = b_re.shape[2]
    dt = jnp.exp(log_dt)[:, None]
    xr, xi = lam_re * dt, lam_im * dt

    def a_pow(ks):
        kk = jnp.asarray(ks, F32)[None, :, None]
        mag = jnp.exp(xr[:, None, :] * kk)
        return mag * jnp.cos(xi[:, None, :] * kk), mag * jnp.sin(xi[:, None, :] * kk)

    pr, pi = a_pow(jnp.arange(chunk + 1))
    ar, ai = pr[:, 1], pi[:, 1]
    den = lam_re * lam_re + lam_im * lam_im
    qr = ((ar - 1.0) * lam_re + ai * lam_im) / den
    qi = (ai * lam_re - (ar - 1.0) * lam_im) / den
    bb_re = qr[..., None] * b_re - qi[..., None] * b_im
    bb_im = qr[..., None] * b_im + qi[..., None] * b_re
    w_re = c_re[:, None] * pr[:, :chunk, None] - c_im[:, None] * pi[:, :chunk, None]
    w_im = c_re[:, None] * pi[:, :chunk, None] + c_im[:, None] * pr[:, :chunk, None]
    kt = jnp.einsum("gtoq,gqi->gtoi", jnp.concatenate([w_re, -w_im], axis=-1),
                    jnp.concatenate([bb_re, bb_im], axis=1), precision=HIGHEST)
    pair = lambda x, y: jnp.concatenate([x, y], axis=-1)
    rr, ri = pr[:, :chunk][:, ::-1], pi[:, :chunk][:, ::-1]
    bt_re, bt_im = bb_re.transpose(0, 2, 1), bb_im.transpose(0, 2, 1)
    bc = (pair(rr, rr)[:, :, None] * pair(bt_re, bt_im)[:, None]
          + pair(-ri, ri)[:, :, None] * pair(bt_im, bt_re)[:, None])
    shifts = [1]
    while shifts[-1] * 2 < n_chunk:
        shifts.append(shifts[-1] * 2)
    while len(shifts) < SUBLANES:
        shifts.append(shifts[-1])
    sr, si = a_pow(chunk * jnp.asarray(shifts, F32))
    acos, asin = pair(sr, sr), pair(-si, si)
    if not per_tile:
        kpad = jnp.concatenate([kt, jnp.zeros((g, 1, c, c), F32)], axis=1)
        s_idx = jnp.arange(chunk)[:, None]
        t_idx = jnp.arange(chunk)[None, :]
        tau = jnp.where(t_idx >= s_idx, t_idx - s_idx, chunk)
        mmat = kpad[:, tau].transpose(0, 1, 4, 2, 3).reshape(g, chunk * c, chunk * c)
        cr, ci = c_re.transpose(0, 2, 1)[:, :, None], c_im.transpose(0, 2, 1)[:, :, None]
        nr, ni = pr[:, 1:].transpose(0, 2, 1)[..., None], pi[:, 1:].transpose(0, 2, 1)[..., None]
        cp = jnp.concatenate([cr * nr - ci * ni, -(cr * ni + ci * nr)], axis=1)
        dtile = jnp.tile(d.reshape(g, 1, c), (1, chunk, 1)).reshape(g, 1, chunk * c)
        return (mmat.astype(BF16), bc.reshape(g, chunk * c, 2 * p).astype(BF16),
                cp.reshape(g, 2 * p, chunk * c).astype(BF16), dtile, acos, asin)
    gl = LANES // c
    nt = g // gl
    p2 = 2 * p
    kd = kt.at[:, 0].add(d.reshape(g, c)[:, :, None] * jnp.eye(c, dtype=F32))
    bd = kd.reshape(nt, gl, chunk, c, c).transpose(0, 2, 4, 1, 3).reshape(nt, chunk, c, LANES)
    bct = bc.reshape(nt, gl, chunk, c, p2).transpose(0, 2, 1, 3, 4).reshape(nt, chunk, LANES, p2)
    on_lanes = lambda x: x.reshape(nt, gl, c, p).transpose(0, 3, 1, 2).reshape(nt, 1, p, LANES)
    spread = lambda x: jnp.repeat(
        x[:, 1:].reshape(nt, gl, chunk, p).transpose(0, 2, 3, 1), c, axis=-1)
    cr, ci, nr, ni = on_lanes(c_re), on_lanes(c_im), spread(pr), spread(pi)
    cpt = jnp.concatenate([cr * nr - ci * ni, -(cr * ni + ci * nr)], axis=2)
    tile = lambda a: a.reshape(nt, gl, SUBLANES, p2).transpose(0, 2, 1, 3).reshape(
        nt, SUBLANES, gl * p2)
    return bd, bct, cpt, tile(acos), tile(asin)


def _s5_step(hn, n_seq, t_len, tabs, h0):
    mmat, bc, cp, dt, acos, asin = tabs
    g, lc = mmat.shape[:2]
    p2 = bc.shape[2]
    c = lc // t_len
    u_r = hn.reshape(n_seq, t_len, g, c).transpose(2, 0, 1, 3).reshape(g, n_seq, lc)
    gt = min(8, g)
    grp = lambda shape: pl.BlockSpec((gt,) + shape, lambda i: (i, 0, 0))
    z_r, h_last = pl.pallas_call(
        functools.partial(_s5_step_body, gt=gt), grid=(g // gt,),
        in_specs=[grp((n_seq, lc)), grp((lc, lc)), grp((lc, p2)), grp((p2, lc)), grp((1, lc)),
                  grp((SUBLANES, p2)), grp((SUBLANES, p2)), grp((n_seq, p2))],
        out_specs=[grp((n_seq, lc)), grp((n_seq, p2))],
        out_shape=[jax.ShapeDtypeStruct((g, n_seq, lc), BF16),
                   jax.ShapeDtypeStruct((g, n_seq, p2), F32)],
        compiler_params=_cparams(1), name="s5_step")(u_r, mmat, bc, cp, dt, acos, asin, h0)
    z = z_r.reshape(g, n_seq, t_len, c).transpose(1, 2, 0, 3).reshape(n_seq * t_len, g * c)
    return z, h_last.transpose(1, 0, 2)


def _s5_seq(hn, n_seq, t_len, tabs, chunk):
    bd, bct, cpt, acos, asin = tabs
    m, d = hn.shape
    nt = bd.shape[0]
    p2 = bct.shape[3]
    gl = acos.shape[2] // p2
    c = LANES // gl
    n_chunk = t_len // chunk
    assert p2 == LANES and d == nt * LANES
    assert chunk % 2 == 0
    tab = lambda a: pl.BlockSpec((None,) + a.shape[1:], lambda j: (j,) + (0,) * (a.ndim - 1))
    z, h_last = pl.pallas_call(
        functools.partial(_s5_seq_body, chunk=chunk, n_chunk=n_chunk, n_seq=n_seq, c=c),
        grid=(nt,),
        in_specs=[pl.BlockSpec((m, LANES), lambda j: (0, j)),
                  tab(bd), tab(bct), tab(cpt), tab(acos), tab(asin)],
        out_specs=[pl.BlockSpec((m, LANES), lambda j: (0, j)),
                   pl.BlockSpec((None, n_seq, gl * p2), lambda j: (j, 0, 0))],
        out_shape=[jax.ShapeDtypeStruct((m, d), F32),
                   jax.ShapeDtypeStruct((nt, n_seq, gl * p2), F32)],
        scratch_shapes=[pltpu.VMEM((chunk * LANES, chunk * LANES), BF16),
                        pltpu.VMEM((chunk * LANES, gl * p2), BF16),
                        pltpu.VMEM((gl * p2, chunk * LANES), BF16),
                        pltpu.VMEM((n_seq * n_chunk, gl * p2), F32)],
        compiler_params=_cparams(1), name="s5_seq")(hn, bd, bct, cpt, acos, asin)
    h_last = h_last.reshape(nt, n_seq, gl, p2).transpose(1, 0, 2, 3).reshape(n_seq, nt * gl, p2)
    return z, h_last


def _glu_epilogue(accs, tile_refs, const_refs):
    a, gate = accs
    return [tile_refs[0][...] + a * jax.nn.sigmoid(gate)]


def _s5_layer(y, gain, n_seq, t_len, tabs, h0, w_glu, layer, chunk):
    m, d = y.shape
    if h0 is None:
        z, h_last = _s5_seq(_norm(y, gain), n_seq, t_len, tabs, chunk)
    else:
        z, h_last = _s5_step(_norm(y, gain), n_seq, t_len, tabs, h0)
    (out,) = _fused_matmul("s5_glu", m, d, d, [(z, "rows")], [],
                           [(w_glu, layer, 0), (w_glu, layer, d)], [y], [F32],
                           _ident_prologue, _glu_epilogue)
    p = h_last.shape[2] // 2
    return out, h_last[..., :p], h_last[..., p:]


def _conv_in_epilogue(accs, tile_refs, const_refs):
    gate_b, gate_c, v = accs
    return [gate_b, gate_c * v]


def _conv_prologue(row_refs, const_refs, tm, *, t_len, has_buf):
    w = const_refs[0][...]
    bg, cv = row_refs[0][...], row_refs[1][...]
    t = _mod(pl.program_id(0) * tm + _iota((tm, 1), 0), t_len)
    if has_buf:
        b0, b1 = row_refs[2][...], row_refs[3][...]
        r1 = jnp.where(t >= 1, pltpu.roll(cv, 1, axis=0), b1)
        r2 = jnp.where(t >= 2, pltpu.roll(cv, 2, axis=0), jnp.where(t == 1, b1, b0))
    else:
        full = jnp.concatenate([row_refs[2][...], cv], axis=0)
        r1 = jnp.where(t >= 1, pltpu.roll(full, 1, axis=0)[SUBLANES:], 0.0)
        r2 = jnp.where(t >= 2, pltpu.roll(full, 2, axis=0)[SUBLANES:], 0.0)
    return bg * (w[2:3] * cv + w[1:2] * r1 + w[0:1] * r2)


def _resid_epilogue(accs, tile_refs, const_refs):
    return [tile_refs[0][...] + accs[0]]


def _conv_layer(y, gain, n_seq, t_len, buf, w_in, w_dw, w_out, layer):
    m, d = y.shape
    assert t_len >= 2
    bg, cv = _fused_matmul("conv_in", m, d, d, [(y, "rows")], [gain.reshape(1, d)],
                           [(w_in, layer, 0), (w_in, layer, d), (w_in, layer, 2 * d)],
                           [], [F32, F32], _norm_prologue, _conv_in_epilogue)
    rows = [(bg, "rows"), (cv, "rows")]
    if buf is None:
        rows.append((cv, "prev"))
    else:
        assert m <= 512
        rows += [(jnp.repeat(buf[:, k], t_len, axis=0), "rows") for k in range(2)]
    (out,) = _fused_matmul(
        "conv_out", m, d, d, rows, [w_dw[layer]], [(w_out, layer, 0)], [y], [F32],
        functools.partial(_conv_prologue, t_len=t_len, has_buf=buf is not None),
        _resid_epilogue)
    return out, cv.reshape(n_seq, t_len, d)[:, t_len - 2:]


LOG2_E = math.log2(math.e)


def _sb_logs(z, mask):
    sp = jnp.log2(1.0 + jnp.exp2(-jnp.abs(z)))
    log_b = jnp.minimum(z, 0.0) - sp
    log_1m = log_b - z
    if mask is not None:
        log_1m = jnp.where(mask, log_1m, 0.0)
    return log_b, log_1m


def _sb_suffix(log_1m, upper):
    hi = log_1m.astype(BF16)
    lo = (log_1m - hi.astype(F32)).astype(BF16)
    return (jnp.dot(hi, upper, preferred_element_type=F32)
            + jnp.dot(lo, upper, preferred_element_type=F32))


def _sb_finish(log_b, log_1m, suffix, mask, run):
    a = jnp.exp2(log_b + suffix + run)
    if mask is not None:
        a = jnp.where(mask, a, 0.0)
    return a, run + jnp.sum(log_1m, axis=-1, keepdims=True)


def _sb_weights(z, mask, upper, run):
    log_b, log_1m = _sb_logs(z, mask)
    return _sb_finish(log_b, log_1m, _sb_suffix(log_1m, upper), mask, run)


def _later_key_matrix(tk):
    return (_iota((tk, tk), 0) > _iota((tk, tk), 1)).astype(BF16)


def _qk(q, k_blk):
    return lax.dot_general(q, k_blk, (((1,), (1,)), ((), ())), preferred_element_type=F32)


def _attn_body(bias_ref, q_ref, k_ref, v_ref, o_ref, *, tq, tr, dh, hps):
    hg = pl.program_id(1)
    qi = pl.program_id(2)
    upper = _later_key_matrix(tq)
    units = [(hh, r0) for hh in range(hps) for r0 in range(0, tq, tr)]

    def tile(kb, carries, masked):
        start = pl.multiple_of(kb * tq, tq)
        lanes = lambda hh: slice(hh * dh, (hh + 1) * dh)
        masks = [(_iota((tr, tq), 1) < _iota((tr, tq), 0) + r0) if masked else None
                 for _, r0 in units]
        zs = [_qk(q_ref[r0:r0 + tr, lanes(hh)], k_ref[pl.ds(start, tq), lanes(hh)])
              + bias_ref[hg * hps + hh] for hh, r0 in units]
        logs = [_sb_logs(z, m) for z, m in zip(zs, masks)]
        sufs = [_sb_suffix(l1m, upper) for _, l1m in logs]
        fins = [_sb_finish(lb, l1m, suf, m, run)
                for (lb, l1m), suf, m, (run, _) in zip(logs, sufs, masks, carries)]
        accs = [acc + jnp.dot(a.astype(BF16), v_ref[pl.ds(start, tq), lanes(hh)],
                              preferred_element_type=F32)
                for (a, _), (_, acc), (hh, _) in zip(fins, carries, units)]
        return tuple((run, acc) for (_, run), acc in zip(fins, accs))

    carries = tuple((jnp.zeros((tr, 1), F32), jnp.zeros((tr, dh), F32)) for _ in units)
    carries = tile(qi, carries, True)
    carries = lax.fori_loop(0, qi, lambda it, c: tile(qi - 1 - it, c, False), carries)
    for (hh, r0), (_, acc) in zip(units, carries):
        o_ref[r0:r0 + tr, hh * dh:(hh + 1) * dh] = acc.astype(o_ref.dtype)


def _attn_prompt(q, k, v, bias, n_seq, t_len, n_heads):
    m, d = q.shape
    dh = d // n_heads
    tq = min(256, t_len)
    hps = next(h for h in (4, 2, 1) if n_heads % h == 0)
    qb = t_len // tq
    grid_spec = pltpu.PrefetchScalarGridSpec(
        num_scalar_prefetch=1, grid=(n_seq, n_heads // hps, qb),
        in_specs=[pl.BlockSpec((tq, hps * dh), lambda n, h, i, b: (n * qb + i, h)),
                  pl.BlockSpec((t_len, hps * dh), lambda n, h, i, b: (n, h)),
                  pl.BlockSpec((t_len, hps * dh), lambda n, h, i, b: (n, h))],
        out_specs=pl.BlockSpec((tq, hps * dh), lambda n, h, i, b: (n * qb + i, h)))
    return pl.pallas_call(
        functools.partial(_attn_body, tq=tq, tr=min(128, tq), dh=dh, hps=hps),
        grid_spec=grid_spec,
        out_shape=jax.ShapeDtypeStruct((m, d), BF16),
        compiler_params=_cparams(3), name="sb_attn_prompt")(bias, q, k, v)


def _decode_body(pt_ref, q_ref, bias_ref, e_ref, et_ref, hm_ref, hm32_ref, *refs, n_heads, ppt):
    kn_refs, vn_refs = refs[:ppt], refs[ppt:2 * ppt]
    kc_refs, vc_refs = refs[2 * ppt:3 * ppt], refs[3 * ppt:4 * ppt]
    o_ref, run_ref, acc_ref = refs[4 * ppt:]
    step = pl.program_id(1)
    page = vn_refs[0].shape[0] // n_heads
    rows = q_ref.shape[0]
    tk = ppt * page

    def tile(k_refs, v_refs, mask, run, acc):
        parts = []
        for slot in range(ppt):
            z_rows = _qk(q_ref[...], k_refs[slot][...].astype(BF16)) * hm32_ref[...]
            hi = z_rows.astype(BF16)
            parts += [hi, (z_rows - hi.astype(F32)).astype(BF16)]
        zc = jnp.dot(jnp.concatenate(parts, axis=0), et_ref[...], preferred_element_type=F32)
        zs = [zc[2 * s * rows:(2 * s + 1) * rows] + zc[(2 * s + 1) * rows:(2 * s + 2) * rows]
              for s in range(ppt)]
        z = (zs[0] if ppt == 1 else jnp.concatenate(zs, axis=1)) + bias_ref[...]
        a, run = _sb_weights(z, mask, _later_key_matrix(tk), run)
        a = a.astype(BF16)
        a_st = jnp.concatenate([a[:, s * page:(s + 1) * page] for s in range(ppt)], axis=0)
        a_rows = jnp.dot(a_st, e_ref[...], preferred_element_type=F32).astype(BF16)
        for slot in range(ppt):
            acc = acc + jnp.dot(a_rows[slot * rows:(slot + 1) * rows] * hm_ref[...],
                                v_refs[slot][...].astype(BF16), preferred_element_type=F32)
        run_ref[...] = jnp.broadcast_to(run, run_ref.shape)
        acc_ref[...] = acc

    @pl.when(step == 0)
    def _():
        mask = _iota((rows, tk), 1) < _div(_iota((rows, tk), 0), n_heads)
        tile(kn_refs, vn_refs, mask, jnp.zeros((rows, 1), F32), jnp.zeros(acc_ref.shape, F32))

    @pl.when(step > 0)
    def _():
        tile(kc_refs, vc_refs, None, run_ref[:, 0:1], acc_ref[...])

    @pl.when(step == pl.num_programs(1) - 1)
    def _():
        o_ref[...] = acc_ref[...]


def _attn_decode(q, k_new, v_new, bias, cache_k, cache_v, layer, page_table, n_seq, t_len,
                 n_heads):
    d = q.shape[1]
    dh = d // n_heads
    n_layers, n_pool, page = cache_k.shape[:3]
    n_pages = page_table.shape[1]
    ppt = next(p for p in (4, 2, 1) if n_pages % p == 0)
    n_tiles = n_pages // ppt
    rows = t_len * n_heads
    assert n_heads % SUBLANES == 0 and t_len <= page
    q_rows = q.reshape(n_seq, rows, dh)
    bias_rows = jnp.tile(bias, t_len).reshape(rows, 1)
    key_of_row = jnp.arange(page * n_heads) // n_heads
    expand = (key_of_row[None, :] == jnp.arange(page)[:, None]).astype(BF16)
    head_of_row = jnp.arange(page * n_heads) % n_heads
    head_mask = head_of_row[None, :] == (jnp.arange(rows) % n_heads)[:, None]
    pad_t = ((0, 0), (0, ppt * page - t_len), (0, 0), (0, 0))
    as_rows = lambda x: jnp.pad(x.reshape(n_seq, t_len, n_heads, dh), pad_t).reshape(
        n_seq, ppt, page * n_heads, dh)
    kn, vn = as_rows(k_new), as_rows(v_new)
    kc = cache_k.reshape(n_layers, n_pool, page * n_heads, dh)
    vc = cache_v.reshape(n_layers, n_pool, page * n_heads, dh)

    def phys(n, s, pt, slot):
        return pt[n, (n_tiles - jnp.maximum(s, 1)) * ppt + slot]

    const = lambda shape: pl.BlockSpec(shape, lambda n, s, pt: (0,) * len(shape))
    in_specs = [pl.BlockSpec((None, rows, dh), lambda n, s, pt: (n, 0, 0)),
                const((rows, 1)), const(expand.shape), const(expand.shape[::-1]),
                const(head_mask.shape), const(head_mask.shape)]
    args = [q_rows, bias_rows, expand, expand.T, head_mask.astype(BF16), head_mask.astype(F32)]
    blk = (None, None, page * n_heads, dh)
    for arr in (kn, vn):
        for slot in range(ppt):
            in_specs.append(pl.BlockSpec(blk, lambda n, s, pt, slot=slot: (n, slot, 0, 0)))
            args.append(arr)
    for arr in (kc, vc):
        for slot in range(ppt):
            in_specs.append(pl.BlockSpec(
                blk, lambda n, s, pt, slot=slot: (layer, phys(n, s, pt, slot), 0, 0)))
            args.append(arr)
    grid_spec = pltpu.PrefetchScalarGridSpec(
        num_scalar_prefetch=1, grid=(n_seq, n_tiles + 1), in_specs=in_specs,
        out_specs=pl.BlockSpec((None, rows, dh), lambda n, s, pt: (n, 0, 0)),
        scratch_shapes=[pltpu.VMEM((rows, LANES), F32), pltpu.VMEM((rows, dh), F32)])
    o = pl.pallas_call(
        functools.partial(_decode_body, n_heads=n_heads, ppt=ppt), grid_spec=grid_spec,
        out_shape=jax.ShapeDtypeStruct((n_seq, rows, dh), F32),
        compiler_params=_cparams(2), name="sb_attn_decode")(page_table, *args)
    return o.reshape(n_seq * t_len, d).astype(BF16)


def _attn_layer(y, gain, n_seq, t_len, n_heads, w_qkv, q_gain, k_gain, sb_bias, w_o, layer,
                cache=None):
    m, d = y.shape
    dh = d // n_heads
    scale = dh ** -0.5 * LOG2_E
    sb_bias = sb_bias * LOG2_E
    qn = lambda accs, t, c: [_head_norm(accs[0], c[1][...], scale)]
    kn = lambda accs, t, c: [_head_norm(accs[0], c[1][...], 1.0)] * 2
    vn = lambda accs, t, c: [accs[0]] * 2
    (q,) = _norm_proj("attn_q", y, gain, w_qkv, layer, 0, d, [BF16], qn, [q_gain.reshape(1, dh)])
    k32, kb = _norm_proj("attn_k", y, gain, w_qkv, layer, d, d, [F32, BF16], kn,
                         [k_gain.reshape(1, dh)])
    v32, vb = _norm_proj("attn_v", y, gain, w_qkv, layer, 2 * d, d, [F32, BF16], vn)
    if cache is None:
        o = _attn_prompt(q, kb, vb, sb_bias, n_seq, t_len, n_heads)
    else:
        cache_k, cache_v, page_table = cache
        o = _attn_decode(q, k32, v32, sb_bias, cache_k, cache_v, layer, page_table, n_seq, t_len,
                         n_heads)
    (out,) = _fused_matmul("attn_out", m, d, d, [(o, "rows")], [], [(w_o, layer, 0)], [y], [F32],
                           _ident_prologue, _resid_epilogue)
    shape = (n_seq, t_len, n_heads, dh)
    return out, k32.reshape(shape), v32.reshape(shape)


def kernel(x_prompt, x_sample, state_ssm_re, state_ssm_im, state_conv, cache_k, cache_v, page_table, norm_mix, norm_mlp, ssm_lambda_re, ssm_lambda_im, ssm_log_dt, ssm_b_re, ssm_b_im, ssm_c_re, ssm_c_im, ssm_d, ssm_w_glu, conv_w_in, conv_w_dw, conv_w_out, attn_w_qkv, attn_q_norm, attn_k_norm, attn_sb_bias, attn_w_o, mlp_w_up, mlp_w_down):
    n_p, t_p, d = x_prompt.shape
    n_s, t_s, _ = x_sample.shape
    depth = norm_mix.shape[0]
    n_heads = attn_sb_bias.shape[1]
    assert conv_w_dw.shape[1] == 3 and d // n_heads == LANES
    chunk_p = min(S5_CHUNK, t_p)
    assert t_p % chunk_p == 0 and (t_p // chunk_p) & (t_p // chunk_p - 1) == 0

    w_glu, w_in, w_out = (w.astype(BF16) for w in (ssm_w_glu, conv_w_in, conv_w_out))
    w_qkv, w_o = attn_w_qkv.astype(BF16), attn_w_o.astype(BF16)
    w_up, w_down = mlp_w_up.astype(BF16), mlp_w_down.astype(BF16)

    y_p = x_prompt.reshape(n_p * t_p, d)
    y_s = x_sample.reshape(n_s * t_s, d)
    outs = {k: [] for k in ("re_p", "im_p", "re_s", "im_s", "cv_p", "cv_s", "k_p", "v_p", "k_s", "v_s")}
    for i in range(depth):
        kind, j = i % 3, i // 3
        if kind == 0:
            ssm = (ssm_lambda_re[j], ssm_lambda_im[j], ssm_log_dt[j], ssm_b_re[j], ssm_b_im[j],
                   ssm_c_re[j], ssm_c_im[j], ssm_d[j])
            tabs_p = _s5_tables(*ssm, chunk_p, t_p // chunk_p, True)
            tabs_s = _s5_tables(*ssm, t_s, 1, False)
            h0 = jnp.concatenate([state_ssm_re[j], state_ssm_im[j]], axis=-1).transpose(1, 0, 2)
            y_p, re_p, im_p = _s5_layer(y_p, norm_mix[i], n_p, t_p, tabs_p, None, w_glu, j, chunk_p)
            y_s, re_s, im_s = _s5_layer(y_s, norm_mix[i], n_s, t_s, tabs_s, h0, w_glu, j, t_s)
            outs["re_p"].append(re_p); outs["im_p"].append(im_p)
            outs["re_s"].append(re_s); outs["im_s"].append(im_s)
        elif kind == 1:
            y_p, cv_p = _conv_layer(y_p, norm_mix[i], n_p, t_p, None, w_in, conv_w_dw, w_out, j)
            y_s, cv_s = _conv_layer(y_s, norm_mix[i], n_s, t_s, state_conv[j], w_in, conv_w_dw,
                                    w_out, j)
            outs["cv_p"].append(cv_p); outs["cv_s"].append(cv_s)
        else:
            attn = (w_qkv, attn_q_norm[j], attn_k_norm[j], attn_sb_bias[j], w_o, j)
            y_p, k_p, v_p = _attn_layer(y_p, norm_mix[i], n_p, t_p, n_heads, *attn)
            y_s, k_s, v_s = _attn_layer(y_s, norm_mix[i], n_s, t_s, n_heads, *attn,
                                        cache=(cache_k, cache_v, page_table))
            outs["k_p"].append(k_p); outs["v_p"].append(v_p)
            outs["k_s"].append(k_s); outs["v_s"].append(v_s)
        y_p = _mlp(y_p, norm_mlp[i], w_up, w_down, i)
        y_s = _mlp(y_s, norm_mlp[i], w_up, w_down, i)
    st = lambda key: jnp.stack(outs[key])
    return (y_p.reshape(n_p, t_p, d), y_s.reshape(n_s, t_s, d),
            st("re_p"), st("im_p"), st("re_s"), st("im_s"), st("cv_p"), st("cv_s"),
            st("k_p"), st("v_p"), st("k_s"), st("v_s"))
```

```python
import functools
import math

import jax
import jax.numpy as jnp
from jax import lax
from jax.experimental import pallas as pl
from jax.experimental.pallas import tpu as pltpu

F32 = jnp.float32
BF16 = jnp.bfloat16
EPS = 1e-6
LANES = 128
SUBLANES = 8
VMEM_LIMIT_BYTES = 56 * 1024 * 1024
VMEM_BUDGET_BYTES = 44 * 1024 * 1024
S5_CHUNK = 16
HIGHEST = lax.Precision.HIGHEST


def _cparams(n_axes):
    return pltpu.CompilerParams(dimension_semantics=("arbitrary",) * n_axes,
                                vmem_limit_bytes=VMEM_LIMIT_BYTES)


def _rms(x, g):
    ms = jnp.mean(x * x, axis=-1, keepdims=True)
    return x * lax.rsqrt(ms + EPS) * g


def _iota(shape, dim):
    return lax.broadcasted_iota(jnp.int32, shape, dim)


def _mod(x, n):
    return x & (n - 1) if n & (n - 1) == 0 else lax.rem(x, n)


def _div(x, n):
    return x >> (n.bit_length() - 1) if n & (n - 1) == 0 else lax.div(x, n)


def _plan_tiles(m, kdim, n_out, row_bytes_per_row, n_w, tile_itemsizes):
    for tm, tn, n_buf in ((1024, 1024, 2), (1024, 512, 2), (1024, 1024, 1), (1024, 512, 1),
                          (512, 1024, 2), (512, 512, 2), (512, 512, 1)):
        tm, tn = min(tm, m), min(tn, n_out)
        stream = n_w * kdim * tn * 2 + tm * tn * sum(tile_itemsizes)
        fixed = tm * kdim * 2 + 2 * n_w * tm * tn * 4
        if n_buf * tm * row_bytes_per_row + 2 * stream + fixed <= VMEM_BUDGET_BYTES:
            break
    return tm, tn, n_buf


def _fused_matmul(name, m, kdim, n_out, row_in, const_in, weights, tile_in, out_dtypes,
                  prologue, epilogue):
    n_row, n_const, n_w, n_tile, n_o = (len(row_in), len(const_in), len(weights),
                                        len(tile_in), len(out_dtypes))
    size = lambda dt: jnp.dtype(dt).itemsize
    tm, tn, n_buf = _plan_tiles(
        m, kdim, n_out, sum(a.shape[1] * size(a.dtype) for a, kind in row_in if kind == "rows"),
        n_w, [size(a.dtype) for a in tile_in] + [size(dt) for dt in out_dtypes])
    row_mode = pl.Buffered(1) if n_buf == 1 else None
    per = tm // SUBLANES
    in_specs, args = [], []
    for arr, kind in row_in:
        if kind == "rows":
            spec = pl.BlockSpec((tm, arr.shape[1]), lambda i, j: (i, 0), pipeline_mode=row_mode)
        else:
            spec = pl.BlockSpec((SUBLANES, arr.shape[1]),
                                lambda i, j: (jnp.maximum(i * per - 1, 0), 0))
        in_specs.append(spec)
        args.append(arr)
    for arr in const_in:
        in_specs.append(pl.BlockSpec(arr.shape, lambda i, j, nd=arr.ndim: (0,) * nd))
        args.append(arr)
    for arr, layer, col in weights:
        in_specs.append(pl.BlockSpec((None, kdim, tn),
                                     lambda i, j, layer=layer, off=col // tn: (layer, 0, j + off)))
        args.append(arr)
    for arr in tile_in:
        in_specs.append(pl.BlockSpec((tm, tn), lambda i, j: (i, j)))
        args.append(arr)
    out_shape = [jax.ShapeDtypeStruct((m, n_out), dt) for dt in out_dtypes]
    out_specs = [pl.BlockSpec((tm, tn), lambda i, j: (i, j)) for _ in out_dtypes]

    def body(*refs):
        p = 0
        row_refs = refs[p:p + n_row]; p += n_row
        const_refs = refs[p:p + n_const]; p += n_const
        w_refs = refs[p:p + n_w]; p += n_w
        tile_refs = refs[p:p + n_tile]; p += n_tile
        out_refs = refs[p:p + n_o]; p += n_o
        lhs_ref = refs[p]

        @pl.when(pl.program_id(1) == 0)
        def _():
            lhs_ref[...] = prologue(row_refs, const_refs, tm).astype(BF16)

        lhs = lhs_ref[...]
        accs = [jnp.dot(lhs, w[...], preferred_element_type=F32) for w in w_refs]
        for o_ref, val in zip(out_refs, epilogue(accs, tile_refs, const_refs)):
            o_ref[...] = val.astype(o_ref.dtype)

    return pl.pallas_call(
        body, grid=(m // tm, n_out // tn), in_specs=in_specs, out_specs=out_specs,
        out_shape=out_shape, scratch_shapes=[pltpu.VMEM((tm, kdim), BF16)],
        compiler_params=_cparams(2), name=name)(*args)


def _ident_prologue(row_refs, const_refs, tm):
    return row_refs[0][...]


def _head_norm(acc, gain, scale):
    segs = []
    for h in range(acc.shape[1] // LANES):
        seg = acc[:, h * LANES:(h + 1) * LANES]
        ms = jnp.mean(seg * seg, axis=-1, keepdims=True)
        y = seg * lax.rsqrt(ms + EPS) * gain
        segs.append(y * scale if scale != 1.0 else y)
    return segs[0] if len(segs) == 1 else jnp.concatenate(segs, axis=-1)


def _mlp_body(x_ref, g_ref, wu_ref, wd_ref, o_ref, xn_ref, acc_ref):
    f = pl.program_id(1)

    @pl.when(f == 0)
    def _():
        xn_ref[...] = _rms(x_ref[...], g_ref[...]).astype(BF16)
        acc_ref[...] = jnp.zeros_like(acc_ref)

    h = jnp.maximum(jnp.dot(xn_ref[...], wu_ref[...], preferred_element_type=F32), 0.0)
    acc_ref[...] += jnp.dot((h * h).astype(BF16), wd_ref[...], preferred_element_type=F32)

    @pl.when(f == pl.num_programs(1) - 1)
    def _():
        o_ref[...] = x_ref[...] + acc_ref[...]


def _mlp(x, gain, w_up, w_down, layer):
    m, d = x.shape
    ff = w_up.shape[2]
    tm = min(512, m)
    tf = min(512 if m > 512 else 2048, ff)
    return pl.pallas_call(
        _mlp_body, grid=(m // tm, ff // tf),
        in_specs=[pl.BlockSpec((tm, d), lambda i, f: (i, 0)),
                  pl.BlockSpec((1, d), lambda i, f: (0, 0)),
                  pl.BlockSpec((None, d, tf), lambda i, f: (layer, 0, f)),
                  pl.BlockSpec((None, tf, d), lambda i, f: (layer, f, 0))],
        out_specs=pl.BlockSpec((tm, d), lambda i, f: (i, 0)),
        out_shape=jax.ShapeDtypeStruct((m, d), F32),
        scratch_shapes=[pltpu.VMEM((tm, d), BF16), pltpu.VMEM((tm, d), F32)],
        compiler_params=_cparams(2), name="mlp")(x, gain.reshape(1, d), w_up, w_down)


def _norm_body(x_ref, g_ref, o_ref):
    o_ref[...] = _rms(x_ref[...], g_ref[...])


def _norm(x, gain):
    m, d = x.shape
    tm = min(512, m)
    return pl.pallas_call(
        _norm_body, grid=(m // tm,),
        in_specs=[pl.BlockSpec((tm, d), lambda i: (i, 0)), pl.BlockSpec((1, d), lambda i: (0, 0))],
        out_specs=pl.BlockSpec((tm, d), lambda i: (i, 0)),
        out_shape=jax.ShapeDtypeStruct((m, d), F32),
        compiler_params=_cparams(1), name="rmsnorm")(x, gain.reshape(1, d))


def _complex_scale(acos, asin, h):
    return acos * h + asin * pltpu.roll(h, h.shape[1] // 2, axis=1)


def _gelu(y):
    return y * (0.5 * (1.0 + jnp.tanh(math.sqrt(2.0 / math.pi) * (y + 0.044715 * (y * y * y)))))


def _s5_step_body(u_ref, m_ref, bc_ref, cp_ref, d_ref, acos_ref, asin_ref, h0_ref, z_ref, hl_ref,
                  *, gt):
    def one_group(g, carry):
        u = u_ref[g]
        ub = u.astype(BF16)
        h_prev = h0_ref[g]
        hl_ref[g] = (jnp.dot(ub, bc_ref[g], preferred_element_type=F32)
                     + _complex_scale(acos_ref[g][0:1], asin_ref[g][0:1], h_prev))
        y = (jnp.dot(ub, m_ref[g], preferred_element_type=F32)
             + jnp.dot(h_prev.astype(BF16), cp_ref[g], preferred_element_type=F32)
             + u * d_ref[g])
        z_ref[g] = _gelu(y).astype(z_ref.dtype)
        return carry

    lax.fori_loop(0, gt, one_group, 0)


def _s5_seq_body(x_ref, bd_ref, bct_ref, cpt_ref, acos_ref, asin_ref, z_ref, hl_ref,
                 m8_ref, bc8_ref, cp8_ref, h_ref, *, chunk, n_chunk, n_seq, c):
    gl = LANES // c
    rows = n_seq * n_chunk
    blk = lambda i: slice(i * LANES, (i + 1) * LANES)
    grp_of_lane = _div(_iota((c, LANES), 1), c)
    for tau in range(chunk):
        kt = bd_ref[tau]
        m8_ref[blk(0), blk(tau)] = jnp.concatenate(
            [jnp.where(grp_of_lane == g, kt, 0.0) for g in range(gl)], axis=0).astype(BF16)
    for s in range(1, chunk):
        for t in range(s, chunk):
            m8_ref[blk(s), blk(t)] = m8_ref[blk(0), blk(t - s)]
        if s % 2:
            m8_ref[blk(s), blk(s - 1)] = jnp.zeros((LANES, LANES), BF16)
    row_grp = _div(_iota((LANES, LANES), 0), c)
    lane_grp = _div(_iota((LANES, LANES), 1), c)
    for s in range(chunk):
        for g in range(gl):
            bc8_ref[blk(s), blk(g)] = jnp.where(row_grp == g, bct_ref[s], 0.0).astype(BF16)
            cp8_ref[blk(g), blk(s)] = jnp.where(lane_grp == g, cpt_ref[s], 0.0).astype(BF16)

    u2 = jnp.concatenate([x_ref[pl.ds(s, rows, stride=chunk), :].astype(BF16)
                          for s in range(chunk)], axis=1)
    h = jnp.dot(u2, bc8_ref[...], preferred_element_type=F32)
    kidx = _iota((rows, 1), 0) & (n_chunk - 1)
    swap = lambda v: jnp.concatenate(
        [pltpu.roll(v[:, blk(g)], LANES // 2, axis=1) for g in range(gl)], axis=1)
    shift, si = 1, 0
    while shift < n_chunk:
        sh = jnp.where(kidx >= shift, pltpu.roll(h, shift, axis=0), 0.0)
        h = h + acos_ref[si:si + 1, :] * sh + asin_ref[si:si + 1, :] * swap(sh)
        shift, si = shift * 2, si + 1
    h_ref[...] = h
    h_prev = jnp.where(kidx >= 1, pltpu.roll(h, 1, axis=0), 0.0).astype(BF16)
    for t in range(0, chunk, 2):
        cols = slice(t * LANES, (t + 2) * LANES)
        y = (jnp.dot(u2[:, :(t + 2) * LANES], m8_ref[:(t + 2) * LANES, cols],
                     preferred_element_type=F32)
             + jnp.dot(h_prev, cp8_ref[:, cols], preferred_element_type=F32))
        z = _gelu(y)
        z_ref[pl.ds(t, rows, stride=chunk), :] = z[:, :LANES]
        z_ref[pl.ds(t + 1, rows, stride=chunk), :] = z[:, LANES:]
    for n in range(n_seq):
        hl_ref[pl.ds(n, 1), :] = h_ref[pl.ds((n + 1) * n_chunk - 1, 1), :]


def _s5_tables(lam_re, lam_im, log_dt, b_re, b_im, c_re, c_im, d, chunk, n_chunk, per_tile):
    g, p = lam_re.shape
    c = b_re.shape[2]
    dt = jnp.exp(log_dt)[:, None]
    xr, xi = lam_re * dt, lam_im * dt

    def a_pow(ks):
        kk = jnp.asarray(ks, F32)[None, :, None]
        mag = jnp.exp(xr[:, None, :] * kk)
        return mag * jnp.cos(xi[:, None, :] * kk), mag * jnp.sin(xi[:, None, :] * kk)

    pr, pi = a_pow(jnp.arange(chunk + 1))
    ar, ai = pr[:, 1], pi[:, 1]
    den = lam_re * lam_re + lam_im * lam_im
    qr = ((ar - 1.0) * lam_re + ai * lam_im) / den
    qi = (ai * lam_re - (ar - 1.0) * lam_im) / den
    bb_re = qr[..., None] * b_re - qi[..., None] * b_im
    bb_im = qr[..., None] * b_im + qi[..., None] * b_re
    w_re = c_re[:, None] * pr[:, :chunk, None] - c_im[:, None] * pi[:, :chunk, None]
    w_im = c_re[:, None] * pi[:, :chunk, None] + c_im[:, None] * pr[:, :chunk, None]
    kt = jnp.einsum("gtoq,gqi->gtoi", jnp.concatenate([w_re, -w_im], axis=-1),
                    jnp.concatenate([bb_re, bb_im], axis=1), precision=HIGHEST)
    pair = lambda x, y: jnp.concatenate([x, y], axis=-1)
    rr, ri = pr[:, :chunk][:, ::-1], pi[:, :chunk][:, ::-1]
    bt_re, bt_im = bb_re.transpose(0, 2, 1), bb_im.transpose(0, 2, 1)
    bc = (pair(rr, rr)[:, :, None] * pair(bt_re, bt_im)[:, None]
          + pair(-ri, ri)[:, :, None] * pair(bt_im, bt_re)[:, None])
    shifts = [1]
    while shifts[-1] * 2 < n_chunk:
        shifts.append(shifts[-1] * 2)
    while len(shifts) < SUBLANES:
        shifts.append(shifts[-1])
    sr, si = a_pow(chunk * jnp.asarray(shifts, F32))
    acos, asin = pair(sr, sr), pair(-si, si)
    if not per_tile:
        kpad = jnp.concatenate([kt, jnp.zeros((g, 1, c, c), F32)], axis=1)
        s_idx = jnp.arange(chunk)[:, None]
        t_idx = jnp.arange(chunk)[None, :]
        tau = jnp.where(t_idx >= s_idx, t_idx - s_idx, chunk)
        mmat = kpad[:, tau].transpose(0, 1, 4, 2, 3).reshape(g, chunk * c, chunk * c)
        cr, ci = c_re.transpose(0, 2, 1)[:, :, None], c_im.transpose(0, 2, 1)[:, :, None]
        nr, ni = pr[:, 1:].transpose(0, 2, 1)[..., None], pi[:, 1:].transpose(0, 2, 1)[..., None]
        cp = jnp.concatenate([cr * nr - ci * ni, -(cr * ni + ci * nr)], axis=1)
        dtile = jnp.tile(d.reshape(g, 1, c), (1, chunk, 1)).reshape(g, 1, chunk * c)
        return (mmat.astype(BF16), bc.reshape(g, chunk * c, 2 * p).astype(BF16),
                cp.reshape(g, 2 * p, chunk * c).astype(BF16), dtile, acos, asin)
    gl = LANES // c
    nt = g // gl
    p2 = 2 * p
    kd = kt.at[:, 0].add(d.reshape(g, c)[:, :, None] * jnp.eye(c, dtype=F32))
    bd = kd.reshape(nt, gl, chunk, c, c).transpose(0, 2, 4, 1, 3).reshape(nt, chunk, c, LANES)
    bct = bc.reshape(nt, gl, chunk, c, p2).transpose(0, 2, 1, 3, 4).reshape(nt, chunk, LANES, p2)
    on_lanes = lambda x: x.reshape(nt, gl, c, p).transpose(0, 3, 1, 2).reshape(nt, 1, p, LANES)
    spread = lambda x: jnp.repeat(
        x[:, 1:].reshape(nt, gl, chunk, p).transpose(0, 2, 3, 1), c, axis=-1)
    cr, ci, nr, ni = on_lanes(c_re), on_lanes(c_im), spread(pr), spread(pi)
    cpt = jnp.concatenate([cr * nr - ci * ni, -(cr * ni + ci * nr)], axis=2)
    tile = lambda a: a.reshape(nt, gl, SUBLANES, p2).transpose(0, 2, 1, 3).reshape(
        nt, SUBLANES, gl * p2)
    return bd, bct, cpt, tile(acos), tile(asin)


def _s5_step(hn, n_seq, t_len, tabs, h0):
    mmat, bc, cp, dt, acos, asin = tabs
    g, lc = mmat.shape[:2]
    p2 = bc.shape[2]
    c = lc // t_len
    u_r = hn.reshape(n_seq, t_len, g, c).transpose(2, 0, 1, 3).reshape(g, n_seq, lc)
    gt = min(8, g)
    grp = lambda shape: pl.BlockSpec((gt,) + shape, lambda i: (i, 0, 0))
    z_r, h_last = pl.pallas_call(
        functools.partial(_s5_step_body, gt=gt), grid=(g // gt,),
        in_specs=[grp((n_seq, lc)), grp((lc, lc)), grp((lc, p2)), grp((p2, lc)), grp((1, lc)),
                  grp((SUBLANES, p2)), grp((SUBLANES, p2)), grp((n_seq, p2))],
        out_specs=[grp((n_seq, lc)), grp((n_seq, p2))],
        out_shape=[jax.ShapeDtypeStruct((g, n_seq, lc), BF16),
                   jax.ShapeDtypeStruct((g, n_seq, p2), F32)],
        compiler_params=_cparams(1), name="s5_step")(u_r, mmat, bc, cp, dt, acos, asin, h0)
    z = z_r.reshape(g, n_seq, t_len, c).transpose(1, 2, 0, 3).reshape(n_seq * t_len, g * c)
    return z, h_last.transpose(1, 0, 2)


def _s5_seq(hn, n_seq, t_len, tabs, chunk):
    bd, bct, cpt, acos, asin = tabs
    m, d = hn.shape
    nt = bd.shape[0]
    p2 = bct.shape[3]
    gl = acos.shape[2] // p2
    c = LANES // gl
    n_chunk = t_len // chunk
    assert p2 == LANES and d == nt * LANES
    assert chunk % 2 == 0
    tab = lambda a: pl.BlockSpec((None,) + a.shape[1:], lambda j: (j,) + (0,) * (a.ndim - 1))
    z, h_last = pl.pallas_call(
        functools.partial(_s5_seq_body, chunk=chunk, n_chunk=n_chunk, n_seq=n_seq, c=c),
        grid=(nt,),
        in_specs=[pl.BlockSpec((m, LANES), lambda j: (0, j)),
                  tab(bd), tab(bct), tab(cpt), tab(acos), tab(asin)],
        out_specs=[pl.BlockSpec((m, LANES), lambda j: (0, j)),
                   pl.BlockSpec((None, n_seq, gl * p2), lambda j: (j, 0, 0))],
        out_shape=[jax.ShapeDtypeStruct((m, d), F32),
                   jax.ShapeDtypeStruct((nt, n_seq, gl * p2), F32)],
        scratch_shapes=[pltpu.VMEM((chunk * LANES, chunk * LANES), BF16),
                        pltpu.VMEM((chunk * LANES, gl * p2), BF16),
                        pltpu.VMEM((gl * p2, chunk * LANES), BF16),
                        pltpu.VMEM((n_seq * n_chunk, gl * p2), F32)],
        compiler_params=_cparams(1), name="s5_seq")(hn, bd, bct, cpt, acos, asin)
    h_last = h_last.reshape(nt, n_seq, gl, p2).transpose(1, 0, 2, 3).reshape(n_seq, nt * gl, p2)
    return z, h_last


def _glu_epilogue(accs, tile_refs, const_refs):
    a, gate = accs
    return [tile_refs[0][...] + a * jax.nn.sigmoid(gate)]


def _s5_layer(y, gain, n_seq, t_len, tabs, h0, w_glu, layer, chunk):
    m, d = y.shape
    if h0 is None:
        z, h_last = _s5_seq(_norm(y, gain), n_seq, t_len, tabs, chunk)
    else:
        z, h_last = _s5_step(_norm(y, gain), n_seq, t_len, tabs, h0)
    (out,) = _fused_matmul("s5_glu", m, d, d, [(z, "rows")], [],
                           [(w_glu, layer, 0), (w_glu, layer, d)], [y], [F32],
                           _ident_prologue, _glu_epilogue)
    p = h_last.shape[2] // 2
    return out, h_last[..., :p], h_last[..., p:]


def _conv_in_body(*refs, tm, t_len, has_buf):
    if has_buf:
        x_ref, g_ref, wb_ref, wc_ref, wv_ref, dw_ref, b0_ref, b1_ref, o_ref, cv_ref, lhs_ref = refs
    else:
        x_ref, g_ref, wb_ref, wc_ref, wv_ref, dw_ref, o_ref, tail_ref, lhs_ref, carry_ref = refs
    i, j = pl.program_id(0), pl.program_id(1)

    @pl.when(j == 0)
    def _():
        lhs_ref[...] = _rms(x_ref[...], g_ref[...]).astype(BF16)

    lhs = lhs_ref[...]
    gate_b, gate_c, v = (jnp.dot(lhs, w[...], preferred_element_type=F32)
                         for w in (wb_ref, wc_ref, wv_ref))
    cv = gate_c * v
    w = dw_ref[...]
    t = _mod(i * tm + _iota((tm, 1), 0), t_len)
    if has_buf:
        b0, b1 = b0_ref[...], b1_ref[...]
        r1 = jnp.where(t >= 1, pltpu.roll(cv, 1, axis=0), b1)
        r2 = jnp.where(t >= 2, pltpu.roll(cv, 2, axis=0), jnp.where(t == 1, b1, b0))
        cv_ref[...] = cv
    else:
        @pl.when(i == 0)
        def _():
            carry_ref[j] = jnp.zeros(carry_ref.shape[1:], F32)

        full = jnp.concatenate([carry_ref[j], cv], axis=0)
        r1 = jnp.where(t >= 1, pltpu.roll(full, 1, axis=0)[SUBLANES:], 0.0)
        r2 = jnp.where(t >= 2, pltpu.roll(full, 2, axis=0)[SUBLANES:], 0.0)
        carry_ref[j] = cv[tm - SUBLANES:]
        tail_ref[...] = cv[tm - SUBLANES:]
    o_ref[...] = (gate_b * (w[2:3] * cv + w[1:2] * r1 + w[0:1] * r2)).astype(o_ref.dtype)


def _conv_in(y, gain, n_seq, t_len, buf, w_in, w_dw, layer):
    m, d = y.shape
    has_buf = buf is not None
    tm, tn = (m if has_buf else min(1024, t_len)), min(512, d)
    assert t_len >= 2 and m <= 1024 if has_buf else t_len % tm == 0 and tm >= SUBLANES
    nj = d // tn
    col = lambda k: pl.BlockSpec((None, d, tn), lambda i, j, k=k: (layer, 0, j + k * nj))
    tile = pl.BlockSpec((tm, tn), lambda i, j: (i, j))
    in_specs = [pl.BlockSpec((tm, d), lambda i, j: (i, 0)), pl.BlockSpec((1, d), lambda i, j: (0, 0)),
                col(0), col(1), col(2), pl.BlockSpec((None, 3, tn), lambda i, j: (layer, 0, j))]
    args = [y, gain.reshape(1, d), w_in, w_in, w_in, w_dw]
    scratch = [pltpu.VMEM((tm, d), BF16)]
    if has_buf:
        in_specs += [tile, tile]
        args += [jnp.repeat(buf[:, k], t_len, axis=0) for k in range(2)]
        out_specs = [tile, tile]
        out_shape = [jax.ShapeDtypeStruct((m, d), BF16), jax.ShapeDtypeStruct((m, d), F32)]
    else:
        out_specs = [tile, pl.BlockSpec((None, SUBLANES, tn), lambda i, j: (i, 0, j))]
        out_shape = [jax.ShapeDtypeStruct((m, d), BF16),
                     jax.ShapeDtypeStruct((m // tm, SUBLANES, d), F32)]
        scratch.append(pltpu.VMEM((nj, SUBLANES, tn), F32))
    g, extra = pl.pallas_call(
        functools.partial(_conv_in_body, tm=tm, t_len=t_len, has_buf=has_buf),
        grid=(m // tm, nj), in_specs=in_specs, out_specs=out_specs, out_shape=out_shape,
        scratch_shapes=scratch, compiler_params=_cparams(2), name="conv_in")(*args)
    if has_buf:
        return g, extra.reshape(n_seq, t_len, d)[:, t_len - 2:]
    per_seq = t_len // tm
    return g, extra.reshape(n_seq, per_seq, SUBLANES, d)[:, per_seq - 1, SUBLANES - 2:]


def _resid_epilogue(accs, tile_refs, const_refs):
    return [tile_refs[0][...] + accs[0]]


def _conv_layer(y, gain, n_seq, t_len, buf, w_in, w_dw, w_out, layer):
    m, d = y.shape
    g, state = _conv_in(y, gain, n_seq, t_len, buf, w_in, w_dw, layer)
    (out,) = _fused_matmul("conv_out", m, d, d, [(g, "rows")], [], [(w_out, layer, 0)], [y],
                           [F32], _ident_prologue, _resid_epilogue)
    return out, state


LOG2_E = math.log2(math.e)


def _sb_logs(z, mask):
    sp = jnp.log2(1.0 + jnp.exp2(-jnp.abs(z)))
    log_b = jnp.minimum(z, 0.0) - sp
    log_1m = log_b - z
    if mask is not None:
        log_1m = jnp.where(mask, log_1m, 0.0)
    return log_b, log_1m


def _sb_suffix(log_1m, upper):
    hi = log_1m.astype(BF16)
    lo = (log_1m - hi.astype(F32)).astype(BF16)
    return (jnp.dot(hi, upper, preferred_element_type=F32)
            + jnp.dot(lo, upper, preferred_element_type=F32))


def _sb_finish(log_b, log_1m, suffix, mask, run):
    a = jnp.exp2(log_b + suffix + run)
    if mask is not None:
        a = jnp.where(mask, a, 0.0)
    return a, run + jnp.sum(log_1m, axis=-1, keepdims=True)


def _sb_weights(z, mask, upper, run):
    log_b, log_1m = _sb_logs(z, mask)
    return _sb_finish(log_b, log_1m, _sb_suffix(log_1m, upper), mask, run)


def _later_key_matrix(tk):
    return (_iota((tk, tk), 0) > _iota((tk, tk), 1)).astype(BF16)


def _qk(q, k_blk):
    return lax.dot_general(q, k_blk, (((1,), (1,)), ((), ())), preferred_element_type=F32)


def _attn_body(bias_ref, q_ref, k_ref, v_ref, o_ref, *, tq, tr, dh, hps):
    hg = pl.program_id(1)
    qi = pl.program_id(2)
    upper = _later_key_matrix(tq)
    units = [(hh, r0) for hh in range(hps) for r0 in range(0, tq, tr)]

    def tile(kb, carries, masked):
        start = pl.multiple_of(kb * tq, tq)
        lanes = lambda hh: slice(hh * dh, (hh + 1) * dh)
        masks = [(_iota((tr, tq), 1) < _iota((tr, tq), 0) + r0) if masked else None
                 for _, r0 in units]
        zs = [_qk(q_ref[r0:r0 + tr, lanes(hh)], k_ref[pl.ds(start, tq), lanes(hh)])
              + bias_ref[hg * hps + hh] for hh, r0 in units]
        logs = [_sb_logs(z, m) for z, m in zip(zs, masks)]
        sufs = [_sb_suffix(l1m, upper) for _, l1m in logs]
        fins = [_sb_finish(lb, l1m, suf, m, run)
                for (lb, l1m), suf, m, (run, _) in zip(logs, sufs, masks, carries)]
        accs = [acc + jnp.dot(a.astype(BF16), v_ref[pl.ds(start, tq), lanes(hh)],
                              preferred_element_type=F32)
                for (a, _), (_, acc), (hh, _) in zip(fins, carries, units)]
        return tuple((run, acc) for (_, run), acc in zip(fins, accs))

    carries = tuple((jnp.zeros((tr, 1), F32), jnp.zeros((tr, dh), F32)) for _ in units)
    carries = tile(qi, carries, True)
    carries = lax.fori_loop(0, qi, lambda it, c: tile(qi - 1 - it, c, False), carries)
    for (hh, r0), (_, acc) in zip(units, carries):
        o_ref[r0:r0 + tr, hh * dh:(hh + 1) * dh] = acc.astype(o_ref.dtype)


def _attn_prompt(q, k, v, bias, n_seq, t_len, n_heads):
    m, d = q.shape
    dh = d // n_heads
    tq = min(256, t_len)
    hps = next(h for h in (4, 2, 1) if n_heads % h == 0)
    qb = t_len // tq
    grid_spec = pltpu.PrefetchScalarGridSpec(
        num_scalar_prefetch=1, grid=(n_seq, n_heads // hps, qb),
        in_specs=[pl.BlockSpec((tq, hps * dh), lambda n, h, i, b: (n * qb + i, h)),
                  pl.BlockSpec((t_len, hps * dh), lambda n, h, i, b: (n, h)),
                  pl.BlockSpec((t_len, hps * dh), lambda n, h, i, b: (n, h))],
        out_specs=pl.BlockSpec((tq, hps * dh), lambda n, h, i, b: (n * qb + i, h)))
    return pl.pallas_call(
        functools.partial(_attn_body, tq=tq, tr=min(128, tq), dh=dh, hps=hps),
        grid_spec=grid_spec,
        out_shape=jax.ShapeDtypeStruct((m, d), BF16),
        compiler_params=_cparams(3), name="sb_attn_prompt")(bias, q, k, v)


def _decode_body(pt_ref, q_ref, bias_ref, e_ref, et_ref, hm_ref, hm32_ref, *refs, n_heads, ppt):
    kn_refs, vn_refs = refs[:ppt], refs[ppt:2 * ppt]
    kc_refs, vc_refs = refs[2 * ppt:3 * ppt], refs[3 * ppt:4 * ppt]
    o_ref, run_ref, acc_ref = refs[4 * ppt:]
    step = pl.program_id(1)
    page = vn_refs[0].shape[0] // n_heads
    rows = q_ref.shape[0]
    tk = ppt * page

    def tile(k_refs, v_refs, mask, run, acc):
        parts = []
        for slot in range(ppt):
            z_rows = _qk(q_ref[...], k_refs[slot][...].astype(BF16)) * hm32_ref[...]
            hi = z_rows.astype(BF16)
            parts += [hi, (z_rows - hi.astype(F32)).astype(BF16)]
        zc = jnp.dot(jnp.concatenate(parts, axis=0), et_ref[...], preferred_element_type=F32)
        zs = [zc[2 * s * rows:(2 * s + 1) * rows] + zc[(2 * s + 1) * rows:(2 * s + 2) * rows]
              for s in range(ppt)]
        z = (zs[0] if ppt == 1 else jnp.concatenate(zs, axis=1)) + bias_ref[...]
        a, run = _sb_weights(z, mask, _later_key_matrix(tk), run)
        a = a.astype(BF16)
        a_st = jnp.concatenate([a[:, s * page:(s + 1) * page] for s in range(ppt)], axis=0)
        a_rows = jnp.dot(a_st, e_ref[...], preferred_element_type=F32).astype(BF16)
        for slot in range(ppt):
            acc = acc + jnp.dot(a_rows[slot * rows:(slot + 1) * rows] * hm_ref[...],
                                v_refs[slot][...].astype(BF16), preferred_element_type=F32)
        run_ref[...] = jnp.broadcast_to(run, run_ref.shape)
        acc_ref[...] = acc

    @pl.when(step == 0)
    def _():
        mask = _iota((rows, tk), 1) < _div(_iota((rows, tk), 0), n_heads)
        tile(kn_refs, vn_refs, mask, jnp.zeros((rows, 1), F32), jnp.zeros(acc_ref.shape, F32))

    @pl.when(step > 0)
    def _():
        tile(kc_refs, vc_refs, None, run_ref[:, 0:1], acc_ref[...])

    @pl.when(step == pl.num_programs(1) - 1)
    def _():
        o_ref[...] = acc_ref[...]


def _attn_decode(q, k_new, v_new, bias, cache_k, cache_v, layer, page_table, n_seq, t_len,
                 n_heads):
    d = q.shape[1]
    dh = d // n_heads
    n_layers, n_pool, page = cache_k.shape[:3]
    n_pages = page_table.shape[1]
    ppt = next(p for p in (4, 2, 1) if n_pages % p == 0)
    n_tiles = n_pages // ppt
    rows = t_len * n_heads
    assert n_heads % SUBLANES == 0 and t_len <= page
    q_rows = q.reshape(n_seq, rows, dh)
    bias_rows = jnp.tile(bias, t_len).reshape(rows, 1)
    key_of_row = jnp.arange(page * n_heads) // n_heads
    expand = (key_of_row[None, :] == jnp.arange(page)[:, None]).astype(BF16)
    head_of_row = jnp.arange(page * n_heads) % n_heads
    head_mask = head_of_row[None, :] == (jnp.arange(rows) % n_heads)[:, None]
    pad_t = ((0, 0), (0, ppt * page - t_len), (0, 0), (0, 0))
    as_rows = lambda x: jnp.pad(x.reshape(n_seq, t_len, n_heads, dh), pad_t).reshape(
        n_seq, ppt, page * n_heads, dh)
    kn, vn = as_rows(k_new), as_rows(v_new)
    kc = cache_k.reshape(n_layers, n_pool, page * n_heads, dh)
    vc = cache_v.reshape(n_layers, n_pool, page * n_heads, dh)

    def phys(n, s, pt, slot):
        return pt[n, (n_tiles - jnp.maximum(s, 1)) * ppt + slot]

    const = lambda shape: pl.BlockSpec(shape, lambda n, s, pt: (0,) * len(shape))
    in_specs = [pl.BlockSpec((None, rows, dh), lambda n, s, pt: (n, 0, 0)),
                const((rows, 1)), const(expand.shape), const(expand.shape[::-1]),
                const(head_mask.shape), const(head_mask.shape)]
    args = [q_rows, bias_rows, expand, expand.T, head_mask.astype(BF16), head_mask.astype(F32)]
    blk = (None, None, page * n_heads, dh)
    for arr in (kn, vn):
        for slot in range(ppt):
            in_specs.append(pl.BlockSpec(blk, lambda n, s, pt, slot=slot: (n, slot, 0, 0)))
            args.append(arr)
    for arr in (kc, vc):
        for slot in range(ppt):
            in_specs.append(pl.BlockSpec(
                blk, lambda n, s, pt, slot=slot: (layer, phys(n, s, pt, slot), 0, 0)))
            args.append(arr)
    grid_spec = pltpu.PrefetchScalarGridSpec(
        num_scalar_prefetch=1, grid=(n_seq, n_tiles + 1), in_specs=in_specs,
        out_specs=pl.BlockSpec((None, rows, dh), lambda n, s, pt: (n, 0, 0)),
        scratch_shapes=[pltpu.VMEM((rows, LANES), F32), pltpu.VMEM((rows, dh), F32)])
    o = pl.pallas_call(
        functools.partial(_decode_body, n_heads=n_heads, ppt=ppt), grid_spec=grid_spec,
        out_shape=jax.ShapeDtypeStruct((n_seq, rows, dh), F32),
        compiler_params=_cparams(2), name="sb_attn_decode")(page_table, *args)
    return o.reshape(n_seq * t_len, d).astype(BF16)


def _qkv_body(x_ref, g_ref, qg_ref, kg_ref, w_ref, q_ref, k32_ref, kb_ref, v32_ref, vb_ref,
              lhs_ref, *, nq, q_scale):
    j = pl.program_id(1)

    @pl.when(j == 0)
    def _():
        lhs_ref[...] = _rms(x_ref[...], g_ref[...]).astype(BF16)

    acc = jnp.dot(lhs_ref[...], w_ref[...], preferred_element_type=F32)

    @pl.when(j < nq)
    def _():
        q_ref[...] = _head_norm(acc, qg_ref[...], q_scale).astype(q_ref.dtype)

    @pl.when(jnp.logical_and(j >= nq, j < 2 * nq))
    def _():
        k = _head_norm(acc, kg_ref[...], 1.0)
        k32_ref[...] = k
        kb_ref[...] = k.astype(kb_ref.dtype)

    @pl.when(j >= 2 * nq)
    def _():
        v32_ref[...] = acc
        vb_ref[...] = acc.astype(vb_ref.dtype)


def _qkv_proj(y, gain, w_qkv, layer, q_gain, k_gain, q_scale):
    m, d = y.shape
    dh = q_gain.shape[0]
    tm, tn = min(1024, m), min(512, d)
    nq = d // tn
    out = lambda kind: pl.BlockSpec((tm, tn), lambda i, j: (i, jnp.clip(j - kind * nq, 0, nq - 1)))
    small = lambda n: pl.BlockSpec((1, n), lambda i, j: (0, 0))
    return pl.pallas_call(
        functools.partial(_qkv_body, nq=nq, q_scale=q_scale), grid=(m // tm, 3 * nq),
        in_specs=[pl.BlockSpec((tm, d), lambda i, j: (i, 0)), small(d), small(dh), small(dh),
                  pl.BlockSpec((None, d, tn), lambda i, j: (layer, 0, j))],
        out_specs=[out(0), out(1), out(1), out(2), out(2)],
        out_shape=[jax.ShapeDtypeStruct((m, d), dt) for dt in (BF16, F32, BF16, F32, BF16)],
        scratch_shapes=[pltpu.VMEM((tm, d), BF16)],
        compiler_params=_cparams(2), name="attn_qkv")(
            y, gain.reshape(1, d), q_gain.reshape(1, dh), k_gain.reshape(1, dh), w_qkv)


def _attn_layer(y, gain, n_seq, t_len, n_heads, w_qkv, q_gain, k_gain, sb_bias, w_o, layer,
                cache=None):
    m, d = y.shape
    dh = d // n_heads
    sb_bias = sb_bias * LOG2_E
    q, k32, kb, v32, vb = _qkv_proj(y, gain, w_qkv, layer, q_gain, k_gain, dh ** -0.5 * LOG2_E)
    if cache is None:
        o = _attn_prompt(q, kb, vb, sb_bias, n_seq, t_len, n_heads)
    else:
        cache_k, cache_v, page_table = cache
        o = _attn_decode(q, k32, v32, sb_bias, cache_k, cache_v, layer, page_table, n_seq, t_len,
                         n_heads)
    (out,) = _fused_matmul("attn_out", m, d, d, [(o, "rows")], [], [(w_o, layer, 0)], [y], [F32],
                           _ident_prologue, _resid_epilogue)
    shape = (n_seq, t_len, n_heads, dh)
    return out, k32.reshape(shape), v32.reshape(shape)


def kernel(x_prompt, x_sample, state_ssm_re, state_ssm_im, state_conv, cache_k, cache_v, page_table, norm_mix, norm_mlp, ssm_lambda_re, ssm_lambda_im, ssm_log_dt, ssm_b_re, ssm_b_im, ssm_c_re, ssm_c_im, ssm_d, ssm_w_glu, conv_w_in, conv_w_dw, conv_w_out, attn_w_qkv, attn_q_norm, attn_k_norm, attn_sb_bias, attn_w_o, mlp_w_up, mlp_w_down):
    n_p, t_p, d = x_prompt.shape
    n_s, t_s, _ = x_sample.shape
    depth = norm_mix.shape[0]
    n_heads = attn_sb_bias.shape[1]
    assert conv_w_dw.shape[1] == 3 and d // n_heads == LANES
    chunk_p = min(S5_CHUNK, t_p)
    assert t_p % chunk_p == 0 and (t_p // chunk_p) & (t_p // chunk_p - 1) == 0

    w_glu, w_in, w_out = (w.astype(BF16) for w in (ssm_w_glu, conv_w_in, conv_w_out))
    w_qkv, w_o = attn_w_qkv.astype(BF16), attn_w_o.astype(BF16)
    w_up, w_down = mlp_w_up.astype(BF16), mlp_w_down.astype(BF16)

    y_p = x_prompt.reshape(n_p * t_p, d)
    y_s = x_sample.reshape(n_s * t_s, d)
    outs = {k: [] for k in ("re_p", "im_p", "re_s", "im_s", "cv_p", "cv_s", "k_p", "v_p", "k_s", "v_s")}
    for i in range(depth):
        kind, j = i % 3, i // 3
        if kind == 0:
            ssm = (ssm_lambda_re[j], ssm_lambda_im[j], ssm_log_dt[j], ssm_b_re[j], ssm_b_im[j],
                   ssm_c_re[j], ssm_c_im[j], ssm_d[j])
            tabs_p = _s5_tables(*ssm, chunk_p, t_p // chunk_p, True)
            tabs_s = _s5_tables(*ssm, t_s, 1, False)
            h0 = jnp.concatenate([state_ssm_re[j], state_ssm_im[j]], axis=-1).transpose(1, 0, 2)
            y_p, re_p, im_p = _s5_layer(y_p, norm_mix[i], n_p, t_p, tabs_p, None, w_glu, j, chunk_p)
            y_s, re_s, im_s = _s5_layer(y_s, norm_mix[i], n_s, t_s, tabs_s, h0, w_glu, j, t_s)
            outs["re_p"].append(re_p); outs["im_p"].append(im_p)
            outs["re_s"].append(re_s); outs["im_s"].append(im_s)
        elif kind == 1:
            y_p, cv_p = _conv_layer(y_p, norm_mix[i], n_p, t_p, None, w_in, conv_w_dw, w_out, j)
            y_s, cv_s = _conv_layer(y_s, norm_mix[i], n_s, t_s, state_conv[j], w_in, conv_w_dw,
                                    w_out, j)
            outs["cv_p"].append(cv_p); outs["cv_s"].append(cv_s)
        else:
            attn = (w_qkv, attn_q_norm[j], attn_k_norm[j], attn_sb_bias[j], w_o, j)
            y_p, k_p, v_p = _attn_layer(y_p, norm_mix[i], n_p, t_p, n_heads, *attn)
            y_s, k_s, v_s = _attn_layer(y_s, norm_mix[i], n_s, t_s, n_heads, *attn,
                                        cache=(cache_k, cache_v, page_table))
            outs["k_p"].append(k_p); outs["v_p"].append(v_p)
            outs["k_s"].append(k_s); outs["v_s"].append(v_s)
        y_p = _mlp(y_p, norm_mlp[i], w_up, w_down, i)
        y_s = _mlp(y_s, norm_mlp[i], w_up, w_down, i)
    st = lambda key: jnp.stack(outs[key])
    return (y_p.reshape(n_p, t_p, d), y_s.reshape(n_s, t_s, d),
            st("re_p"), st("im_p"), st("re_s"), st("im_s"), st("cv_p"), st("cv_s"),
            st("k_p"), st("v_p"), st("k_s"), st("v_s"))
```

```python
import functools
import math

import jax
import jax.numpy as jnp
from jax import lax
from jax.experimental import pallas as pl
from jax.experimental.pallas import tpu as pltpu

F32 = jnp.float32
BF16 = jnp.bfloat16
EPS = 1e-6
LANES = 128
SUBLANES = 8
VMEM_LIMIT_BYTES = 56 * 1024 * 1024
VMEM_BUDGET_BYTES = 44 * 1024 * 1024
S5_CHUNK = 16
HIGHEST = lax.Precision.HIGHEST


def _cparams(n_axes):
    return pltpu.CompilerParams(dimension_semantics=("arbitrary",) * n_axes,
                                vmem_limit_bytes=VMEM_LIMIT_BYTES)


def _rms(x, g):
    ms = jnp.mean(x * x, axis=-1, keepdims=True)
    return x * lax.rsqrt(ms + EPS) * g


def _iota(shape, dim):
    return lax.broadcasted_iota(jnp.int32, shape, dim)


def _mod(x, n):
    return x & (n - 1) if n & (n - 1) == 0 else lax.rem(x, n)


def _div(x, n):
    return x >> (n.bit_length() - 1) if n & (n - 1) == 0 else lax.div(x, n)


def _plan_tiles(m, kdim, n_out, row_bytes_per_row, n_w, tile_itemsizes):
    for tm, tn, n_buf in ((1024, 1024, 2), (1024, 512, 2), (1024, 1024, 1), (1024, 512, 1),
                          (512, 1024, 2), (512, 512, 2), (512, 512, 1)):
        tm, tn = min(tm, m), min(tn, n_out)
        stream = n_w * kdim * tn * 2 + tm * tn * sum(tile_itemsizes)
        fixed = tm * kdim * 2 + 2 * n_w * tm * tn * 4
        if n_buf * tm * row_bytes_per_row + 2 * stream + fixed <= VMEM_BUDGET_BYTES:
            break
    return tm, tn, n_buf


def _fused_matmul(name, m, kdim, n_out, row_in, const_in, weights, tile_in, out_dtypes,
                  prologue, epilogue):
    n_row, n_const, n_w, n_tile, n_o = (len(row_in), len(const_in), len(weights),
                                        len(tile_in), len(out_dtypes))
    size = lambda dt: jnp.dtype(dt).itemsize
    tm, tn, n_buf = _plan_tiles(
        m, kdim, n_out, sum(a.shape[1] * size(a.dtype) for a, kind in row_in if kind == "rows"),
        n_w, [size(a.dtype) for a in tile_in] + [size(dt) for dt in out_dtypes])
    row_mode = pl.Buffered(1) if n_buf == 1 else None
    per = tm // SUBLANES
    in_specs, args = [], []
    for arr, kind in row_in:
        if kind == "rows":
            spec = pl.BlockSpec((tm, arr.shape[1]), lambda i, j: (i, 0), pipeline_mode=row_mode)
        else:
            spec = pl.BlockSpec((SUBLANES, arr.shape[1]),
                                lambda i, j: (jnp.maximum(i * per - 1, 0), 0))
        in_specs.append(spec)
        args.append(arr)
    for arr in const_in:
        in_specs.append(pl.BlockSpec(arr.shape, lambda i, j, nd=arr.ndim: (0,) * nd))
        args.append(arr)
    for arr, layer, col in weights:
        in_specs.append(pl.BlockSpec((None, kdim, tn),
                                     lambda i, j, layer=layer, off=col // tn: (layer, 0, j + off)))
        args.append(arr)
    for arr in tile_in:
        in_specs.append(pl.BlockSpec((tm, tn), lambda i, j: (i, j)))
        args.append(arr)
    out_shape = [jax.ShapeDtypeStruct((m, n_out), dt) for dt in out_dtypes]
    out_specs = [pl.BlockSpec((tm, tn), lambda i, j: (i, j)) for _ in out_dtypes]

    def body(*refs):
        p = 0
        row_refs = refs[p:p + n_row]; p += n_row
        const_refs = refs[p:p + n_const]; p += n_const
        w_refs = refs[p:p + n_w]; p += n_w
        tile_refs = refs[p:p + n_tile]; p += n_tile
        out_refs = refs[p:p + n_o]; p += n_o
        lhs_ref = refs[p]

        @pl.when(pl.program_id(1) == 0)
        def _():
            lhs_ref[...] = prologue(row_refs, const_refs, tm).astype(BF16)

        lhs = lhs_ref[...]
        accs = [jnp.dot(lhs, w[...], preferred_element_type=F32) for w in w_refs]
        for o_ref, val in zip(out_refs, epilogue(accs, tile_refs, const_refs)):
            o_ref[...] = val.astype(o_ref.dtype)

    return pl.pallas_call(
        body, grid=(m // tm, n_out // tn), in_specs=in_specs, out_specs=out_specs,
        out_shape=out_shape, scratch_shapes=[pltpu.VMEM((tm, kdim), BF16)],
        compiler_params=_cparams(2), name=name)(*args)


def _ident_prologue(row_refs, const_refs, tm):
    return row_refs[0][...]


def _head_norm(acc, gain, scale):
    segs = []
    for h in range(acc.shape[1] // LANES):
        seg = acc[:, h * LANES:(h + 1) * LANES]
        ms = jnp.mean(seg * seg, axis=-1, keepdims=True)
        y = seg * lax.rsqrt(ms + EPS) * gain
        segs.append(y * scale if scale != 1.0 else y)
    return segs[0] if len(segs) == 1 else jnp.concatenate(segs, axis=-1)


def _mlp_body(x_ref, g_ref, wu_ref, wd_ref, o_ref, xn_ref, acc_ref):
    f = pl.program_id(1)

    @pl.when(f == 0)
    def _():
        xn_ref[...] = _rms(x_ref[...], g_ref[...]).astype(BF16)
        acc_ref[...] = jnp.zeros_like(acc_ref)

    h = jnp.maximum(jnp.dot(xn_ref[...], wu_ref[...], preferred_element_type=F32), 0.0)
    acc_ref[...] += jnp.dot((h * h).astype(BF16), wd_ref[...], preferred_element_type=F32)

    @pl.when(f == pl.num_programs(1) - 1)
    def _():
        o_ref[...] = x_ref[...] + acc_ref[...]


def _mlp(x, gain, w_up, w_down, layer):
    m, d = x.shape
    ff = w_up.shape[2]
    tm = min(512, m)
    tf = min(512 if m > 512 else 2048, ff)
    return pl.pallas_call(
        _mlp_body, grid=(m // tm, ff // tf),
        in_specs=[pl.BlockSpec((tm, d), lambda i, f: (i, 0)),
                  pl.BlockSpec((1, d), lambda i, f: (0, 0)),
                  pl.BlockSpec((None, d, tf), lambda i, f: (layer, 0, f)),
                  pl.BlockSpec((None, tf, d), lambda i, f: (layer, f, 0))],
        out_specs=pl.BlockSpec((tm, d), lambda i, f: (i, 0)),
        out_shape=jax.ShapeDtypeStruct((m, d), F32),
        scratch_shapes=[pltpu.VMEM((tm, d), BF16), pltpu.VMEM((tm, d), F32)],
        compiler_params=_cparams(2), name="mlp")(x, gain.reshape(1, d), w_up, w_down)


def _norm_body(x_ref, g_ref, o_ref):
    o_ref[...] = _rms(x_ref[...], g_ref[...])


def _norm(x, gain):
    m, d = x.shape
    tm = min(512, m)
    return pl.pallas_call(
        _norm_body, grid=(m // tm,),
        in_specs=[pl.BlockSpec((tm, d), lambda i: (i, 0)), pl.BlockSpec((1, d), lambda i: (0, 0))],
        out_specs=pl.BlockSpec((tm, d), lambda i: (i, 0)),
        out_shape=jax.ShapeDtypeStruct((m, d), F32),
        compiler_params=_cparams(1), name="rmsnorm")(x, gain.reshape(1, d))


def _complex_scale(acos, asin, h):
    return acos * h + asin * pltpu.roll(h, h.shape[1] // 2, axis=1)


def _gelu(y):
    return y * (0.5 * (1.0 + jnp.tanh(math.sqrt(2.0 / math.pi) * (y + 0.044715 * (y * y * y)))))


def _s5_step_body(u_ref, m_ref, bc_ref, cp_ref, d_ref, acos_ref, asin_ref, h0_ref, z_ref, hl_ref,
                  *, gt):
    def one_group(g, carry):
        u = u_ref[g]
        ub = u.astype(BF16)
        h_prev = h0_ref[g]
        hl_ref[g] = (jnp.dot(ub, bc_ref[g], preferred_element_type=F32)
                     + _complex_scale(acos_ref[g][0:1], asin_ref[g][0:1], h_prev))
        y = (jnp.dot(ub, m_ref[g], preferred_element_type=F32)
             + jnp.dot(h_prev.astype(BF16), cp_ref[g], preferred_element_type=F32)
             + u * d_ref[g])
        z_ref[g] = _gelu(y).astype(z_ref.dtype)
        return carry

    lax.fori_loop(0, gt, one_group, 0)


def _s5_seq_body(x_ref, bd_ref, bct_ref, cpt_ref, acos_ref, asin_ref, z_ref, hl_ref,
                 m8_ref, bc8_ref, cp8_ref, h_ref, *, chunk, n_chunk, n_seq, c):
    gl = LANES // c
    rows = n_seq * n_chunk
    blk = lambda i: slice(i * LANES, (i + 1) * LANES)
    grp_of_lane = _div(_iota((c, LANES), 1), c)
    for tau in range(chunk):
        kt = bd_ref[tau]
        m8_ref[blk(0), blk(tau)] = jnp.concatenate(
            [jnp.where(grp_of_lane == g, kt, 0.0) for g in range(gl)], axis=0).astype(BF16)
    for s in range(1, chunk):
        for t in range(s, chunk):
            m8_ref[blk(s), blk(t)] = m8_ref[blk(0), blk(t - s)]
        if s % 2:
            m8_ref[blk(s), blk(s - 1)] = jnp.zeros((LANES, LANES), BF16)
    row_grp = _div(_iota((LANES, LANES), 0), c)
    lane_grp = _div(_iota((LANES, LANES), 1), c)
    for s in range(chunk):
        for g in range(gl):
            bc8_ref[blk(s), blk(g)] = jnp.where(row_grp == g, bct_ref[s], 0.0).astype(BF16)
            cp8_ref[blk(g), blk(s)] = jnp.where(lane_grp == g, cpt_ref[s], 0.0).astype(BF16)

    u2 = jnp.concatenate([x_ref[pl.ds(s, rows, stride=chunk), :].astype(BF16)
                          for s in range(chunk)], axis=1)
    h = jnp.dot(u2, bc8_ref[...], preferred_element_type=F32)
    kidx = _iota((rows, 1), 0) & (n_chunk - 1)
    swap = lambda v: jnp.concatenate(
        [pltpu.roll(v[:, blk(g)], LANES // 2, axis=1) for g in range(gl)], axis=1)
    shift, si = 1, 0
    while shift < n_chunk:
        sh = jnp.where(kidx >= shift, pltpu.roll(h, shift, axis=0), 0.0)
        h = h + acos_ref[si:si + 1, :] * sh + asin_ref[si:si + 1, :] * swap(sh)
        shift, si = shift * 2, si + 1
    h_ref[...] = h
    h_prev = jnp.where(kidx >= 1, pltpu.roll(h, 1, axis=0), 0.0).astype(BF16)
    for t in range(0, chunk, 2):
        cols = slice(t * LANES, (t + 2) * LANES)
        y = (jnp.dot(u2[:, :(t + 2) * LANES], m8_ref[:(t + 2) * LANES, cols],
                     preferred_element_type=F32)
             + jnp.dot(h_prev, cp8_ref[:, cols], preferred_element_type=F32))
        z = _gelu(y)
        z_ref[pl.ds(t, rows, stride=chunk), :] = z[:, :LANES]
        z_ref[pl.ds(t + 1, rows, stride=chunk), :] = z[:, LANES:]
    for n in range(n_seq):
        hl_ref[pl.ds(n, 1), :] = h_ref[pl.ds((n + 1) * n_chunk - 1, 1), :]


def _s5_tables(lam_re, lam_im, log_dt, b_re, b_im, c_re, c_im, d, chunk, n_chunk, per_tile):
    g, p = lam_re.shape
    c = b_re.shape[2]
    dt = jnp.exp(log_dt)[:, None]
    xr, xi = lam_re * dt, lam_im * dt

    def a_pow(ks):
        kk = jnp.asarray(ks, F32)[None, :, None]
        mag = jnp.exp(xr[:, None, :] * kk)
        return mag * jnp.cos(xi[:, None, :] * kk), mag * jnp.sin(xi[:, None, :] * kk)

    pr, pi = a_pow(jnp.arange(chunk + 1))
    ar, ai = pr[:, 1], pi[:, 1]
    den = lam_re * lam_re + lam_im * lam_im
    qr = ((ar - 1.0) * lam_re + ai * lam_im) / den
    qi = (ai * lam_re - (ar - 1.0) * lam_im) / den
    bb_re = qr[..., None] * b_re - qi[..., None] * b_im
    bb_im = qr[..., None] * b_im + qi[..., None] * b_re
    w_re = c_re[:, None] * pr[:, :chunk, None] - c_im[:, None] * pi[:, :chunk, None]
    w_im = c_re[:, None] * pi[:, :chunk, None] + c_im[:, None] * pr[:, :chunk, None]
    kt = jnp.einsum("gtoq,gqi->gtoi", jnp.concatenate([w_re, -w_im], axis=-1),
                    jnp.concatenate([bb_re, bb_im], axis=1), precision=HIGHEST)
    pair = lambda x, y: jnp.concatenate([x, y], axis=-1)
    rr, ri = pr[:, :chunk][:, ::-1], pi[:, :chunk][:, ::-1]
    bt_re, bt_im = bb_re.transpose(0, 2, 1), bb_im.transpose(0, 2, 1)
    bc = (pair(rr, rr)[:, :, None] * pair(bt_re, bt_im)[:, None]
          + pair(-ri, ri)[:, :, None] * pair(bt_im, bt_re)[:, None])
    shifts = [1]
    while shifts[-1] * 2 < n_chunk:
        shifts.append(shifts[-1] * 2)
    while len(shifts) < SUBLANES:
        shifts.append(shifts[-1])
    sr, si = a_pow(chunk * jnp.asarray(shifts, F32))
    acos, asin = pair(sr, sr), pair(-si, si)
    if not per_tile:
        kpad = jnp.concatenate([kt, jnp.zeros((g, 1, c, c), F32)], axis=1)
        s_idx = jnp.arange(chunk)[:, None]
        t_idx = jnp.arange(chunk)[None, :]
        tau = jnp.where(t_idx >= s_idx, t_idx - s_idx, chunk)
        mmat = kpad[:, tau].transpose(0, 1, 4, 2, 3).reshape(g, chunk * c, chunk * c)
        cr, ci = c_re.transpose(0, 2, 1)[:, :, None], c_im.transpose(0, 2, 1)[:, :, None]
        nr, ni = pr[:, 1:].transpose(0, 2, 1)[..., None], pi[:, 1:].transpose(0, 2, 1)[..., None]
        cp = jnp.concatenate([cr * nr - ci * ni, -(cr * ni + ci * nr)], axis=1)
        dtile = jnp.tile(d.reshape(g, 1, c), (1, chunk, 1)).reshape(g, 1, chunk * c)
        return (mmat.astype(BF16), bc.reshape(g, chunk * c, 2 * p).astype(BF16),
                cp.reshape(g, 2 * p, chunk * c).astype(BF16), dtile, acos, asin)
    gl = LANES // c
    nt = g // gl
    p2 = 2 * p
    kd = kt.at[:, 0].add(d.reshape(g, c)[:, :, None] * jnp.eye(c, dtype=F32))
    bd = kd.reshape(nt, gl, chunk, c, c).transpose(0, 2, 4, 1, 3).reshape(nt, chunk, c, LANES)
    bct = bc.reshape(nt, gl, chunk, c, p2).transpose(0, 2, 1, 3, 4).reshape(nt, chunk, LANES, p2)
    on_lanes = lambda x: x.reshape(nt, gl, c, p).transpose(0, 3, 1, 2).reshape(nt, 1, p, LANES)
    spread = lambda x: jnp.repeat(
        x[:, 1:].reshape(nt, gl, chunk, p).transpose(0, 2, 3, 1), c, axis=-1)
    cr, ci, nr, ni = on_lanes(c_re), on_lanes(c_im), spread(pr), spread(pi)
    cpt = jnp.concatenate([cr * nr - ci * ni, -(cr * ni + ci * nr)], axis=2)
    tile = lambda a: a.reshape(nt, gl, SUBLANES, p2).transpose(0, 2, 1, 3).reshape(
        nt, SUBLANES, gl * p2)
    return bd, bct, cpt, tile(acos), tile(asin)


def _s5_step(hn, n_seq, t_len, tabs, h0):
    mmat, bc, cp, dt, acos, asin = tabs
    g, lc = mmat.shape[:2]
    p2 = bc.shape[2]
    c = lc // t_len
    u_r = hn.reshape(n_seq, t_len, g, c).transpose(2, 0, 1, 3).reshape(g, n_seq, lc)
    gt = min(8, g)
    grp = lambda shape: pl.BlockSpec((gt,) + shape, lambda i: (i, 0, 0))
    z_r, h_last = pl.pallas_call(
        functools.partial(_s5_step_body, gt=gt), grid=(g // gt,),
        in_specs=[grp((n_seq, lc)), grp((lc, lc)), grp((lc, p2)), grp((p2, lc)), grp((1, lc)),
                  grp((SUBLANES, p2)), grp((SUBLANES, p2)), grp((n_seq, p2))],
        out_specs=[grp((n_seq, lc)), grp((n_seq, p2))],
        out_shape=[jax.ShapeDtypeStruct((g, n_seq, lc), BF16),
                   jax.ShapeDtypeStruct((g, n_seq, p2), F32)],
        compiler_params=_cparams(1), name="s5_step")(u_r, mmat, bc, cp, dt, acos, asin, h0)
    z = z_r.reshape(g, n_seq, t_len, c).transpose(1, 2, 0, 3).reshape(n_seq * t_len, g * c)
    return z, h_last.transpose(1, 0, 2)


def _s5_seq(hn, n_seq, t_len, tabs, chunk):
    bd, bct, cpt, acos, asin = tabs
    m, d = hn.shape
    nt = bd.shape[0]
    p2 = bct.shape[3]
    gl = acos.shape[2] // p2
    c = LANES // gl
    n_chunk = t_len // chunk
    assert p2 == LANES and d == nt * LANES
    assert chunk % 2 == 0
    tab = lambda a: pl.BlockSpec((None,) + a.shape[1:], lambda j: (j,) + (0,) * (a.ndim - 1))
    z, h_last = pl.pallas_call(
        functools.partial(_s5_seq_body, chunk=chunk, n_chunk=n_chunk, n_seq=n_seq, c=c),
        grid=(nt,),
        in_specs=[pl.BlockSpec((m, LANES), lambda j: (0, j)),
                  tab(bd), tab(bct), tab(cpt), tab(acos), tab(asin)],
        out_specs=[pl.BlockSpec((m, LANES), lambda j: (0, j)),
                   pl.BlockSpec((None, n_seq, gl * p2), lambda j: (j, 0, 0))],
        out_shape=[jax.ShapeDtypeStruct((m, d), F32),
                   jax.ShapeDtypeStruct((nt, n_seq, gl * p2), F32)],
        scratch_shapes=[pltpu.VMEM((chunk * LANES, chunk * LANES), BF16),
                        pltpu.VMEM((chunk * LANES, gl * p2), BF16),
                        pltpu.VMEM((gl * p2, chunk * LANES), BF16),
                        pltpu.VMEM((n_seq * n_chunk, gl * p2), F32)],
        compiler_params=_cparams(1), name="s5_seq")(hn, bd, bct, cpt, acos, asin)
    h_last = h_last.reshape(nt, n_seq, gl, p2).transpose(1, 0, 2, 3).reshape(n_seq, nt * gl, p2)
    return z, h_last


def _glu_epilogue(accs, tile_refs, const_refs):
    a, gate = accs
    return [tile_refs[0][...] + a * jax.nn.sigmoid(gate)]


def _s5_layer(y, gain, n_seq, t_len, tabs, h0, w_glu, layer, chunk):
    m, d = y.shape
    if h0 is None:
        z, h_last = _s5_seq(_norm(y, gain), n_seq, t_len, tabs, chunk)
    else:
        z, h_last = _s5_step(_norm(y, gain), n_seq, t_len, tabs, h0)
    (out,) = _fused_matmul("s5_glu", m, d, d, [(z, "rows")], [],
                           [(w_glu, layer, 0), (w_glu, layer, d)], [y], [F32],
                           _ident_prologue, _glu_epilogue)
    p = h_last.shape[2] // 2
    return out, h_last[..., :p], h_last[..., p:]


def _conv_in_body(*refs, tm, t_len, has_buf):
    if has_buf:
        x_ref, g_ref, wb_ref, wc_ref, wv_ref, dw_ref, b0_ref, b1_ref, o_ref, cv_ref, lhs_ref = refs
    else:
        x_ref, g_ref, wb_ref, wc_ref, wv_ref, dw_ref, o_ref, tail_ref, lhs_ref, carry_ref = refs
    i, j = pl.program_id(0), pl.program_id(1)

    @pl.when(j == 0)
    def _():
        lhs_ref[...] = _rms(x_ref[...], g_ref[...]).astype(BF16)

    lhs = lhs_ref[...]
    gate_b, gate_c, v = (jnp.dot(lhs, w[...], preferred_element_type=F32)
                         for w in (wb_ref, wc_ref, wv_ref))
    cv = gate_c * v
    w = dw_ref[...]
    t = _mod(i * tm + _iota((tm, 1), 0), t_len)
    if has_buf:
        b0, b1 = b0_ref[...], b1_ref[...]
        r1 = jnp.where(t >= 1, pltpu.roll(cv, 1, axis=0), b1)
        r2 = jnp.where(t >= 2, pltpu.roll(cv, 2, axis=0), jnp.where(t == 1, b1, b0))
        cv_ref[...] = cv
    else:
        @pl.when(i == 0)
        def _():
            carry_ref[j] = jnp.zeros(carry_ref.shape[1:], F32)

        full = jnp.concatenate([carry_ref[j], cv], axis=0)
        r1 = jnp.where(t >= 1, pltpu.roll(full, 1, axis=0)[SUBLANES:], 0.0)
        r2 = jnp.where(t >= 2, pltpu.roll(full, 2, axis=0)[SUBLANES:], 0.0)
        carry_ref[j] = cv[tm - SUBLANES:]
        tail_ref[...] = cv[tm - SUBLANES:]
    o_ref[...] = (gate_b * (w[2:3] * cv + w[1:2] * r1 + w[0:1] * r2)).astype(o_ref.dtype)


def _conv_in(y, gain, n_seq, t_len, buf, w_in, w_dw, layer):
    m, d = y.shape
    has_buf = buf is not None
    tm, tn = (m if has_buf else min(1024, t_len)), min(512, d)
    assert t_len >= 2 and m <= 1024 if has_buf else t_len % tm == 0 and tm >= SUBLANES
    nj = d // tn
    col = lambda k: pl.BlockSpec((None, d, tn), lambda i, j, k=k: (layer, 0, j + k * nj))
    tile = pl.BlockSpec((tm, tn), lambda i, j: (i, j))
    in_specs = [pl.BlockSpec((tm, d), lambda i, j: (i, 0)), pl.BlockSpec((1, d), lambda i, j: (0, 0)),
                col(0), col(1), col(2), pl.BlockSpec((None, 3, tn), lambda i, j: (layer, 0, j))]
    args = [y, gain.reshape(1, d), w_in, w_in, w_in, w_dw]
    scratch = [pltpu.VMEM((tm, d), BF16)]
    if has_buf:
        in_specs += [tile, tile]
        args += [jnp.repeat(buf[:, k], t_len, axis=0) for k in range(2)]
        out_specs = [tile, tile]
        out_shape = [jax.ShapeDtypeStruct((m, d), BF16), jax.ShapeDtypeStruct((m, d), F32)]
    else:
        out_specs = [tile, pl.BlockSpec((None, SUBLANES, tn), lambda i, j: (i, 0, j))]
        out_shape = [jax.ShapeDtypeStruct((m, d), BF16),
                     jax.ShapeDtypeStruct((m // tm, SUBLANES, d), F32)]
        scratch.append(pltpu.VMEM((nj, SUBLANES, tn), F32))
    g, extra = pl.pallas_call(
        functools.partial(_conv_in_body, tm=tm, t_len=t_len, has_buf=has_buf),
        grid=(m // tm, nj), in_specs=in_specs, out_specs=out_specs, out_shape=out_shape,
        scratch_shapes=scratch, compiler_params=_cparams(2), name="conv_in")(*args)
    if has_buf:
        return g, extra.reshape(n_seq, t_len, d)[:, t_len - 2:]
    per_seq = t_len // tm
    return g, extra.reshape(n_seq, per_seq, SUBLANES, d)[:, per_seq - 1, SUBLANES - 2:]


def _resid_epilogue(accs, tile_refs, const_refs):
    return [tile_refs[0][...] + accs[0]]


def _conv_layer(y, gain, n_seq, t_len, buf, w_in, w_dw, w_out, layer):
    m, d = y.shape
    g, state = _conv_in(y, gain, n_seq, t_len, buf, w_in, w_dw, layer)
    (out,) = _fused_matmul("conv_out", m, d, d, [(g, "rows")], [], [(w_out, layer, 0)], [y],
                           [F32], _ident_prologue, _resid_epilogue)
    return out, state


LOG2_E = math.log2(math.e)


def _sb_logs(z, mask):
    sp = jnp.log2(1.0 + jnp.exp2(-jnp.abs(z)))
    log_b = jnp.minimum(z, 0.0) - sp
    log_1m = log_b - z
    if mask is not None:
        log_1m = jnp.where(mask, log_1m, 0.0)
    return log_b, log_1m


def _sb_suffix(log_1m, upper):
    hi = log_1m.astype(BF16)
    lo = (log_1m - hi.astype(F32)).astype(BF16)
    return (jnp.dot(hi, upper, preferred_element_type=F32)
            + jnp.dot(lo, upper, preferred_element_type=F32))


def _sb_finish(log_b, log_1m, suffix, mask, run):
    a = jnp.exp2(log_b + suffix + run)
    if mask is not None:
        a = jnp.where(mask, a, 0.0)
    return a, run + jnp.sum(log_1m, axis=-1, keepdims=True)


def _sb_weights(z, mask, upper, run):
    log_b, log_1m = _sb_logs(z, mask)
    return _sb_finish(log_b, log_1m, _sb_suffix(log_1m, upper), mask, run)


def _later_key_matrix(tk):
    return (_iota((tk, tk), 0) > _iota((tk, tk), 1)).astype(BF16)


def _qk(q, k_blk):
    return lax.dot_general(q, k_blk, (((1,), (1,)), ((), ())), preferred_element_type=F32)


def _attn_body(bias_ref, q_ref, k_ref, v_ref, o_ref, z_scr, acc_scr, run_scr, *, tq, tr, dh, hps):
    hg = pl.program_id(1)
    qi = pl.program_id(2)
    upper = _later_key_matrix(tq)
    units = [(hh, r0) for hh in range(hps) for r0 in range(0, tq, tr)]
    lanes = lambda hh: slice(hh * dh, (hh + 1) * dh)

    def logits_into(slot, kb):
        start = pl.multiple_of(kb * tq, tq)
        for u, (hh, r0) in enumerate(units):
            z_scr[slot, u] = (_qk(q_ref[r0:r0 + tr, lanes(hh)], k_ref[pl.ds(start, tq), lanes(hh)])
                              + bias_ref[hg * hps + hh])

    def tile(slot, kb, masked, next_kb):
        start = pl.multiple_of(kb * tq, tq)
        masks = [(_iota((tr, tq), 1) < _iota((tr, tq), 0) + r0) if masked else None
                 for _, r0 in units]
        zs = [z_scr[slot, u] for u in range(len(units))]
        if next_kb is not None:
            logits_into(1 - slot, next_kb)
        logs = [_sb_logs(z, m) for z, m in zip(zs, masks)]
        sufs = [_sb_suffix(l1m, upper) for _, l1m in logs]
        fins = [_sb_finish(lb, l1m, suf, m, run_scr[u][:, 0:1])
                for u, ((lb, l1m), suf, m) in enumerate(zip(logs, sufs, masks))]
        for u, ((a, run), (hh, _)) in enumerate(zip(fins, units)):
            acc_scr[u] += jnp.dot(a.astype(BF16), v_ref[pl.ds(start, tq), lanes(hh)],
                                  preferred_element_type=F32)
            run_scr[u] = jnp.broadcast_to(run, run_scr.shape[1:])

    acc_scr[...] = jnp.zeros(acc_scr.shape, F32)
    run_scr[...] = jnp.zeros(run_scr.shape, F32)
    logits_into(0, qi)
    tile(0, qi, True, jnp.maximum(qi - 1, 0))

    def pair(p, carry):
        kb = qi - 1 - 2 * p
        tile(1, kb, False, jnp.maximum(kb - 1, 0))
        tile(0, kb - 1, False, jnp.maximum(kb - 2, 0))
        return carry

    lax.fori_loop(0, qi // 2, pair, 0)

    @pl.when(qi % 2 == 1)
    def _():
        tile(1, 0, False, None)

    for u, (hh, r0) in enumerate(units):
        o_ref[r0:r0 + tr, lanes(hh)] = acc_scr[u].astype(o_ref.dtype)


def _attn_prompt(q, k, v, bias, n_seq, t_len, n_heads):
    m, d = q.shape
    dh = d // n_heads
    tq = min(256, t_len)
    hps = next(h for h in (4, 2, 1) if n_heads % h == 0)
    qb = t_len // tq
    tr = min(128, tq)
    n_units = hps * (tq // tr)
    grid_spec = pltpu.PrefetchScalarGridSpec(
        num_scalar_prefetch=1, grid=(n_seq, n_heads // hps, qb),
        in_specs=[pl.BlockSpec((tq, hps * dh), lambda n, h, i, b: (n * qb + i, h)),
                  pl.BlockSpec((t_len, hps * dh), lambda n, h, i, b: (n, h)),
                  pl.BlockSpec((t_len, hps * dh), lambda n, h, i, b: (n, h))],
        out_specs=pl.BlockSpec((tq, hps * dh), lambda n, h, i, b: (n * qb + i, h)),
        scratch_shapes=[pltpu.VMEM((2, n_units, tr, tq), F32), pltpu.VMEM((n_units, tr, dh), F32),
                        pltpu.VMEM((n_units, tr, LANES), F32)])
    return pl.pallas_call(
        functools.partial(_attn_body, tq=tq, tr=tr, dh=dh, hps=hps), grid_spec=grid_spec,
        out_shape=jax.ShapeDtypeStruct((m, d), BF16),
        compiler_params=_cparams(3), name="sb_attn_prompt")(bias, q, k, v)


def _decode_body(pt_ref, q_ref, bias_ref, e_ref, et_ref, hm_ref, hm32_ref, *refs, n_heads, ppt):
    kn_refs, vn_refs = refs[:ppt], refs[ppt:2 * ppt]
    kc_refs, vc_refs = refs[2 * ppt:3 * ppt], refs[3 * ppt:4 * ppt]
    o_ref, run_ref, acc_ref = refs[4 * ppt:]
    step = pl.program_id(1)
    page = vn_refs[0].shape[0] // n_heads
    rows = q_ref.shape[0]
    tk = ppt * page

    def tile(k_refs, v_refs, mask, run, acc):
        parts = []
        for slot in range(ppt):
            z_rows = _qk(q_ref[...], k_refs[slot][...].astype(BF16)) * hm32_ref[...]
            hi = z_rows.astype(BF16)
            parts += [hi, (z_rows - hi.astype(F32)).astype(BF16)]
        zc = jnp.dot(jnp.concatenate(parts, axis=0), et_ref[...], preferred_element_type=F32)
        zs = [zc[2 * s * rows:(2 * s + 1) * rows] + zc[(2 * s + 1) * rows:(2 * s + 2) * rows]
              for s in range(ppt)]
        z = (zs[0] if ppt == 1 else jnp.concatenate(zs, axis=1)) + bias_ref[...]
        a, run = _sb_weights(z, mask, _later_key_matrix(tk), run)
        a = a.astype(BF16)
        a_st = jnp.concatenate([a[:, s * page:(s + 1) * page] for s in range(ppt)], axis=0)
        a_rows = jnp.dot(a_st, e_ref[...], preferred_element_type=F32).astype(BF16)
        for slot in range(ppt):
            acc = acc + jnp.dot(a_rows[slot * rows:(slot + 1) * rows] * hm_ref[...],
                                v_refs[slot][...].astype(BF16), preferred_element_type=F32)
        run_ref[...] = jnp.broadcast_to(run, run_ref.shape)
        acc_ref[...] = acc

    @pl.when(step == 0)
    def _():
        mask = _iota((rows, tk), 1) < _div(_iota((rows, tk), 0), n_heads)
        tile(kn_refs, vn_refs, mask, jnp.zeros((rows, 1), F32), jnp.zeros(acc_ref.shape, F32))

    @pl.when(step > 0)
    def _():
        tile(kc_refs, vc_refs, None, run_ref[:, 0:1], acc_ref[...])

    @pl.when(step == pl.num_programs(1) - 1)
    def _():
        o_ref[...] = acc_ref[...]


def _attn_decode(q, k_new, v_new, bias, cache_k, cache_v, layer, page_table, n_seq, t_len,
                 n_heads):
    d = q.shape[1]
    dh = d // n_heads
    n_layers, n_pool, page = cache_k.shape[:3]
    n_pages = page_table.shape[1]
    ppt = next(p for p in (4, 2, 1) if n_pages % p == 0)
    n_tiles = n_pages // ppt
    rows = t_len * n_heads
    assert n_heads % SUBLANES == 0 and t_len <= page
    q_rows = q.reshape(n_seq, rows, dh)
    bias_rows = jnp.tile(bias, t_len).reshape(rows, 1)
    key_of_row = jnp.arange(page * n_heads) // n_heads
    expand = (key_of_row[None, :] == jnp.arange(page)[:, None]).astype(BF16)
    head_of_row = jnp.arange(page * n_heads) % n_heads
    head_mask = head_of_row[None, :] == (jnp.arange(rows) % n_heads)[:, None]
    pad_t = ((0, 0), (0, page - t_len), (0, 0), (0, 0))
    as_rows = lambda x: jnp.pad(x.reshape(n_seq, t_len, n_heads, dh), pad_t).reshape(
        n_seq, page * n_heads, dh)
    kn, vn = as_rows(k_new), as_rows(v_new)
    kc = cache_k.reshape(n_layers, n_pool, page * n_heads, dh)
    vc = cache_v.reshape(n_layers, n_pool, page * n_heads, dh)

    def phys(n, s, pt, slot):
        return pt[n, (n_tiles - jnp.maximum(s, 1)) * ppt + slot]

    const = lambda shape: pl.BlockSpec(shape, lambda n, s, pt: (0,) * len(shape))
    in_specs = [pl.BlockSpec((None, rows, dh), lambda n, s, pt: (n, 0, 0)),
                const((rows, 1)), const(expand.shape), const(expand.shape[::-1]),
                const(head_mask.shape), const(head_mask.shape)]
    args = [q_rows, bias_rows, expand, expand.T, head_mask.astype(BF16), head_mask.astype(F32)]
    blk = (None, None, page * n_heads, dh)
    for arr in (kn, vn):
        for slot in range(ppt):
            in_specs.append(pl.BlockSpec(blk[1:], lambda n, s, pt: (n, 0, 0)))
            args.append(arr)
    for arr in (kc, vc):
        for slot in range(ppt):
            in_specs.append(pl.BlockSpec(
                blk, lambda n, s, pt, slot=slot: (layer, phys(n, s, pt, slot), 0, 0)))
            args.append(arr)
    grid_spec = pltpu.PrefetchScalarGridSpec(
        num_scalar_prefetch=1, grid=(n_seq, n_tiles + 1), in_specs=in_specs,
        out_specs=pl.BlockSpec((None, rows, dh), lambda n, s, pt: (n, 0, 0)),
        scratch_shapes=[pltpu.VMEM((rows, LANES), F32), pltpu.VMEM((rows, dh), F32)])
    o = pl.pallas_call(
        functools.partial(_decode_body, n_heads=n_heads, ppt=ppt), grid_spec=grid_spec,
        out_shape=jax.ShapeDtypeStruct((n_seq, rows, dh), F32),
        compiler_params=_cparams(2), name="sb_attn_decode")(page_table, *args)
    return o.reshape(n_seq * t_len, d).astype(BF16)


def _qkv_body(x_ref, g_ref, qg_ref, kg_ref, w_ref, q_ref, k32_ref, kb_ref, v32_ref, vb_ref,
              lhs_ref, *, nq, q_scale):
    j = pl.program_id(1)

    @pl.when(j == 0)
    def _():
        lhs_ref[...] = _rms(x_ref[...], g_ref[...]).astype(BF16)

    acc = jnp.dot(lhs_ref[...], w_ref[...], preferred_element_type=F32)

    @pl.when(j < nq)
    def _():
        q_ref[...] = _head_norm(acc, qg_ref[...], q_scale).astype(q_ref.dtype)

    @pl.when(jnp.logical_and(j >= nq, j < 2 * nq))
    def _():
        k = _head_norm(acc, kg_ref[...], 1.0)
        k32_ref[...] = k
        kb_ref[...] = k.astype(kb_ref.dtype)

    @pl.when(j >= 2 * nq)
    def _():
        v32_ref[...] = acc
        vb_ref[...] = acc.astype(vb_ref.dtype)


def _qkv_proj(y, gain, w_qkv, layer, q_gain, k_gain, q_scale):
    m, d = y.shape
    dh = q_gain.shape[0]
    tm, tn = min(1024, m), min(512, d)
    nq = d // tn
    out = lambda kind: pl.BlockSpec((tm, tn), lambda i, j: (i, jnp.clip(j - kind * nq, 0, nq - 1)))
    small = lambda n: pl.BlockSpec((1, n), lambda i, j: (0, 0))
    return pl.pallas_call(
        functools.partial(_qkv_body, nq=nq, q_scale=q_scale), grid=(m // tm, 3 * nq),
        in_specs=[pl.BlockSpec((tm, d), lambda i, j: (i, 0)), small(d), small(dh), small(dh),
                  pl.BlockSpec((None, d, tn), lambda i, j: (layer, 0, j))],
        out_specs=[out(0), out(1), out(1), out(2), out(2)],
        out_shape=[jax.ShapeDtypeStruct((m, d), dt) for dt in (BF16, F32, BF16, F32, BF16)],
        scratch_shapes=[pltpu.VMEM((tm, d), BF16)],
        compiler_params=_cparams(2), name="attn_qkv")(
            y, gain.reshape(1, d), q_gain.reshape(1, dh), k_gain.reshape(1, dh), w_qkv)


def _attn_layer(y, gain, n_seq, t_len, n_heads, w_qkv, q_gain, k_gain, sb_bias, w_o, layer,
                cache=None):
    m, d = y.shape
    dh = d // n_heads
    sb_bias = sb_bias * LOG2_E
    q, k32, kb, v32, vb = _qkv_proj(y, gain, w_qkv, layer, q_gain, k_gain, dh ** -0.5 * LOG2_E)
    if cache is None:
        o = _attn_prompt(q, kb, vb, sb_bias, n_seq, t_len, n_heads)
    else:
        cache_k, cache_v, page_table = cache
        o = _attn_decode(q, k32, v32, sb_bias, cache_k, cache_v, layer, page_table, n_seq, t_len,
                         n_heads)
    (out,) = _fused_matmul("attn_out", m, d, d, [(o, "rows")], [], [(w_o, layer, 0)], [y], [F32],
                           _ident_prologue, _resid_epilogue)
    shape = (n_seq, t_len, n_heads, dh)
    return out, k32.reshape(shape), v32.reshape(shape)


def kernel(x_prompt, x_sample, state_ssm_re, state_ssm_im, state_conv, cache_k, cache_v, page_table, norm_mix, norm_mlp, ssm_lambda_re, ssm_lambda_im, ssm_log_dt, ssm_b_re, ssm_b_im, ssm_c_re, ssm_c_im, ssm_d, ssm_w_glu, conv_w_in, conv_w_dw, conv_w_out, attn_w_qkv, attn_q_norm, attn_k_norm, attn_sb_bias, attn_w_o, mlp_w_up, mlp_w_down):
    n_p, t_p, d = x_prompt.shape
    n_s, t_s, _ = x_sample.shape
    depth = norm_mix.shape[0]
    n_heads = attn_sb_bias.shape[1]
    assert conv_w_dw.shape[1] == 3 and d // n_heads == LANES
    chunk_p = min(S5_CHUNK, t_p)
    assert t_p % chunk_p == 0 and (t_p // chunk_p) & (t_p // chunk_p - 1) == 0

    w_glu, w_in, w_out = (w.astype(BF16) for w in (ssm_w_glu, conv_w_in, conv_w_out))
    w_qkv, w_o = attn_w_qkv.astype(BF16), attn_w_o.astype(BF16)
    w_up, w_down = mlp_w_up.astype(BF16), mlp_w_down.astype(BF16)

    y_p = x_prompt.reshape(n_p * t_p, d)
    y_s = x_sample.reshape(n_s * t_s, d)
    outs = {k: [] for k in ("re_p", "im_p", "re_s", "im_s", "cv_p", "cv_s", "k_p", "v_p", "k_s", "v_s")}
    for i in range(depth):
        kind, j = i % 3, i // 3
        if kind == 0:
            ssm = (ssm_lambda_re[j], ssm_lambda_im[j], ssm_log_dt[j], ssm_b_re[j], ssm_b_im[j],
                   ssm_c_re[j], ssm_c_im[j], ssm_d[j])
            tabs_p = _s5_tables(*ssm, chunk_p, t_p // chunk_p, True)
            tabs_s = _s5_tables(*ssm, t_s, 1, False)
            h0 = jnp.concatenate([state_ssm_re[j], state_ssm_im[j]], axis=-1).transpose(1, 0, 2)
            y_p, re_p, im_p = _s5_layer(y_p, norm_mix[i], n_p, t_p, tabs_p, None, w_glu, j, chunk_p)
            y_s, re_s, im_s = _s5_layer(y_s, norm_mix[i], n_s, t_s, tabs_s, h0, w_glu, j, t_s)
            outs["re_p"].append(re_p); outs["im_p"].append(im_p)
            outs["re_s"].append(re_s); outs["im_s"].append(im_s)
        elif kind == 1:
            y_p, cv_p = _conv_layer(y_p, norm_mix[i], n_p, t_p, None, w_in, conv_w_dw, w_out, j)
            y_s, cv_s = _conv_layer(y_s, norm_mix[i], n_s, t_s, state_conv[j], w_in, conv_w_dw,
                                    w_out, j)
            outs["cv_p"].append(cv_p); outs["cv_s"].append(cv_s)
        else:
            attn = (w_qkv, attn_q_norm[j], attn_k_norm[j], attn_sb_bias[j], w_o, j)
            y_p, k_p, v_p = _attn_layer(y_p, norm_mix[i], n_p, t_p, n_heads, *attn)
            y_s, k_s, v_s = _attn_layer(y_s, norm_mix[i], n_s, t_s, n_heads, *attn,
                                        cache=(cache_k, cache_v, page_table))
            outs["k_p"].append(k_p); outs["v_p"].append(v_p)
            outs["k_s"].append(k_s); outs["v_s"].append(v_s)
        y_p = _mlp(y_p, norm_mlp[i], w_up, w_down, i)
        y_s = _mlp(y_s, norm_mlp[i], w_up, w_down, i)
    st = lambda key: jnp.stack(outs[key])
    return (y_p.reshape(n_p, t_p, d), y_s.reshape(n_s, t_s, d),
            st("re_p"), st("im_p"), st("re_s"), st("im_s"), st("cv_p"), st("cv_s"),
            st("k_p"), st("v_p"), st("k_s"), st("v_s"))
```

```python
import functools
import math

import jax
import jax.numpy as jnp
from jax import lax
from jax.experimental import pallas as pl
from jax.experimental.pallas import tpu as pltpu

F32 = jnp.float32
BF16 = jnp.bfloat16
EPS = 1e-6
LANES = 128
SUBLANES = 8
VMEM_LIMIT_BYTES = 56 * 1024 * 1024
VMEM_BUDGET_BYTES = 44 * 1024 * 1024
S5_CHUNK = 16
HIGHEST = lax.Precision.HIGHEST


def _cparams(n_axes):
    return pltpu.CompilerParams(dimension_semantics=("arbitrary",) * n_axes,
                                vmem_limit_bytes=VMEM_LIMIT_BYTES)


def _rms(x, g):
    ms = jnp.mean(x * x, axis=-1, keepdims=True)
    return x * lax.rsqrt(ms + EPS) * g


def _iota(shape, dim):
    return lax.broadcasted_iota(jnp.int32, shape, dim)


def _mod(x, n):
    return x & (n - 1) if n & (n - 1) == 0 else lax.rem(x, n)


def _div(x, n):
    return x >> (n.bit_length() - 1) if n & (n - 1) == 0 else lax.div(x, n)


def _plan_tiles(m, kdim, n_out, row_bytes_per_row, n_w, tile_itemsizes):
    for tm, tn, n_buf in ((1024, 1024, 2), (1024, 512, 2), (1024, 1024, 1), (1024, 512, 1),
                          (512, 1024, 2), (512, 512, 2), (512, 512, 1)):
        tm, tn = min(tm, m), min(tn, n_out)
        stream = n_w * kdim * tn * 2 + tm * tn * sum(tile_itemsizes)
        fixed = tm * kdim * 2 + 2 * n_w * tm * tn * 4
        if n_buf * tm * row_bytes_per_row + 2 * stream + fixed <= VMEM_BUDGET_BYTES:
            break
    return tm, tn, n_buf


def _fused_matmul(name, m, kdim, n_out, row_in, const_in, weights, tile_in, out_dtypes,
                  prologue, epilogue):
    n_row, n_const, n_w, n_tile, n_o = (len(row_in), len(const_in), len(weights),
                                        len(tile_in), len(out_dtypes))
    size = lambda dt: jnp.dtype(dt).itemsize
    tm, tn, n_buf = _plan_tiles(
        m, kdim, n_out, sum(a.shape[1] * size(a.dtype) for a, kind in row_in if kind == "rows"),
        n_w, [size(a.dtype) for a in tile_in] + [size(dt) for dt in out_dtypes])
    assert m % tm == 0 and n_out % tn == 0
    row_mode = pl.Buffered(1) if n_buf == 1 else None
    per = tm // SUBLANES
    in_specs, args = [], []
    for arr, kind in row_in:
        if kind == "rows":
            spec = pl.BlockSpec((tm, arr.shape[1]), lambda i, j: (i, 0), pipeline_mode=row_mode)
        else:
            spec = pl.BlockSpec((SUBLANES, arr.shape[1]),
                                lambda i, j: (jnp.maximum(i * per - 1, 0), 0))
        in_specs.append(spec)
        args.append(arr)
    for arr in const_in:
        in_specs.append(pl.BlockSpec(arr.shape, lambda i, j, nd=arr.ndim: (0,) * nd))
        args.append(arr)
    for arr, layer, col in weights:
        in_specs.append(pl.BlockSpec((None, kdim, tn),
                                     lambda i, j, layer=layer, off=col // tn: (layer, 0, j + off)))
        args.append(arr)
    for arr in tile_in:
        in_specs.append(pl.BlockSpec((tm, tn), lambda i, j: (i, j)))
        args.append(arr)
    out_shape = [jax.ShapeDtypeStruct((m, n_out), dt) for dt in out_dtypes]
    out_specs = [pl.BlockSpec((tm, tn), lambda i, j: (i, j)) for _ in out_dtypes]

    def body(*refs):
        p = 0
        row_refs = refs[p:p + n_row]; p += n_row
        const_refs = refs[p:p + n_const]; p += n_const
        w_refs = refs[p:p + n_w]; p += n_w
        tile_refs = refs[p:p + n_tile]; p += n_tile
        out_refs = refs[p:p + n_o]; p += n_o
        lhs_ref = refs[p]

        @pl.when(pl.program_id(1) == 0)
        def _():
            lhs_ref[...] = prologue(row_refs, const_refs, tm).astype(BF16)

        slab = min(256, tm)
        for r0 in range(0, tm, slab):
            rows = slice(r0, r0 + slab)
            lhs = lhs_ref[rows, :]
            accs = [jnp.dot(lhs, w[...], preferred_element_type=F32) for w in w_refs]
            tiles = [t.at[rows, :] for t in tile_refs]
            for o_ref, val in zip(out_refs, epilogue(accs, tiles, const_refs)):
                o_ref[rows, :] = val.astype(o_ref.dtype)

    return pl.pallas_call(
        body, grid=(m // tm, n_out // tn), in_specs=in_specs, out_specs=out_specs,
        out_shape=out_shape, scratch_shapes=[pltpu.VMEM((tm, kdim), BF16)],
        compiler_params=_cparams(2), name=name)(*args)


def _ident_prologue(row_refs, const_refs, tm):
    return row_refs[0][...]


def _head_norm(acc, gain, scale):
    segs = []
    for h in range(acc.shape[1] // LANES):
        seg = acc[:, h * LANES:(h + 1) * LANES]
        ms = jnp.mean(seg * seg, axis=-1, keepdims=True)
        y = seg * lax.rsqrt(ms + EPS) * gain
        segs.append(y * scale if scale != 1.0 else y)
    return segs[0] if len(segs) == 1 else jnp.concatenate(segs, axis=-1)


def _mlp_body(x_ref, g_ref, wu_ref, wd_ref, o_ref, xn_ref, acc_ref):
    f = pl.program_id(1)

    @pl.when(f == 0)
    def _():
        xn_ref[...] = _rms(x_ref[...], g_ref[...]).astype(BF16)
        acc_ref[...] = jnp.zeros_like(acc_ref)

    h = jnp.maximum(jnp.dot(xn_ref[...], wu_ref[...], preferred_element_type=F32), 0.0)
    acc_ref[...] += jnp.dot((h * h).astype(BF16), wd_ref[...], preferred_element_type=F32)

    @pl.when(f == pl.num_programs(1) - 1)
    def _():
        o_ref[...] = x_ref[...] + acc_ref[...]


def _mlp(x, gain, w_up, w_down, layer):
    m, d = x.shape
    ff = w_up.shape[2]
    tm = min(512, m)
    tf = min(512 if m > 512 else 2048, ff)
    assert m % tm == 0 and ff % tf == 0
    return pl.pallas_call(
        _mlp_body, grid=(m // tm, ff // tf),
        in_specs=[pl.BlockSpec((tm, d), lambda i, f: (i, 0)),
                  pl.BlockSpec((1, d), lambda i, f: (0, 0)),
                  pl.BlockSpec((None, d, tf), lambda i, f: (layer, 0, f)),
                  pl.BlockSpec((None, tf, d), lambda i, f: (layer, f, 0))],
        out_specs=pl.BlockSpec((tm, d), lambda i, f: (i, 0)),
        out_shape=jax.ShapeDtypeStruct((m, d), F32),
        scratch_shapes=[pltpu.VMEM((tm, d), BF16), pltpu.VMEM((tm, d), F32)],
        compiler_params=_cparams(2), name="mlp")(x, gain.reshape(1, d), w_up, w_down)


def _norm_body(x_ref, g_ref, o_ref):
    o_ref[...] = _rms(x_ref[...], g_ref[...])


def _norm(x, gain):
    m, d = x.shape
    tm = min(512, m)
    assert m % tm == 0
    return pl.pallas_call(
        _norm_body, grid=(m // tm,),
        in_specs=[pl.BlockSpec((tm, d), lambda i: (i, 0)), pl.BlockSpec((1, d), lambda i: (0, 0))],
        out_specs=pl.BlockSpec((tm, d), lambda i: (i, 0)),
        out_shape=jax.ShapeDtypeStruct((m, d), F32),
        compiler_params=_cparams(1), name="rmsnorm")(x, gain.reshape(1, d))


def _complex_scale(acos, asin, h):
    return acos * h + asin * pltpu.roll(h, h.shape[1] // 2, axis=1)


def _gelu(y):
    return y * (0.5 * (1.0 + jnp.tanh(math.sqrt(2.0 / math.pi) * (y + 0.044715 * (y * y * y)))))


def _s5_step_body(u_ref, m_ref, bc_ref, cp_ref, d_ref, acos_ref, asin_ref, h0_ref, z_ref, hl_ref,
                  *, gt):
    def one_group(g, carry):
        u = u_ref[g]
        ub = u.astype(BF16)
        h_prev = h0_ref[g]
        hl_ref[g] = (jnp.dot(ub, bc_ref[g], preferred_element_type=F32)
                     + _complex_scale(acos_ref[g][0:1], asin_ref[g][0:1], h_prev))
        y = (jnp.dot(ub, m_ref[g], preferred_element_type=F32)
             + jnp.dot(h_prev.astype(BF16), cp_ref[g], preferred_element_type=F32)
             + u * d_ref[g])
        z_ref[g] = _gelu(y).astype(z_ref.dtype)
        return carry

    lax.fori_loop(0, gt, one_group, 0)


def _s5_seq_body(x_ref, bd_ref, bct_ref, cpt_ref, acos_ref, asin_ref, z_ref, hl_ref,
                 m8_ref, bc8_ref, cp8_ref, h_ref, *, chunk, n_chunk, n_seq, c):
    gl = LANES // c
    rows = n_seq * n_chunk
    blk = lambda i: slice(i * LANES, (i + 1) * LANES)
    grp_of_lane = _div(_iota((c, LANES), 1), c)
    for tau in range(chunk):
        kt = bd_ref[tau]
        m8_ref[blk(0), blk(tau)] = jnp.concatenate(
            [jnp.where(grp_of_lane == g, kt, 0.0) for g in range(gl)], axis=0).astype(BF16)
    for s in range(1, chunk):
        for t in range(s, chunk):
            m8_ref[blk(s), blk(t)] = m8_ref[blk(0), blk(t - s)]
        if s % 2:
            m8_ref[blk(s), blk(s - 1)] = jnp.zeros((LANES, LANES), BF16)
    row_grp = _div(_iota((LANES, LANES), 0), c)
    lane_grp = _div(_iota((LANES, LANES), 1), c)
    for s in range(chunk):
        for g in range(gl):
            bc8_ref[blk(s), blk(g)] = jnp.where(row_grp == g, bct_ref[s], 0.0).astype(BF16)
            cp8_ref[blk(g), blk(s)] = jnp.where(lane_grp == g, cpt_ref[s], 0.0).astype(BF16)

    u2 = jnp.concatenate([x_ref[pl.ds(s, rows, stride=chunk), :].astype(BF16)
                          for s in range(chunk)], axis=1)
    h = jnp.dot(u2, bc8_ref[...], preferred_element_type=F32)
    kidx = _iota((rows, 1), 0) & (n_chunk - 1)
    swap = lambda v: jnp.concatenate(
        [pltpu.roll(v[:, blk(g)], LANES // 2, axis=1) for g in range(gl)], axis=1)
    shift, si = 1, 0
    while shift < n_chunk:
        sh = jnp.where(kidx >= shift, pltpu.roll(h, shift, axis=0), 0.0)
        h = h + acos_ref[si:si + 1, :] * sh + asin_ref[si:si + 1, :] * swap(sh)
        shift, si = shift * 2, si + 1
    h_ref[...] = h
    h_prev = jnp.where(kidx >= 1, pltpu.roll(h, 1, axis=0), 0.0).astype(BF16)
    for t in range(0, chunk, 2):
        cols = slice(t * LANES, (t + 2) * LANES)
        y = (jnp.dot(u2[:, :(t + 2) * LANES], m8_ref[:(t + 2) * LANES, cols],
                     preferred_element_type=F32)
             + jnp.dot(h_prev, cp8_ref[:, cols], preferred_element_type=F32))
        z = _gelu(y)
        z_ref[pl.ds(t, rows, stride=chunk), :] = z[:, :LANES]
        z_ref[pl.ds(t + 1, rows, stride=chunk), :] = z[:, LANES:]
    for n in range(n_seq):
        hl_ref[pl.ds(n, 1), :] = h_ref[pl.ds((n + 1) * n_chunk - 1, 1), :]


def _s5_tables(lam_re, lam_im, log_dt, b_re, b_im, c_re, c_im, d, chunk, n_chunk, per_tile):
    g, p = lam_re.shape
    c = b_re.shape[2]
    dt = jnp.exp(log_dt)[:, None]
    xr, xi = lam_re * dt, lam_im * dt

    def a_pow(ks):
        kk = jnp.asarray(ks, F32)[None, :, None]
        mag = jnp.exp(xr[:, None, :] * kk)
        return mag * jnp.cos(xi[:, None, :] * kk), mag * jnp.sin(xi[:, None, :] * kk)

    pr, pi = a_pow(jnp.arange(chunk + 1))
    ar, ai = pr[:, 1], pi[:, 1]
    den = lam_re * lam_re + lam_im * lam_im
    qr = ((ar - 1.0) * lam_re + ai * lam_im) / den
    qi = (ai * lam_re - (ar - 1.0) * lam_im) / den
    bb_re = qr[..., None] * b_re - qi[..., None] * b_im
    bb_im = qr[..., None] * b_im + qi[..., None] * b_re
    w_re = c_re[:, None] * pr[:, :chunk, None] - c_im[:, None] * pi[:, :chunk, None]
    w_im = c_re[:, None] * pi[:, :chunk, None] + c_im[:, None] * pr[:, :chunk, None]
    kt = jnp.einsum("gtoq,gqi->gtoi", jnp.concatenate([w_re, -w_im], axis=-1),
                    jnp.concatenate([bb_re, bb_im], axis=1), precision=HIGHEST)
    pair = lambda x, y: jnp.concatenate([x, y], axis=-1)
    rr, ri = pr[:, :chunk][:, ::-1], pi[:, :chunk][:, ::-1]
    bt_re, bt_im = bb_re.transpose(0, 2, 1), bb_im.transpose(0, 2, 1)
    bc = (pair(rr, rr)[:, :, None] * pair(bt_re, bt_im)[:, None]
          + pair(-ri, ri)[:, :, None] * pair(bt_im, bt_re)[:, None])
    shifts = [1]
    while shifts[-1] * 2 < n_chunk:
        shifts.append(shifts[-1] * 2)
    while len(shifts) < SUBLANES:
        shifts.append(shifts[-1])
    sr, si = a_pow(chunk * jnp.asarray(shifts, F32))
    acos, asin = pair(sr, sr), pair(-si, si)
    if not per_tile:
        kpad = jnp.concatenate([kt, jnp.zeros((g, 1, c, c), F32)], axis=1)
        s_idx = jnp.arange(chunk)[:, None]
        t_idx = jnp.arange(chunk)[None, :]
        tau = jnp.where(t_idx >= s_idx, t_idx - s_idx, chunk)
        mmat = kpad[:, tau].transpose(0, 1, 4, 2, 3).reshape(g, chunk * c, chunk * c)
        cr, ci = c_re.transpose(0, 2, 1)[:, :, None], c_im.transpose(0, 2, 1)[:, :, None]
        nr, ni = pr[:, 1:].transpose(0, 2, 1)[..., None], pi[:, 1:].transpose(0, 2, 1)[..., None]
        cp = jnp.concatenate([cr * nr - ci * ni, -(cr * ni + ci * nr)], axis=1)
        dtile = jnp.tile(d.reshape(g, 1, c), (1, chunk, 1)).reshape(g, 1, chunk * c)
        return (mmat.astype(BF16), bc.reshape(g, chunk * c, 2 * p).astype(BF16),
                cp.reshape(g, 2 * p, chunk * c).astype(BF16), dtile, acos, asin)
    gl = LANES // c
    nt = g // gl
    p2 = 2 * p
    kd = kt.at[:, 0].add(d.reshape(g, c)[:, :, None] * jnp.eye(c, dtype=F32))
    bd = kd.reshape(nt, gl, chunk, c, c).transpose(0, 2, 4, 1, 3).reshape(nt, chunk, c, LANES)
    bct = bc.reshape(nt, gl, chunk, c, p2).transpose(0, 2, 1, 3, 4).reshape(nt, chunk, LANES, p2)
    on_lanes = lambda x: x.reshape(nt, gl, c, p).transpose(0, 3, 1, 2).reshape(nt, 1, p, LANES)
    spread = lambda x: jnp.repeat(
        x[:, 1:].reshape(nt, gl, chunk, p).transpose(0, 2, 3, 1), c, axis=-1)
    cr, ci, nr, ni = on_lanes(c_re), on_lanes(c_im), spread(pr), spread(pi)
    cpt = jnp.concatenate([cr * nr - ci * ni, -(cr * ni + ci * nr)], axis=2)
    tile = lambda a: a.reshape(nt, gl, SUBLANES, p2).transpose(0, 2, 1, 3).reshape(
        nt, SUBLANES, gl * p2)
    return bd, bct, cpt, tile(acos), tile(asin)


def _s5_step(hn, n_seq, t_len, tabs, h0):
    mmat, bc, cp, dt, acos, asin = tabs
    g, lc = mmat.shape[:2]
    p2 = bc.shape[2]
    c = lc // t_len
    u_r = hn.reshape(n_seq, t_len, g, c).transpose(2, 0, 1, 3).reshape(g, n_seq, lc)
    gt = min(8, g)
    grp = lambda shape: pl.BlockSpec((gt,) + shape, lambda i: (i, 0, 0))
    z_r, h_last = pl.pallas_call(
        functools.partial(_s5_step_body, gt=gt), grid=(g // gt,),
        in_specs=[grp((n_seq, lc)), grp((lc, lc)), grp((lc, p2)), grp((p2, lc)), grp((1, lc)),
                  grp((SUBLANES, p2)), grp((SUBLANES, p2)), grp((n_seq, p2))],
        out_specs=[grp((n_seq, lc)), grp((n_seq, p2))],
        out_shape=[jax.ShapeDtypeStruct((g, n_seq, lc), BF16),
                   jax.ShapeDtypeStruct((g, n_seq, p2), F32)],
        compiler_params=_cparams(1), name="s5_step")(u_r, mmat, bc, cp, dt, acos, asin, h0)
    z = z_r.reshape(g, n_seq, t_len, c).transpose(1, 2, 0, 3).reshape(n_seq * t_len, g * c)
    return z, h_last.transpose(1, 0, 2)


def _s5_seq(hn, n_seq, t_len, tabs, chunk):
    bd, bct, cpt, acos, asin = tabs
    m, d = hn.shape
    nt = bd.shape[0]
    p2 = bct.shape[3]
    gl = acos.shape[2] // p2
    c = LANES // gl
    n_chunk = t_len // chunk
    assert p2 == LANES and d == nt * LANES
    assert chunk % 2 == 0
    tab = lambda a: pl.BlockSpec((None,) + a.shape[1:], lambda j: (j,) + (0,) * (a.ndim - 1))
    z, h_last = pl.pallas_call(
        functools.partial(_s5_seq_body, chunk=chunk, n_chunk=n_chunk, n_seq=n_seq, c=c),
        grid=(nt,),
        in_specs=[pl.BlockSpec((m, LANES), lambda j: (0, j)),
                  tab(bd), tab(bct), tab(cpt), tab(acos), tab(asin)],
        out_specs=[pl.BlockSpec((m, LANES), lambda j: (0, j)),
                   pl.BlockSpec((None, n_seq, gl * p2), lambda j: (j, 0, 0))],
        out_shape=[jax.ShapeDtypeStruct((m, d), F32),
                   jax.ShapeDtypeStruct((nt, n_seq, gl * p2), F32)],
        scratch_shapes=[pltpu.VMEM((chunk * LANES, chunk * LANES), BF16),
                        pltpu.VMEM((chunk * LANES, gl * p2), BF16),
                        pltpu.VMEM((gl * p2, chunk * LANES), BF16),
                        pltpu.VMEM((n_seq * n_chunk, gl * p2), F32)],
        compiler_params=_cparams(1), name="s5_seq")(hn, bd, bct, cpt, acos, asin)
    h_last = h_last.reshape(nt, n_seq, gl, p2).transpose(1, 0, 2, 3).reshape(n_seq, nt * gl, p2)
    return z, h_last


def _glu_epilogue(accs, tile_refs, const_refs):
    a, gate = accs
    return [tile_refs[0][...] + a * jax.nn.sigmoid(gate)]


def _s5_layer(y, gain, n_seq, t_len, tabs, h0, w_glu, layer, chunk):
    m, d = y.shape
    if h0 is None:
        z, h_last = _s5_seq(_norm(y, gain), n_seq, t_len, tabs, chunk)
    else:
        z, h_last = _s5_step(_norm(y, gain), n_seq, t_len, tabs, h0)
    (out,) = _fused_matmul("s5_glu", m, d, d, [(z, "rows")], [],
                           [(w_glu, layer, 0), (w_glu, layer, d)], [y], [F32],
                           _ident_prologue, _glu_epilogue)
    p = h_last.shape[2] // 2
    return out, h_last[..., :p], h_last[..., p:]


def _conv_in_body(*refs, tm, t_len, has_buf):
    if has_buf:
        x_ref, g_ref, wb_ref, wc_ref, wv_ref, dw_ref, b0_ref, b1_ref, o_ref, cv_ref, lhs_ref = refs
    else:
        x_ref, g_ref, wb_ref, wc_ref, wv_ref, dw_ref, o_ref, tail_ref, lhs_ref, carry_ref = refs
    i, j = pl.program_id(0), pl.program_id(1)

    @pl.when(j == 0)
    def _():
        lhs_ref[...] = _rms(x_ref[...], g_ref[...]).astype(BF16)

    lhs = lhs_ref[...]
    gate_b, gate_c, v = (jnp.dot(lhs, w[...], preferred_element_type=F32)
                         for w in (wb_ref, wc_ref, wv_ref))
    cv = gate_c * v
    w = dw_ref[...]
    t = _mod(i * tm + _iota((tm, 1), 0), t_len)
    if has_buf:
        b0, b1 = b0_ref[...], b1_ref[...]
        r1 = jnp.where(t >= 1, pltpu.roll(cv, 1, axis=0), b1)
        r2 = jnp.where(t >= 2, pltpu.roll(cv, 2, axis=0), jnp.where(t == 1, b1, b0))
        cv_ref[...] = cv
    else:
        @pl.when(i == 0)
        def _():
            carry_ref[j] = jnp.zeros(carry_ref.shape[1:], F32)

        full = jnp.concatenate([carry_ref[j], cv], axis=0)
        r1 = jnp.where(t >= 1, pltpu.roll(full, 1, axis=0)[SUBLANES:], 0.0)
        r2 = jnp.where(t >= 2, pltpu.roll(full, 2, axis=0)[SUBLANES:], 0.0)
        carry_ref[j] = cv[tm - SUBLANES:]
        tail_ref[...] = cv[tm - SUBLANES:]
    o_ref[...] = (gate_b * (w[2:3] * cv + w[1:2] * r1 + w[0:1] * r2)).astype(o_ref.dtype)


def _conv_in(y, gain, n_seq, t_len, buf, w_in, w_dw, layer):
    m, d = y.shape
    has_buf = buf is not None
    tm, tn = (m if has_buf else min(1024, t_len)), min(512, d)
    assert t_len >= 2 and m <= 1024 if has_buf else t_len % tm == 0 and tm >= SUBLANES
    nj = d // tn
    col = lambda k: pl.BlockSpec((None, d, tn), lambda i, j, k=k: (layer, 0, j + k * nj))
    tile = pl.BlockSpec((tm, tn), lambda i, j: (i, j))
    in_specs = [pl.BlockSpec((tm, d), lambda i, j: (i, 0)), pl.BlockSpec((1, d), lambda i, j: (0, 0)),
                col(0), col(1), col(2), pl.BlockSpec((None, 3, tn), lambda i, j: (layer, 0, j))]
    args = [y, gain.reshape(1, d), w_in, w_in, w_in, w_dw]
    scratch = [pltpu.VMEM((tm, d), BF16)]
    if has_buf:
        in_specs += [tile, tile]
        args += [jnp.repeat(buf[:, k], t_len, axis=0) for k in range(2)]
        out_specs = [tile, tile]
        out_shape = [jax.ShapeDtypeStruct((m, d), BF16), jax.ShapeDtypeStruct((m, d), F32)]
    else:
        out_specs = [tile, pl.BlockSpec((None, SUBLANES, tn), lambda i, j: (i, 0, j))]
        out_shape = [jax.ShapeDtypeStruct((m, d), BF16),
                     jax.ShapeDtypeStruct((m // tm, SUBLANES, d), F32)]
        scratch.append(pltpu.VMEM((nj, SUBLANES, tn), F32))
    g, extra = pl.pallas_call(
        functools.partial(_conv_in_body, tm=tm, t_len=t_len, has_buf=has_buf),
        grid=(m // tm, nj), in_specs=in_specs, out_specs=out_specs, out_shape=out_shape,
        scratch_shapes=scratch, compiler_params=_cparams(2), name="conv_in")(*args)
    if has_buf:
        return g, extra.reshape(n_seq, t_len, d)[:, t_len - 2:]
    per_seq = t_len // tm
    return g, extra.reshape(n_seq, per_seq, SUBLANES, d)[:, per_seq - 1, SUBLANES - 2:]


def _resid_epilogue(accs, tile_refs, const_refs):
    return [tile_refs[0][...] + accs[0]]


def _conv_layer(y, gain, n_seq, t_len, buf, w_in, w_dw, w_out, layer):
    m, d = y.shape
    g, state = _conv_in(y, gain, n_seq, t_len, buf, w_in, w_dw, layer)
    (out,) = _fused_matmul("conv_out", m, d, d, [(g, "rows")], [], [(w_out, layer, 0)], [y],
                           [F32], _ident_prologue, _resid_epilogue)
    return out, state


LOG2_E = math.log2(math.e)


def _sb_logs(z, mask):
    sp = jnp.log2(1.0 + jnp.exp2(-jnp.abs(z)))
    log_b = jnp.minimum(z, 0.0) - sp
    log_1m = log_b - z
    if mask is not None:
        log_1m = jnp.where(mask, log_1m, 0.0)
    return log_b, log_1m


def _sb_suffix(log_1m, upper):
    hi = log_1m.astype(BF16)
    lo = (log_1m - hi.astype(F32)).astype(BF16)
    return (jnp.dot(hi, upper, preferred_element_type=F32)
            + jnp.dot(lo, upper, preferred_element_type=F32))


def _sb_finish(log_b, log_1m, suffix, mask, run):
    a = jnp.exp2(log_b + suffix + run)
    if mask is not None:
        a = jnp.where(mask, a, 0.0)
    return a, run + jnp.sum(log_1m, axis=-1, keepdims=True)


def _sb_weights(z, mask, upper, run):
    log_b, log_1m = _sb_logs(z, mask)
    return _sb_finish(log_b, log_1m, _sb_suffix(log_1m, upper), mask, run)


def _later_key_matrix(tk):
    return (_iota((tk, tk), 0) > _iota((tk, tk), 1)).astype(BF16)


def _qk(q, k_blk):
    return lax.dot_general(q, k_blk, (((1,), (1,)), ((), ())), preferred_element_type=F32)


def _attn_body(bias_ref, q_ref, k_ref, v_ref, o_ref, z_scr, acc_scr, run_scr, *, tq, tr, dh, hps):
    hg = pl.program_id(1)
    qi = pl.program_id(2)
    upper = _later_key_matrix(tq)
    units = [(hh, r0) for hh in range(hps) for r0 in range(0, tq, tr)]
    lanes = lambda hh: slice(hh * dh, (hh + 1) * dh)

    def logits_into(slot, kb):
        start = pl.multiple_of(kb * tq, tq)
        for u, (hh, r0) in enumerate(units):
            z_scr[slot, u] = (_qk(q_ref[r0:r0 + tr, lanes(hh)], k_ref[pl.ds(start, tq), lanes(hh)])
                              + bias_ref[hg * hps + hh])

    def tile(slot, kb, masked, next_kb):
        start = pl.multiple_of(kb * tq, tq)
        masks = [(_iota((tr, tq), 1) < _iota((tr, tq), 0) + r0) if masked else None
                 for _, r0 in units]
        zs = [z_scr[slot, u] for u in range(len(units))]
        if next_kb is not None:
            logits_into(1 - slot, next_kb)
        logs = [_sb_logs(z, m) for z, m in zip(zs, masks)]
        sufs = [_sb_suffix(l1m, upper) for _, l1m in logs]
        fins = [_sb_finish(lb, l1m, suf, m, run_scr[u][:, 0:1])
                for u, ((lb, l1m), suf, m) in enumerate(zip(logs, sufs, masks))]
        for u, ((a, run), (hh, _)) in enumerate(zip(fins, units)):
            acc_scr[u] += jnp.dot(a.astype(BF16), v_ref[pl.ds(start, tq), lanes(hh)],
                                  preferred_element_type=F32)
            run_scr[u] = jnp.broadcast_to(run, run_scr.shape[1:])

    acc_scr[...] = jnp.zeros(acc_scr.shape, F32)
    run_scr[...] = jnp.zeros(run_scr.shape, F32)
    logits_into(0, qi)
    tile(0, qi, True, jnp.maximum(qi - 1, 0))

    def pair(p, carry):
        kb = qi - 1 - 2 * p
        tile(1, kb, False, jnp.maximum(kb - 1, 0))
        tile(0, kb - 1, False, jnp.maximum(kb - 2, 0))
        return carry

    lax.fori_loop(0, qi // 2, pair, 0)

    @pl.when(qi % 2 == 1)
    def _():
        tile(1, 0, False, None)

    for u, (hh, r0) in enumerate(units):
        o_ref[r0:r0 + tr, lanes(hh)] = acc_scr[u].astype(o_ref.dtype)


def _attn_prompt(q, k, v, bias, n_seq, t_len, n_heads):
    m, d = q.shape
    dh = d // n_heads
    tq = min(256, t_len)
    hps = next(h for h in (4, 2, 1) if n_heads % h == 0)
    qb = t_len // tq
    tr = min(128, tq)
    n_units = hps * (tq // tr)
    grid_spec = pltpu.PrefetchScalarGridSpec(
        num_scalar_prefetch=1, grid=(n_seq, n_heads // hps, qb),
        in_specs=[pl.BlockSpec((tq, hps * dh), lambda n, h, i, b: (n * qb + i, h)),
                  pl.BlockSpec((t_len, hps * dh), lambda n, h, i, b: (n, h)),
                  pl.BlockSpec((t_len, hps * dh), lambda n, h, i, b: (n, h))],
        out_specs=pl.BlockSpec((tq, hps * dh), lambda n, h, i, b: (n * qb + i, h)),
        scratch_shapes=[pltpu.VMEM((2, n_units, tr, tq), F32), pltpu.VMEM((n_units, tr, dh), F32),
                        pltpu.VMEM((n_units, tr, LANES), F32)])
    return pl.pallas_call(
        functools.partial(_attn_body, tq=tq, tr=tr, dh=dh, hps=hps), grid_spec=grid_spec,
        out_shape=jax.ShapeDtypeStruct((m, d), BF16),
        compiler_params=_cparams(3), name="sb_attn_prompt")(bias, q, k, v)


def _decode_body(pt_ref, q_ref, bias_ref, e_ref, et_ref, hm_ref, hm32_ref, *refs, n_heads, ppt):
    kn_refs, vn_refs = refs[:ppt], refs[ppt:2 * ppt]
    kc_refs, vc_refs = refs[2 * ppt:3 * ppt], refs[3 * ppt:4 * ppt]
    o_ref, run_ref, acc_ref = refs[4 * ppt:]
    step = pl.program_id(1)
    page = vn_refs[0].shape[0] // n_heads
    rows = q_ref.shape[0]
    tk = ppt * page

    def tile(k_refs, v_refs, mask, run, acc):
        parts = []
        for slot in range(ppt):
            z_rows = _qk(q_ref[...], k_refs[slot][...].astype(BF16)) * hm32_ref[...]
            hi = z_rows.astype(BF16)
            parts += [hi, (z_rows - hi.astype(F32)).astype(BF16)]
        zc = jnp.dot(jnp.concatenate(parts, axis=0), et_ref[...], preferred_element_type=F32)
        zs = [zc[2 * s * rows:(2 * s + 1) * rows] + zc[(2 * s + 1) * rows:(2 * s + 2) * rows]
              for s in range(ppt)]
        z = (zs[0] if ppt == 1 else jnp.concatenate(zs, axis=1)) + bias_ref[...]
        a, run = _sb_weights(z, mask, _later_key_matrix(tk), run)
        a = a.astype(BF16)
        a_st = jnp.concatenate([a[:, s * page:(s + 1) * page] for s in range(ppt)], axis=0)
        a_rows = jnp.dot(a_st, e_ref[...], preferred_element_type=F32).astype(BF16)
        for slot in range(ppt):
            acc = acc + jnp.dot(a_rows[slot * rows:(slot + 1) * rows] * hm_ref[...],
                                v_refs[slot][...].astype(BF16), preferred_element_type=F32)
        run_ref[...] = jnp.broadcast_to(run, run_ref.shape)
        acc_ref[...] = acc

    @pl.when(step == 0)
    def _():
        mask = _iota((rows, tk), 1) < _div(_iota((rows, tk), 0), n_heads)
        tile(kn_refs, vn_refs, mask, jnp.zeros((rows, 1), F32), jnp.zeros(acc_ref.shape, F32))

    @pl.when(step > 0)
    def _():
        tile(kc_refs, vc_refs, None, run_ref[:, 0:1], acc_ref[...])

    @pl.when(step == pl.num_programs(1) - 1)
    def _():
        o_ref[...] = acc_ref[...]


def _attn_decode(q, k_new, v_new, bias, cache_k, cache_v, layer, page_table, n_seq, t_len,
                 n_heads):
    d = q.shape[1]
    dh = d // n_heads
    n_layers, n_pool, page = cache_k.shape[:3]
    n_pages = page_table.shape[1]
    ppt = next(p for p in (4, 2, 1) if n_pages % p == 0)
    n_tiles = n_pages // ppt
    rows = t_len * n_heads
    assert n_heads % SUBLANES == 0 and t_len <= page
    q_rows = q.reshape(n_seq, rows, dh)
    bias_rows = jnp.tile(bias, t_len).reshape(rows, 1)
    key_of_row = jnp.arange(page * n_heads) // n_heads
    expand = (key_of_row[None, :] == jnp.arange(page)[:, None]).astype(BF16)
    head_of_row = jnp.arange(page * n_heads) % n_heads
    head_mask = head_of_row[None, :] == (jnp.arange(rows) % n_heads)[:, None]
    pad_t = ((0, 0), (0, page - t_len), (0, 0), (0, 0))
    as_rows = lambda x: jnp.pad(x.reshape(n_seq, t_len, n_heads, dh), pad_t).reshape(
        n_seq, page * n_heads, dh)
    kn, vn = as_rows(k_new), as_rows(v_new)
    kc = cache_k.reshape(n_layers, n_pool, page * n_heads, dh)
    vc = cache_v.reshape(n_layers, n_pool, page * n_heads, dh)

    def phys(n, s, pt, slot):
        return pt[n, (n_tiles - jnp.maximum(s, 1)) * ppt + slot]

    const = lambda shape: pl.BlockSpec(shape, lambda n, s, pt: (0,) * len(shape))
    in_specs = [pl.BlockSpec((None, rows, dh), lambda n, s, pt: (n, 0, 0)),
                const((rows, 1)), const(expand.shape), const(expand.shape[::-1]),
                const(head_mask.shape), const(head_mask.shape)]
    args = [q_rows, bias_rows, expand, expand.T, head_mask.astype(BF16), head_mask.astype(F32)]
    blk = (None, None, page * n_heads, dh)
    for arr in (kn, vn):
        for slot in range(ppt):
            in_specs.append(pl.BlockSpec(blk[1:], lambda n, s, pt: (n, 0, 0)))
            args.append(arr)
    for arr in (kc, vc):
        for slot in range(ppt):
            in_specs.append(pl.BlockSpec(
                blk, lambda n, s, pt, slot=slot: (layer, phys(n, s, pt, slot), 0, 0)))
            args.append(arr)
    grid_spec = pltpu.PrefetchScalarGridSpec(
        num_scalar_prefetch=1, grid=(n_seq, n_tiles + 1), in_specs=in_specs,
        out_specs=pl.BlockSpec((None, rows, dh), lambda n, s, pt: (n, 0, 0)),
        scratch_shapes=[pltpu.VMEM((rows, LANES), F32), pltpu.VMEM((rows, dh), F32)])
    o = pl.pallas_call(
        functools.partial(_decode_body, n_heads=n_heads, ppt=ppt), grid_spec=grid_spec,
        out_shape=jax.ShapeDtypeStruct((n_seq, rows, dh), F32),
        compiler_params=_cparams(2), name="sb_attn_decode")(page_table, *args)
    return o.reshape(n_seq * t_len, d).astype(BF16)


def _qkv_body(x_ref, g_ref, qg_ref, kg_ref, w_ref, q_ref, k32_ref, kb_ref, v32_ref, vb_ref,
              lhs_ref, *, nq, q_scale):
    j = pl.program_id(1)

    @pl.when(j == 0)
    def _():
        lhs_ref[...] = _rms(x_ref[...], g_ref[...]).astype(BF16)

    def by_slab(write):
        tm = lhs_ref.shape[0]
        slab = min(256, tm)
        for r0 in range(0, tm, slab):
            rows = slice(r0, r0 + slab)
            write(rows, jnp.dot(lhs_ref[rows, :], w_ref[...], preferred_element_type=F32))

    def write_q(rows, acc):
        q_ref[rows, :] = _head_norm(acc, qg_ref[...], q_scale).astype(q_ref.dtype)

    def write_k(rows, acc):
        k = _head_norm(acc, kg_ref[...], 1.0)
        k32_ref[rows, :] = k
        kb_ref[rows, :] = k.astype(kb_ref.dtype)

    def write_v(rows, acc):
        v32_ref[rows, :] = acc
        vb_ref[rows, :] = acc.astype(vb_ref.dtype)

    pl.when(j < nq)(lambda: by_slab(write_q))
    pl.when(jnp.logical_and(j >= nq, j < 2 * nq))(lambda: by_slab(write_k))
    pl.when(j >= 2 * nq)(lambda: by_slab(write_v))


def _qkv_proj(y, gain, w_qkv, layer, q_gain, k_gain, q_scale):
    m, d = y.shape
    dh = q_gain.shape[0]
    tm, tn = min(1024, m), min(512, d)
    nq = d // tn
    assert m % tm == 0 and d % tn == 0 and tn % dh == 0
    out = lambda kind: pl.BlockSpec((tm, tn), lambda i, j: (i, jnp.clip(j - kind * nq, 0, nq - 1)))
    small = lambda n: pl.BlockSpec((1, n), lambda i, j: (0, 0))
    return pl.pallas_call(
        functools.partial(_qkv_body, nq=nq, q_scale=q_scale), grid=(m // tm, 3 * nq),
        in_specs=[pl.BlockSpec((tm, d), lambda i, j: (i, 0)), small(d), small(dh), small(dh),
                  pl.BlockSpec((None, d, tn), lambda i, j: (layer, 0, j))],
        out_specs=[out(0), out(1), out(1), out(2), out(2)],
        out_shape=[jax.ShapeDtypeStruct((m, d), dt) for dt in (BF16, F32, BF16, F32, BF16)],
        scratch_shapes=[pltpu.VMEM((tm, d), BF16)],
        compiler_params=_cparams(2), name="attn_qkv")(
            y, gain.reshape(1, d), q_gain.reshape(1, dh), k_gain.reshape(1, dh), w_qkv)


def _attn_layer(y, gain, n_seq, t_len, n_heads, w_qkv, q_gain, k_gain, sb_bias, w_o, layer,
                cache=None):
    m, d = y.shape
    dh = d // n_heads
    sb_bias = sb_bias * LOG2_E
    q, k32, kb, v32, vb = _qkv_proj(y, gain, w_qkv, layer, q_gain, k_gain, dh ** -0.5 * LOG2_E)
    if cache is None:
        o = _attn_prompt(q, kb, vb, sb_bias, n_seq, t_len, n_heads)
    else:
        cache_k, cache_v, page_table = cache
        o = _attn_decode(q, k32, v32, sb_bias, cache_k, cache_v, layer, page_table, n_seq, t_len,
                         n_heads)
    (out,) = _fused_matmul("attn_out", m, d, d, [(o, "rows")], [], [(w_o, layer, 0)], [y], [F32],
                           _ident_prologue, _resid_epilogue)
    shape = (n_seq, t_len, n_heads, dh)
    return out, k32.reshape(shape), v32.reshape(shape)


def kernel(x_prompt, x_sample, state_ssm_re, state_ssm_im, state_conv, cache_k, cache_v, page_table, norm_mix, norm_mlp, ssm_lambda_re, ssm_lambda_im, ssm_log_dt, ssm_b_re, ssm_b_im, ssm_c_re, ssm_c_im, ssm_d, ssm_w_glu, conv_w_in, conv_w_dw, conv_w_out, attn_w_qkv, attn_q_norm, attn_k_norm, attn_sb_bias, attn_w_o, mlp_w_up, mlp_w_down):
    n_p, t_p, d = x_prompt.shape
    n_s, t_s, _ = x_sample.shape
    depth = norm_mix.shape[0]
    n_heads = attn_sb_bias.shape[1]
    assert conv_w_dw.shape[1] == 3 and d // n_heads == LANES
    chunk_p = min(S5_CHUNK, t_p)
    assert t_p % chunk_p == 0 and (t_p // chunk_p) & (t_p // chunk_p - 1) == 0

    w_glu, w_in, w_out = (w.astype(BF16) for w in (ssm_w_glu, conv_w_in, conv_w_out))
    w_qkv, w_o = attn_w_qkv.astype(BF16), attn_w_o.astype(BF16)
    w_up, w_down = mlp_w_up.astype(BF16), mlp_w_down.astype(BF16)

    y_p = x_prompt.reshape(n_p * t_p, d)
    y_s = x_sample.reshape(n_s * t_s, d)
    outs = {k: [] for k in ("re_p", "im_p", "re_s", "im_s", "cv_p", "cv_s", "k_p", "v_p", "k_s", "v_s")}
    for i in range(depth):
        kind, j = i % 3, i // 3
        if kind == 0:
            ssm = (ssm_lambda_re[j], ssm_lambda_im[j], ssm_log_dt[j], ssm_b_re[j], ssm_b_im[j],
                   ssm_c_re[j], ssm_c_im[j], ssm_d[j])
            tabs_p = _s5_tables(*ssm, chunk_p, t_p // chunk_p, True)
            tabs_s = _s5_tables(*ssm, t_s, 1, False)
            h0 = jnp.concatenate([state_ssm_re[j], state_ssm_im[j]], axis=-1).transpose(1, 0, 2)
            y_p, re_p, im_p = _s5_layer(y_p, norm_mix[i], n_p, t_p, tabs_p, None, w_glu, j, chunk_p)
            y_s, re_s, im_s = _s5_layer(y_s, norm_mix[i], n_s, t_s, tabs_s, h0, w_glu, j, t_s)
            outs["re_p"].append(re_p); outs["im_p"].append(im_p)
            outs["re_s"].append(re_s); outs["im_s"].append(im_s)
        elif kind == 1:
            y_p, cv_p = _conv_layer(y_p, norm_mix[i], n_p, t_p, None, w_in, conv_w_dw, w_out, j)
            y_s, cv_s = _conv_layer(y_s, norm_mix[i], n_s, t_s, state_conv[j], w_in, conv_w_dw,
                                    w_out, j)
            outs["cv_p"].append(cv_p); outs["cv_s"].append(cv_s)
        else:
            attn = (w_qkv, attn_q_norm[j], attn_k_norm[j], attn_sb_bias[j], w_o, j)
            y_p, k_p, v_p = _attn_layer(y_p, norm_mix[i], n_p, t_p, n_heads, *attn)
            y_s, k_s, v_s = _attn_layer(y_s, norm_mix[i], n_s, t_s, n_heads, *attn,
                                        cache=(cache_k, cache_v, page_table))
            outs["k_p"].append(k_p); outs["v_p"].append(v_p)
            outs["k_s"].append(k_s); outs["v_s"].append(v_s)
        y_p = _mlp(y_p, norm_mlp[i], w_up, w_down, i)
        y_s = _mlp(y_s, norm_mlp[i], w_up, w_down, i)
    st = lambda key: jnp.stack(outs[key])
    return (y_p.reshape(n_p, t_p, d), y_s.reshape(n_s, t_s, d),
            st("re_p"), st("im_p"), st("re_s"), st("im_s"), st("cv_p"), st("cv_s"),
            st("k_p"), st("v_p"), st("k_s"), st("v_s"))
```

```python
import functools
import math

import jax
import jax.numpy as jnp
from jax import lax
from jax.experimental import pallas as pl
from jax.experimental.pallas import tpu as pltpu

F32 = jnp.float32
BF16 = jnp.bfloat16
EPS = 1e-6
LANES = 128
SUBLANES = 8
VMEM_LIMIT_BYTES = 56 * 1024 * 1024
VMEM_BUDGET_BYTES = 44 * 1024 * 1024
S5_CHUNK = 16
HIGHEST = lax.Precision.HIGHEST


def _cparams(n_axes):
    return pltpu.CompilerParams(dimension_semantics=("arbitrary",) * n_axes,
                                vmem_limit_bytes=VMEM_LIMIT_BYTES)


def _rms(x, g):
    ms = jnp.mean(x * x, axis=-1, keepdims=True)
    return x * lax.rsqrt(ms + EPS) * g


def _iota(shape, dim):
    return lax.broadcasted_iota(jnp.int32, shape, dim)


def _mod(x, n):
    return x & (n - 1) if n & (n - 1) == 0 else lax.rem(x, n)


def _div(x, n):
    return x >> (n.bit_length() - 1) if n & (n - 1) == 0 else lax.div(x, n)


def _plan_tiles(m, kdim, n_out, row_bytes_per_row, n_w, tile_itemsizes):
    for tm, tn, n_buf in ((1024, 1024, 2), (1024, 512, 2), (1024, 1024, 1), (1024, 512, 1),
                          (512, 1024, 2), (512, 512, 2), (512, 512, 1)):
        tm, tn = min(tm, m), min(tn, n_out)
        stream = n_w * kdim * tn * 2 + tm * tn * sum(tile_itemsizes)
        fixed = tm * kdim * 2 + 2 * n_w * tm * tn * 4
        if n_buf * tm * row_bytes_per_row + 2 * stream + fixed <= VMEM_BUDGET_BYTES:
            break
    return tm, tn, n_buf


def _fused_matmul(name, m, kdim, n_out, row_in, const_in, weights, tile_in, out_dtypes,
                  prologue, epilogue):
    n_row, n_const, n_w, n_tile, n_o = (len(row_in), len(const_in), len(weights),
                                        len(tile_in), len(out_dtypes))
    size = lambda dt: jnp.dtype(dt).itemsize
    tm, tn, n_buf = _plan_tiles(
        m, kdim, n_out, sum(a.shape[1] * size(a.dtype) for a, kind in row_in if kind == "rows"),
        n_w, [size(a.dtype) for a in tile_in] + [size(dt) for dt in out_dtypes])
    assert m % tm == 0 and n_out % tn == 0
    row_mode = pl.Buffered(1) if n_buf == 1 else None
    per = tm // SUBLANES
    in_specs, args = [], []
    for arr, kind in row_in:
        if kind == "rows":
            spec = pl.BlockSpec((tm, arr.shape[1]), lambda i, j: (i, 0), pipeline_mode=row_mode)
        else:
            spec = pl.BlockSpec((SUBLANES, arr.shape[1]),
                                lambda i, j: (jnp.maximum(i * per - 1, 0), 0))
        in_specs.append(spec)
        args.append(arr)
    for arr in const_in:
        in_specs.append(pl.BlockSpec(arr.shape, lambda i, j, nd=arr.ndim: (0,) * nd))
        args.append(arr)
    for arr, layer, col in weights:
        in_specs.append(pl.BlockSpec((None, kdim, tn),
                                     lambda i, j, layer=layer, off=col // tn: (layer, 0, j + off)))
        args.append(arr)
    for arr in tile_in:
        in_specs.append(pl.BlockSpec((tm, tn), lambda i, j: (i, j)))
        args.append(arr)
    out_shape = [jax.ShapeDtypeStruct((m, n_out), dt) for dt in out_dtypes]
    out_specs = [pl.BlockSpec((tm, tn), lambda i, j: (i, j)) for _ in out_dtypes]

    def body(*refs):
        p = 0
        row_refs = refs[p:p + n_row]; p += n_row
        const_refs = refs[p:p + n_const]; p += n_const
        w_refs = refs[p:p + n_w]; p += n_w
        tile_refs = refs[p:p + n_tile]; p += n_tile
        out_refs = refs[p:p + n_o]; p += n_o
        lhs_ref = refs[p]

        @pl.when(pl.program_id(1) == 0)
        def _():
            lhs_ref[...] = prologue(row_refs, const_refs, tm).astype(BF16)

        slab = min(256, tm)
        for r0 in range(0, tm, slab):
            rows = slice(r0, r0 + slab)
            lhs = lhs_ref[rows, :]
            accs = [jnp.dot(lhs, w[...], preferred_element_type=F32) for w in w_refs]
            tiles = [t.at[rows, :] for t in tile_refs]
            for o_ref, val in zip(out_refs, epilogue(accs, tiles, const_refs)):
                o_ref[rows, :] = val.astype(o_ref.dtype)

    return pl.pallas_call(
        body, grid=(m // tm, n_out // tn), in_specs=in_specs, out_specs=out_specs,
        out_shape=out_shape, scratch_shapes=[pltpu.VMEM((tm, kdim), BF16)],
        compiler_params=_cparams(2), name=name)(*args)


def _ident_prologue(row_refs, const_refs, tm):
    return row_refs[0][...]


def _head_norm(acc, gain, scale):
    segs = []
    for h in range(acc.shape[1] // LANES):
        seg = acc[:, h * LANES:(h + 1) * LANES]
        ms = jnp.mean(seg * seg, axis=-1, keepdims=True)
        y = seg * lax.rsqrt(ms + EPS) * gain
        segs.append(y * scale if scale != 1.0 else y)
    return segs[0] if len(segs) == 1 else jnp.concatenate(segs, axis=-1)


def _mlp_body(x_ref, g_ref, wu_ref, wd_ref, o_ref, xn_ref, acc_ref):
    f = pl.program_id(1)

    @pl.when(f == 0)
    def _():
        xn_ref[...] = _rms(x_ref[...], g_ref[...]).astype(BF16)
        acc_ref[...] = jnp.zeros_like(acc_ref)

    h = jnp.maximum(jnp.dot(xn_ref[...], wu_ref[...], preferred_element_type=F32), 0.0)
    acc_ref[...] += jnp.dot((h * h).astype(BF16), wd_ref[...], preferred_element_type=F32)

    @pl.when(f == pl.num_programs(1) - 1)
    def _():
        o_ref[...] = x_ref[...] + acc_ref[...]


def _mlp(x, gain, w_up, w_down, layer):
    m, d = x.shape
    ff = w_up.shape[2]
    tm = min(512, m)
    tf = min(512 if m > 512 else 2048, ff)
    assert m % tm == 0 and ff % tf == 0
    return pl.pallas_call(
        _mlp_body, grid=(m // tm, ff // tf),
        in_specs=[pl.BlockSpec((tm, d), lambda i, f: (i, 0)),
                  pl.BlockSpec((1, d), lambda i, f: (0, 0)),
                  pl.BlockSpec((None, d, tf), lambda i, f: (layer, 0, f)),
                  pl.BlockSpec((None, tf, d), lambda i, f: (layer, f, 0))],
        out_specs=pl.BlockSpec((tm, d), lambda i, f: (i, 0)),
        out_shape=jax.ShapeDtypeStruct((m, d), F32),
        scratch_shapes=[pltpu.VMEM((tm, d), BF16), pltpu.VMEM((tm, d), F32)],
        compiler_params=_cparams(2), name="mlp")(x, gain.reshape(1, d), w_up, w_down)


def _norm_body(x_ref, g_ref, o_ref):
    o_ref[...] = _rms(x_ref[...], g_ref[...])


def _norm(x, gain):
    m, d = x.shape
    tm = min(512, m)
    assert m % tm == 0
    return pl.pallas_call(
        _norm_body, grid=(m // tm,),
        in_specs=[pl.BlockSpec((tm, d), lambda i: (i, 0)), pl.BlockSpec((1, d), lambda i: (0, 0))],
        out_specs=pl.BlockSpec((tm, d), lambda i: (i, 0)),
        out_shape=jax.ShapeDtypeStruct((m, d), F32),
        compiler_params=_cparams(1), name="rmsnorm")(x, gain.reshape(1, d))


def _complex_scale(acos, asin, h):
    return acos * h + asin * pltpu.roll(h, h.shape[1] // 2, axis=1)


def _gelu(y):
    return y * (0.5 * (1.0 + jnp.tanh(math.sqrt(2.0 / math.pi) * (y + 0.044715 * (y * y * y)))))


def _s5_step_body(u_ref, m_ref, bc_ref, cp_ref, d_ref, acos_ref, asin_ref, h0_ref, z_ref, hl_ref,
                  *, gt):
    def one_group(g, carry):
        u = u_ref[g]
        ub = u.astype(BF16)
        h_prev = h0_ref[g]
        hl_ref[g] = (jnp.dot(ub, bc_ref[g], preferred_element_type=F32)
                     + _complex_scale(acos_ref[g][0:1], asin_ref[g][0:1], h_prev))
        y = (jnp.dot(ub, m_ref[g], preferred_element_type=F32)
             + jnp.dot(h_prev.astype(BF16), cp_ref[g], preferred_element_type=F32)
             + u * d_ref[g])
        z_ref[g] = _gelu(y).astype(z_ref.dtype)
        return carry

    lax.fori_loop(0, gt, one_group, 0)


def _s5_seq_body(x_ref, bd_ref, bct_ref, cpt_ref, acos_ref, asin_ref, z_ref, hl_ref,
                 m8_ref, bc8_ref, cp8_ref, h_ref, *, chunk, n_chunk, n_seq, c):
    gl = LANES // c
    rows = n_seq * n_chunk
    blk = lambda i: slice(i * LANES, (i + 1) * LANES)
    grp_of_lane = _div(_iota((c, LANES), 1), c)
    for tau in range(chunk):
        kt = bd_ref[tau]
        m8_ref[blk(0), blk(tau)] = jnp.concatenate(
            [jnp.where(grp_of_lane == g, kt, 0.0) for g in range(gl)], axis=0).astype(BF16)
    for s in range(1, chunk):
        for t in range(s, chunk):
            m8_ref[blk(s), blk(t)] = m8_ref[blk(0), blk(t - s)]
        if s % 2:
            m8_ref[blk(s), blk(s - 1)] = jnp.zeros((LANES, LANES), BF16)
    row_grp = _div(_iota((LANES, LANES), 0), c)
    lane_grp = _div(_iota((LANES, LANES), 1), c)
    for s in range(chunk):
        bct_s, cpt_s = bct_ref[s].astype(F32), cpt_ref[s].astype(F32)
        for g in range(gl):
            bc8_ref[blk(s), blk(g)] = jnp.where(row_grp == g, bct_s, 0.0).astype(BF16)
            cp8_ref[blk(g), blk(s)] = jnp.where(lane_grp == g, cpt_s, 0.0).astype(BF16)

    u2 = jnp.concatenate([x_ref[pl.ds(s, rows, stride=chunk), :].astype(BF16)
                          for s in range(chunk)], axis=1)
    h = jnp.dot(u2, bc8_ref[...], preferred_element_type=F32)
    kidx = _iota((rows, 1), 0) & (n_chunk - 1)
    swap = lambda v: jnp.concatenate(
        [pltpu.roll(v[:, blk(g)], LANES // 2, axis=1) for g in range(gl)], axis=1)
    shift, si = 1, 0
    while shift < n_chunk:
        sh = jnp.where(kidx >= shift, pltpu.roll(h, shift, axis=0), 0.0)
        h = h + acos_ref[si:si + 1, :] * sh + asin_ref[si:si + 1, :] * swap(sh)
        shift, si = shift * 2, si + 1
    h_ref[...] = h
    h_prev = jnp.where(kidx >= 1, pltpu.roll(h, 1, axis=0), 0.0).astype(BF16)
    for t in range(0, chunk, 2):
        cols = slice(t * LANES, (t + 2) * LANES)
        y = (jnp.dot(u2[:, :(t + 2) * LANES], m8_ref[:(t + 2) * LANES, cols],
                     preferred_element_type=F32)
             + jnp.dot(h_prev, cp8_ref[:, cols], preferred_element_type=F32))
        z = _gelu(y)
        z_ref[pl.ds(t, rows, stride=chunk), :] = z[:, :LANES]
        z_ref[pl.ds(t + 1, rows, stride=chunk), :] = z[:, LANES:]
    for n in range(n_seq):
        hl_ref[pl.ds(n, 1), :] = h_ref[pl.ds((n + 1) * n_chunk - 1, 1), :]


def _s5_tables(lam_re, lam_im, log_dt, b_re, b_im, c_re, c_im, d, chunk, n_chunk, per_tile):
    g, p = lam_re.shape
    c = b_re.shape[2]
    dt = jnp.exp(log_dt)[:, None]
    xr, xi = lam_re * dt, lam_im * dt

    def a_pow(ks):
        kk = jnp.asarray(ks, F32)[None, :, None]
        mag = jnp.exp(xr[:, None, :] * kk)
        return mag * jnp.cos(xi[:, None, :] * kk), mag * jnp.sin(xi[:, None, :] * kk)

    pr, pi = a_pow(jnp.arange(chunk + 1))
    ar, ai = pr[:, 1], pi[:, 1]
    den = lam_re * lam_re + lam_im * lam_im
    qr = ((ar - 1.0) * lam_re + ai * lam_im) / den
    qi = (ai * lam_re - (ar - 1.0) * lam_im) / den
    bb_re = qr[..., None] * b_re - qi[..., None] * b_im
    bb_im = qr[..., None] * b_im + qi[..., None] * b_re
    w_re = c_re[:, None] * pr[:, :chunk, None] - c_im[:, None] * pi[:, :chunk, None]
    w_im = c_re[:, None] * pi[:, :chunk, None] + c_im[:, None] * pr[:, :chunk, None]
    kt = jnp.einsum("gtoq,gqi->gtoi", jnp.concatenate([w_re, -w_im], axis=-1),
                    jnp.concatenate([bb_re, bb_im], axis=1), precision=HIGHEST)
    pair = lambda x, y: jnp.concatenate([x, y], axis=-1)
    rr, ri = pr[:, :chunk][:, ::-1], pi[:, :chunk][:, ::-1]
    bt_re, bt_im = bb_re.transpose(0, 2, 1), bb_im.transpose(0, 2, 1)
    bc = (pair(rr, rr)[:, :, None] * pair(bt_re, bt_im)[:, None]
          + pair(-ri, ri)[:, :, None] * pair(bt_im, bt_re)[:, None])
    shifts = [1]
    while shifts[-1] * 2 < n_chunk:
        shifts.append(shifts[-1] * 2)
    while len(shifts) < SUBLANES:
        shifts.append(shifts[-1])
    sr, si = a_pow(chunk * jnp.asarray(shifts, F32))
    acos, asin = pair(sr, sr), pair(-si, si)
    if not per_tile:
        kpad = jnp.concatenate([kt, jnp.zeros((g, 1, c, c), F32)], axis=1)
        s_idx = jnp.arange(chunk)[:, None]
        t_idx = jnp.arange(chunk)[None, :]
        tau = jnp.where(t_idx >= s_idx, t_idx - s_idx, chunk)
        mmat = kpad[:, tau].transpose(0, 1, 4, 2, 3).reshape(g, chunk * c, chunk * c)
        cr, ci = c_re.transpose(0, 2, 1)[:, :, None], c_im.transpose(0, 2, 1)[:, :, None]
        nr, ni = pr[:, 1:].transpose(0, 2, 1)[..., None], pi[:, 1:].transpose(0, 2, 1)[..., None]
        cp = jnp.concatenate([cr * nr - ci * ni, -(cr * ni + ci * nr)], axis=1)
        dtile = jnp.tile(d.reshape(g, 1, c), (1, chunk, 1)).reshape(g, 1, chunk * c)
        return (mmat.astype(BF16), bc.reshape(g, chunk * c, 2 * p).astype(BF16),
                cp.reshape(g, 2 * p, chunk * c).astype(BF16), dtile, acos, asin)
    gl = LANES // c
    nt = g // gl
    p2 = 2 * p
    kd = kt.at[:, 0].add(d.reshape(g, c)[:, :, None] * jnp.eye(c, dtype=F32))
    bd = kd.reshape(nt, gl, chunk, c, c).transpose(0, 2, 4, 1, 3).reshape(nt, chunk, c, LANES)
    bct = bc.reshape(nt, gl, chunk, c, p2).transpose(0, 2, 1, 3, 4).reshape(nt, chunk, LANES, p2)
    on_lanes = lambda x: x.reshape(nt, gl, c, p).transpose(0, 3, 1, 2).reshape(nt, 1, p, LANES)
    spread = lambda x: jnp.repeat(
        x[:, 1:].reshape(nt, gl, chunk, p).transpose(0, 2, 3, 1), c, axis=-1)
    cr, ci, nr, ni = on_lanes(c_re), on_lanes(c_im), spread(pr), spread(pi)
    cpt = jnp.concatenate([cr * nr - ci * ni, -(cr * ni + ci * nr)], axis=2)
    tile = lambda a: a.reshape(nt, gl, SUBLANES, p2).transpose(0, 2, 1, 3).reshape(
        nt, SUBLANES, gl * p2)
    return bd, bct.astype(BF16), cpt.astype(BF16), tile(acos), tile(asin)


def _s5_step(hn, n_seq, t_len, tabs, h0):
    mmat, bc, cp, dt, acos, asin = tabs
    g, lc = mmat.shape[:2]
    p2 = bc.shape[2]
    c = lc // t_len
    u_r = hn.reshape(n_seq, t_len, g, c).transpose(2, 0, 1, 3).reshape(g, n_seq, lc)
    gt = min(8, g)
    grp = lambda shape: pl.BlockSpec((gt,) + shape, lambda i: (i, 0, 0))
    z_r, h_last = pl.pallas_call(
        functools.partial(_s5_step_body, gt=gt), grid=(g // gt,),
        in_specs=[grp((n_seq, lc)), grp((lc, lc)), grp((lc, p2)), grp((p2, lc)), grp((1, lc)),
                  grp((SUBLANES, p2)), grp((SUBLANES, p2)), grp((n_seq, p2))],
        out_specs=[grp((n_seq, lc)), grp((n_seq, p2))],
        out_shape=[jax.ShapeDtypeStruct((g, n_seq, lc), BF16),
                   jax.ShapeDtypeStruct((g, n_seq, p2), F32)],
        compiler_params=_cparams(1), name="s5_step")(u_r, mmat, bc, cp, dt, acos, asin, h0)
    z = z_r.reshape(g, n_seq, t_len, c).transpose(1, 2, 0, 3).reshape(n_seq * t_len, g * c)
    return z, h_last.transpose(1, 0, 2)


def _s5_seq(hn, n_seq, t_len, tabs, chunk):
    bd, bct, cpt, acos, asin = tabs
    m, d = hn.shape
    nt = bd.shape[0]
    p2 = bct.shape[3]
    gl = acos.shape[2] // p2
    c = LANES // gl
    n_chunk = t_len // chunk
    assert p2 == LANES and d == nt * LANES
    assert chunk % 2 == 0
    tab = lambda a: pl.BlockSpec((None,) + a.shape[1:], lambda j: (j,) + (0,) * (a.ndim - 1))
    z, h_last = pl.pallas_call(
        functools.partial(_s5_seq_body, chunk=chunk, n_chunk=n_chunk, n_seq=n_seq, c=c),
        grid=(nt,),
        in_specs=[pl.BlockSpec((m, LANES), lambda j: (0, j)),
                  tab(bd), tab(bct), tab(cpt), tab(acos), tab(asin)],
        out_specs=[pl.BlockSpec((m, LANES), lambda j: (0, j)),
                   pl.BlockSpec((None, n_seq, gl * p2), lambda j: (j, 0, 0))],
        out_shape=[jax.ShapeDtypeStruct((m, d), F32),
                   jax.ShapeDtypeStruct((nt, n_seq, gl * p2), F32)],
        scratch_shapes=[pltpu.VMEM((chunk * LANES, chunk * LANES), BF16),
                        pltpu.VMEM((chunk * LANES, gl * p2), BF16),
                        pltpu.VMEM((gl * p2, chunk * LANES), BF16),
                        pltpu.VMEM((n_seq * n_chunk, gl * p2), F32)],
        compiler_params=_cparams(1), name="s5_seq")(hn, bd, bct, cpt, acos, asin)
    h_last = h_last.reshape(nt, n_seq, gl, p2).transpose(1, 0, 2, 3).reshape(n_seq, nt * gl, p2)
    return z, h_last


def _glu_epilogue(accs, tile_refs, const_refs):
    a, gate = accs
    return [tile_refs[0][...] + a * jax.nn.sigmoid(gate)]


def _s5_layer(y, gain, n_seq, t_len, tabs, h0, w_glu, layer, chunk):
    m, d = y.shape
    if h0 is None:
        z, h_last = _s5_seq(_norm(y, gain), n_seq, t_len, tabs, chunk)
    else:
        z, h_last = _s5_step(_norm(y, gain), n_seq, t_len, tabs, h0)
    (out,) = _fused_matmul("s5_glu", m, d, d, [(z, "rows")], [],
                           [(w_glu, layer, 0), (w_glu, layer, d)], [y], [F32],
                           _ident_prologue, _glu_epilogue)
    p = h_last.shape[2] // 2
    return out, h_last[..., :p], h_last[..., p:]


def _conv_in_body(*refs, tm, t_len, has_buf):
    if has_buf:
        x_ref, g_ref, wb_ref, wc_ref, wv_ref, dw_ref, b0_ref, b1_ref, o_ref, cv_ref, lhs_ref = refs
    else:
        x_ref, g_ref, wb_ref, wc_ref, wv_ref, dw_ref, o_ref, tail_ref, lhs_ref, carry_ref = refs
    i, j = pl.program_id(0), pl.program_id(1)

    @pl.when(j == 0)
    def _():
        lhs_ref[...] = _rms(x_ref[...], g_ref[...]).astype(BF16)

    lhs = lhs_ref[...]
    gate_b, gate_c, v = (jnp.dot(lhs, w[...], preferred_element_type=F32)
                         for w in (wb_ref, wc_ref, wv_ref))
    cv = gate_c * v
    w = dw_ref[...]
    t = _mod(i * tm + _iota((tm, 1), 0), t_len)
    if has_buf:
        b0, b1 = b0_ref[...], b1_ref[...]
        r1 = jnp.where(t >= 1, pltpu.roll(cv, 1, axis=0), b1)
        r2 = jnp.where(t >= 2, pltpu.roll(cv, 2, axis=0), jnp.where(t == 1, b1, b0))
        cv_ref[...] = cv
    else:
        @pl.when(i == 0)
        def _():
            carry_ref[j] = jnp.zeros(carry_ref.shape[1:], F32)

        full = jnp.concatenate([carry_ref[j], cv], axis=0)
        r1 = jnp.where(t >= 1, pltpu.roll(full, 1, axis=0)[SUBLANES:], 0.0)
        r2 = jnp.where(t >= 2, pltpu.roll(full, 2, axis=0)[SUBLANES:], 0.0)
        carry_ref[j] = cv[tm - SUBLANES:]
        tail_ref[...] = cv[tm - SUBLANES:]
    o_ref[...] = (gate_b * (w[2:3] * cv + w[1:2] * r1 + w[0:1] * r2)).astype(o_ref.dtype)


def _conv_in(y, gain, n_seq, t_len, buf, w_in, w_dw, layer):
    m, d = y.shape
    has_buf = buf is not None
    tm, tn = (m if has_buf else min(1024, t_len)), min(512, d)
    assert t_len >= 2 and m <= 1024 if has_buf else t_len % tm == 0 and tm >= SUBLANES
    nj = d // tn
    col = lambda k: pl.BlockSpec((None, d, tn), lambda i, j, k=k: (layer, 0, j + k * nj))
    tile = pl.BlockSpec((tm, tn), lambda i, j: (i, j))
    in_specs = [pl.BlockSpec((tm, d), lambda i, j: (i, 0)), pl.BlockSpec((1, d), lambda i, j: (0, 0)),
                col(0), col(1), col(2), pl.BlockSpec((None, 3, tn), lambda i, j: (layer, 0, j))]
    args = [y, gain.reshape(1, d), w_in, w_in, w_in, w_dw]
    scratch = [pltpu.VMEM((tm, d), BF16)]
    if has_buf:
        in_specs += [tile, tile]
        args += [jnp.repeat(buf[:, k], t_len, axis=0) for k in range(2)]
        out_specs = [tile, tile]
        out_shape = [jax.ShapeDtypeStruct((m, d), BF16), jax.ShapeDtypeStruct((m, d), F32)]
    else:
        out_specs = [tile, pl.BlockSpec((None, SUBLANES, tn), lambda i, j: (i, 0, j))]
        out_shape = [jax.ShapeDtypeStruct((m, d), BF16),
                     jax.ShapeDtypeStruct((m // tm, SUBLANES, d), F32)]
        scratch.append(pltpu.VMEM((nj, SUBLANES, tn), F32))
    g, extra = pl.pallas_call(
        functools.partial(_conv_in_body, tm=tm, t_len=t_len, has_buf=has_buf),
        grid=(m // tm, nj), in_specs=in_specs, out_specs=out_specs, out_shape=out_shape,
        scratch_shapes=scratch, compiler_params=_cparams(2), name="conv_in")(*args)
    if has_buf:
        return g, extra.reshape(n_seq, t_len, d)[:, t_len - 2:]
    per_seq = t_len // tm
    return g, extra.reshape(n_seq, per_seq, SUBLANES, d)[:, per_seq - 1, SUBLANES - 2:]


def _resid_epilogue(accs, tile_refs, const_refs):
    return [tile_refs[0][...] + accs[0]]


def _conv_layer(y, gain, n_seq, t_len, buf, w_in, w_dw, w_out, layer):
    m, d = y.shape
    g, state = _conv_in(y, gain, n_seq, t_len, buf, w_in, w_dw, layer)
    (out,) = _fused_matmul("conv_out", m, d, d, [(g, "rows")], [], [(w_out, layer, 0)], [y],
                           [F32], _ident_prologue, _resid_epilogue)
    return out, state


LOG2_E = math.log2(math.e)


def _sb_logs(z, mask):
    sp = jnp.log2(1.0 + jnp.exp2(-jnp.abs(z)))
    log_b = jnp.minimum(z, 0.0) - sp
    log_1m = log_b - z
    if mask is not None:
        log_1m = jnp.where(mask, log_1m, 0.0)
    return log_b, log_1m


def _sb_suffix(log_1m, upper):
    hi = log_1m.astype(BF16)
    lo = (log_1m - hi.astype(F32)).astype(BF16)
    return (jnp.dot(hi, upper, preferred_element_type=F32)
            + jnp.dot(lo, upper, preferred_element_type=F32))


def _sb_finish(log_b, log_1m, suffix, mask, run):
    a = jnp.exp2(log_b + suffix + run)
    if mask is not None:
        a = jnp.where(mask, a, 0.0)
    return a, run + jnp.sum(log_1m, axis=-1, keepdims=True)


def _sb_weights(z, mask, upper, run):
    log_b, log_1m = _sb_logs(z, mask)
    return _sb_finish(log_b, log_1m, _sb_suffix(log_1m, upper), mask, run)


def _later_key_matrix(tk):
    return (_iota((tk, tk), 0) > _iota((tk, tk), 1)).astype(BF16)


def _qk(q, k_blk):
    return lax.dot_general(q, k_blk, (((1,), (1,)), ((), ())), preferred_element_type=F32)


def _attn_body(bias_ref, q_ref, k_ref, v_ref, o_ref, z_scr, acc_scr, run_scr, *, tq, tr, dh, hps):
    hg = pl.program_id(1)
    qi = pl.program_id(2)
    upper = _later_key_matrix(tq)
    units = [(hh, r0) for hh in range(hps) for r0 in range(0, tq, tr)]
    lanes = lambda hh: slice(hh * dh, (hh + 1) * dh)

    def logits_into(slot, kb):
        start = pl.multiple_of(kb * tq, tq)
        for u, (hh, r0) in enumerate(units):
            z_scr[slot, u] = (_qk(q_ref[r0:r0 + tr, lanes(hh)], k_ref[pl.ds(start, tq), lanes(hh)])
                              + bias_ref[hg * hps + hh])

    def tile(slot, kb, masked, next_kb):
        start = pl.multiple_of(kb * tq, tq)
        masks = [(_iota((tr, tq), 1) < _iota((tr, tq), 0) + r0) if masked else None
                 for _, r0 in units]
        zs = [z_scr[slot, u] for u in range(len(units))]
        if next_kb is not None:
            logits_into(1 - slot, next_kb)
        logs = [_sb_logs(z, m) for z, m in zip(zs, masks)]
        sufs = [_sb_suffix(l1m, upper) for _, l1m in logs]
        fins = [_sb_finish(lb, l1m, suf, m, run_scr[u][:, 0:1])
                for u, ((lb, l1m), suf, m) in enumerate(zip(logs, sufs, masks))]
        for u, ((a, run), (hh, _)) in enumerate(zip(fins, units)):
            acc_scr[u] += jnp.dot(a.astype(BF16), v_ref[pl.ds(start, tq), lanes(hh)],
                                  preferred_element_type=F32)
            run_scr[u] = jnp.broadcast_to(run, run_scr.shape[1:])

    acc_scr[...] = jnp.zeros(acc_scr.shape, F32)
    run_scr[...] = jnp.zeros(run_scr.shape, F32)
    logits_into(0, qi)
    tile(0, qi, True, jnp.maximum(qi - 1, 0))

    def pair(p, carry):
        kb = qi - 1 - 2 * p
        tile(1, kb, False, jnp.maximum(kb - 1, 0))
        tile(0, kb - 1, False, jnp.maximum(kb - 2, 0))
        return carry

    lax.fori_loop(0, qi // 2, pair, 0)

    @pl.when(qi % 2 == 1)
    def _():
        tile(1, 0, False, None)

    for u, (hh, r0) in enumerate(units):
        o_ref[r0:r0 + tr, lanes(hh)] = acc_scr[u].astype(o_ref.dtype)


def _attn_prompt(q, k, v, bias, n_seq, t_len, n_heads):
    m, d = q.shape
    dh = d // n_heads
    tq = min(256, t_len)
    hps = next(h for h in (4, 2, 1) if n_heads % h == 0)
    qb = t_len // tq
    tr = min(128, tq)
    n_units = hps * (tq // tr)
    grid_spec = pltpu.PrefetchScalarGridSpec(
        num_scalar_prefetch=1, grid=(n_seq, n_heads // hps, qb),
        in_specs=[pl.BlockSpec((tq, hps * dh), lambda n, h, i, b: (n * qb + i, h)),
                  pl.BlockSpec((t_len, hps * dh), lambda n, h, i, b: (n, h)),
                  pl.BlockSpec((t_len, hps * dh), lambda n, h, i, b: (n, h))],
        out_specs=pl.BlockSpec((tq, hps * dh), lambda n, h, i, b: (n * qb + i, h)),
        scratch_shapes=[pltpu.VMEM((2, n_units, tr, tq), F32), pltpu.VMEM((n_units, tr, dh), F32),
                        pltpu.VMEM((n_units, tr, LANES), F32)])
    return pl.pallas_call(
        functools.partial(_attn_body, tq=tq, tr=tr, dh=dh, hps=hps), grid_spec=grid_spec,
        out_shape=jax.ShapeDtypeStruct((m, d), BF16),
        compiler_params=_cparams(3), name="sb_attn_prompt")(bias, q, k, v)


def _decode_body(pt_ref, q_ref, bias_ref, e_ref, et_ref, hm_ref, hm32_ref, *refs, n_heads, ppt):
    kn_refs, vn_refs = refs[:ppt], refs[ppt:2 * ppt]
    kc_refs, vc_refs = refs[2 * ppt:3 * ppt], refs[3 * ppt:4 * ppt]
    o_ref, run_ref, acc_ref = refs[4 * ppt:]
    step = pl.program_id(1)
    page = vn_refs[0].shape[0] // n_heads
    rows = q_ref.shape[0]
    tk = ppt * page

    def tile(k_refs, v_refs, mask, run, acc):
        parts = []
        for slot in range(ppt):
            z_rows = _qk(q_ref[...], k_refs[slot][...].astype(BF16)) * hm32_ref[...]
            hi = z_rows.astype(BF16)
            parts += [hi, (z_rows - hi.astype(F32)).astype(BF16)]
        zc = jnp.dot(jnp.concatenate(parts, axis=0), et_ref[...], preferred_element_type=F32)
        zs = [zc[2 * s * rows:(2 * s + 1) * rows] + zc[(2 * s + 1) * rows:(2 * s + 2) * rows]
              for s in range(ppt)]
        z = (zs[0] if ppt == 1 else jnp.concatenate(zs, axis=1)) + bias_ref[...]
        a, run = _sb_weights(z, mask, _later_key_matrix(tk), run)
        a = a.astype(BF16)
        a_st = jnp.concatenate([a[:, s * page:(s + 1) * page] for s in range(ppt)], axis=0)
        a_rows = jnp.dot(a_st, e_ref[...], preferred_element_type=F32).astype(BF16)
        for slot in range(ppt):
            acc = acc + jnp.dot(a_rows[slot * rows:(slot + 1) * rows] * hm_ref[...],
                                v_refs[slot][...].astype(BF16), preferred_element_type=F32)
        run_ref[...] = jnp.broadcast_to(run, run_ref.shape)
        acc_ref[...] = acc

    @pl.when(step == 0)
    def _():
        mask = _iota((rows, tk), 1) < _div(_iota((rows, tk), 0), n_heads)
        tile(kn_refs, vn_refs, mask, jnp.zeros((rows, 1), F32), jnp.zeros(acc_ref.shape, F32))

    @pl.when(step > 0)
    def _():
        tile(kc_refs, vc_refs, None, run_ref[:, 0:1], acc_ref[...])

    @pl.when(step == pl.num_programs(1) - 1)
    def _():
        o_ref[...] = acc_ref[...]


def _attn_decode(q, k_new, v_new, bias, cache_k, cache_v, layer, page_table, n_seq, t_len,
                 n_heads):
    d = q.shape[1]
    dh = d // n_heads
    n_layers, n_pool, page = cache_k.shape[:3]
    n_pages = page_table.shape[1]
    ppt = next(p for p in (4, 2, 1) if n_pages % p == 0)
    n_tiles = n_pages // ppt
    rows = t_len * n_heads
    assert n_heads % SUBLANES == 0 and t_len <= page
    q_rows = q.reshape(n_seq, rows, dh)
    bias_rows = jnp.tile(bias, t_len).reshape(rows, 1)
    key_of_row = jnp.arange(page * n_heads) // n_heads
    expand = (key_of_row[None, :] == jnp.arange(page)[:, None]).astype(BF16)
    head_of_row = jnp.arange(page * n_heads) % n_heads
    head_mask = head_of_row[None, :] == (jnp.arange(rows) % n_heads)[:, None]
    pad_t = ((0, 0), (0, page - t_len), (0, 0), (0, 0))
    as_rows = lambda x: jnp.pad(x.reshape(n_seq, t_len, n_heads, dh), pad_t).reshape(
        n_seq, page * n_heads, dh)
    kn, vn = as_rows(k_new), as_rows(v_new)
    kc = cache_k.reshape(n_layers, n_pool, page * n_heads, dh)
    vc = cache_v.reshape(n_layers, n_pool, page * n_heads, dh)

    def phys(n, s, pt, slot):
        return pt[n, (n_tiles - jnp.maximum(s, 1)) * ppt + slot]

    const = lambda shape: pl.BlockSpec(shape, lambda n, s, pt: (0,) * len(shape))
    in_specs = [pl.BlockSpec((None, rows, dh), lambda n, s, pt: (n, 0, 0)),
                const((rows, 1)), const(expand.shape), const(expand.shape[::-1]),
                const(head_mask.shape), const(head_mask.shape)]
    args = [q_rows, bias_rows, expand, expand.T, head_mask.astype(BF16), head_mask.astype(F32)]
    blk = (None, None, page * n_heads, dh)
    for arr in (kn, vn):
        for slot in range(ppt):
            in_specs.append(pl.BlockSpec(blk[1:], lambda n, s, pt: (n, 0, 0)))
            args.append(arr)
    for arr in (kc, vc):
        for slot in range(ppt):
            in_specs.append(pl.BlockSpec(
                blk, lambda n, s, pt, slot=slot: (layer, phys(n, s, pt, slot), 0, 0)))
            args.append(arr)
    grid_spec = pltpu.PrefetchScalarGridSpec(
        num_scalar_prefetch=1, grid=(n_seq, n_tiles + 1), in_specs=in_specs,
        out_specs=pl.BlockSpec((None, rows, dh), lambda n, s, pt: (n, 0, 0)),
        scratch_shapes=[pltpu.VMEM((rows, LANES), F32), pltpu.VMEM((rows, dh), F32)])
    o = pl.pallas_call(
        functools.partial(_decode_body, n_heads=n_heads, ppt=ppt), grid_spec=grid_spec,
        out_shape=jax.ShapeDtypeStruct((n_seq, rows, dh), F32),
        compiler_params=_cparams(2), name="sb_attn_decode")(page_table, *args)
    return o.reshape(n_seq * t_len, d).astype(BF16)


def _qkv_body(x_ref, g_ref, qg_ref, kg_ref, w_ref, q_ref, k32_ref, kb_ref, v32_ref, vb_ref,
              lhs_ref, *, nq, q_scale):
    j = pl.program_id(1)

    @pl.when(j == 0)
    def _():
        lhs_ref[...] = _rms(x_ref[...], g_ref[...]).astype(BF16)

    def by_slab(write):
        tm = lhs_ref.shape[0]
        slab = min(256, tm)
        for r0 in range(0, tm, slab):
            rows = slice(r0, r0 + slab)
            write(rows, jnp.dot(lhs_ref[rows, :], w_ref[...], preferred_element_type=F32))

    def write_q(rows, acc):
        q_ref[rows, :] = _head_norm(acc, qg_ref[...], q_scale).astype(q_ref.dtype)

    def write_k(rows, acc):
        k = _head_norm(acc, kg_ref[...], 1.0)
        k32_ref[rows, :] = k
        kb_ref[rows, :] = k.astype(kb_ref.dtype)

    def write_v(rows, acc):
        v32_ref[rows, :] = acc
        vb_ref[rows, :] = acc.astype(vb_ref.dtype)

    pl.when(j < nq)(lambda: by_slab(write_q))
    pl.when(jnp.logical_and(j >= nq, j < 2 * nq))(lambda: by_slab(write_k))
    pl.when(j >= 2 * nq)(lambda: by_slab(write_v))


def _qkv_proj(y, gain, w_qkv, layer, q_gain, k_gain, q_scale):
    m, d = y.shape
    dh = q_gain.shape[0]
    tm, tn = min(1024, m), min(512, d)
    nq = d // tn
    assert m % tm == 0 and d % tn == 0 and tn % dh == 0
    out = lambda kind: pl.BlockSpec((tm, tn), lambda i, j: (i, jnp.clip(j - kind * nq, 0, nq - 1)))
    small = lambda n: pl.BlockSpec((1, n), lambda i, j: (0, 0))
    return pl.pallas_call(
        functools.partial(_qkv_body, nq=nq, q_scale=q_scale), grid=(m // tm, 3 * nq),
        in_specs=[pl.BlockSpec((tm, d), lambda i, j: (i, 0)), small(d), small(dh), small(dh),
                  pl.BlockSpec((None, d, tn), lambda i, j: (layer, 0, j))],
        out_specs=[out(0), out(1), out(1), out(2), out(2)],
        out_shape=[jax.ShapeDtypeStruct((m, d), dt) for dt in (BF16, F32, BF16, F32, BF16)],
        scratch_shapes=[pltpu.VMEM((tm, d), BF16)],
        compiler_params=_cparams(2), name="attn_qkv")(
            y, gain.reshape(1, d), q_gain.reshape(1, dh), k_gain.reshape(1, dh), w_qkv)


def _attn_layer(y, gain, n_seq, t_len, n_heads, w_qkv, q_gain, k_gain, sb_bias, w_o, layer,
                cache=None):
    m, d = y.shape
    dh = d // n_heads
    sb_bias = sb_bias * LOG2_E
    q, k32, kb, v32, vb = _qkv_proj(y, gain, w_qkv, layer, q_gain, k_gain, dh ** -0.5 * LOG2_E)
    if cache is None:
        o = _attn_prompt(q, kb, vb, sb_bias, n_seq, t_len, n_heads)
    else:
        cache_k, cache_v, page_table = cache
        o = _attn_decode(q, k32, v32, sb_bias, cache_k, cache_v, layer, page_table, n_seq, t_len,
                         n_heads)
    (out,) = _fused_matmul("attn_out", m, d, d, [(o, "rows")], [], [(w_o, layer, 0)], [y], [F32],
                           _ident_prologue, _resid_epilogue)
    shape = (n_seq, t_len, n_heads, dh)
    return out, k32.reshape(shape), v32.reshape(shape)


def kernel(x_prompt, x_sample, state_ssm_re, state_ssm_im, state_conv, cache_k, cache_v, page_table, norm_mix, norm_mlp, ssm_lambda_re, ssm_lambda_im, ssm_log_dt, ssm_b_re, ssm_b_im, ssm_c_re, ssm_c_im, ssm_d, ssm_w_glu, conv_w_in, conv_w_dw, conv_w_out, attn_w_qkv, attn_q_norm, attn_k_norm, attn_sb_bias, attn_w_o, mlp_w_up, mlp_w_down):
    n_p, t_p, d = x_prompt.shape
    n_s, t_s, _ = x_sample.shape
    depth = norm_mix.shape[0]
    n_heads = attn_sb_bias.shape[1]
    assert conv_w_dw.shape[1] == 3 and d // n_heads == LANES
    chunk_p = min(S5_CHUNK, t_p)
    assert t_p % chunk_p == 0 and (t_p // chunk_p) & (t_p // chunk_p - 1) == 0

    w_glu, w_in, w_out = (w.astype(BF16) for w in (ssm_w_glu, conv_w_in, conv_w_out))
    w_qkv, w_o = attn_w_qkv.astype(BF16), attn_w_o.astype(BF16)
    w_up, w_down = mlp_w_up.astype(BF16), mlp_w_down.astype(BF16)

    y_p = x_prompt.reshape(n_p * t_p, d)
    y_s = x_sample.reshape(n_s * t_s, d)
    outs = {k: [] for k in ("re_p", "im_p", "re_s", "im_s", "cv_p", "cv_s", "k_p", "v_p", "k_s", "v_s")}
    for i in range(depth):
        kind, j = i % 3, i // 3
        if kind == 0:
            ssm = (ssm_lambda_re[j], ssm_lambda_im[j], ssm_log_dt[j], ssm_b_re[j], ssm_b_im[j],
                   ssm_c_re[j], ssm_c_im[j], ssm_d[j])
            tabs_p = _s5_tables(*ssm, chunk_p, t_p // chunk_p, True)
            tabs_s = _s5_tables(*ssm, t_s, 1, False)
            h0 = jnp.concatenate([state_ssm_re[j], state_ssm_im[j]], axis=-1).transpose(1, 0, 2)
            y_p, re_p, im_p = _s5_layer(y_p, norm_mix[i], n_p, t_p, tabs_p, None, w_glu, j, chunk_p)
            y_s, re_s, im_s = _s5_layer(y_s, norm_mix[i], n_s, t_s, tabs_s, h0, w_glu, j, t_s)
            outs["re_p"].append(re_p); outs["im_p"].append(im_p)
            outs["re_s"].append(re_s); outs["im_s"].append(im_s)
        elif kind == 1:
            y_p, cv_p = _conv_layer(y_p, norm_mix[i], n_p, t_p, None, w_in, conv_w_dw, w_out, j)
            y_s, cv_s = _conv_layer(y_s, norm_mix[i], n_s, t_s, state_conv[j], w_in, conv_w_dw,
                                    w_out, j)
            outs["cv_p"].append(cv_p); outs["cv_s"].append(cv_s)
        else:
            attn = (w_qkv, attn_q_norm[j], attn_k_norm[j], attn_sb_bias[j], w_o, j)
            y_p, k_p, v_p = _attn_layer(y_p, norm_mix[i], n_p, t_p, n_heads, *attn)
            y_s, k_s, v_s = _attn_layer(y_s, norm_mix[i], n_s, t_s, n_heads, *attn,
                                        cache=(cache_k, cache_v, page_table))
            outs["k_p"].append(k_p); outs["v_p"].append(v_p)
            outs["k_s"].append(k_s); outs["v_s"].append(v_s)
        y_p = _mlp(y_p, norm_mlp[i], w_up, w_down, i)
        y_s = _mlp(y_s, norm_mlp[i], w_up, w_down, i)
    st = lambda key: jnp.stack(outs[key])
    return (y_p.reshape(n_p, t_p, d), y_s.reshape(n_s, t_s, d),
            st("re_p"), st("im_p"), st("re_s"), st("im_s"), st("cv_p"), st("cv_s"),
            st("k_p"), st("v_p"), st("k_s"), st("v_s"))
```
